```python
import jax, jax.numpy as jnp
from jax import lax
import numpy as np

D_MODEL = 1024
BATCH = 4
SEQ = 4096
DEPTH = 4

CHUNK = 64
MEM_LEN = 256

POOL_GROUPS = 4
POOL_WIDTH = 512
POOL_GROUP_DIM = POOL_WIDTH // POOL_GROUPS
POOL_WINDOWS = (2, 4, 8, 16)
CONV_WIDTH = 512
DW_WIDTH = 31
ATTN_HEADS = 4
ATTN_HEAD_DIM = 128
ATTN_WIDTH = ATTN_HEADS * ATTN_HEAD_DIM
N_BRANCH = 3
COL_POOL = POOL_WIDTH
COL_CONV = COL_POOL + 2 * CONV_WIDTH
COL_Q = COL_CONV + ATTN_WIDTH
IN_COLS = COL_Q + N_BRANCH * D_MODEL
N_EXPERTS = 32
TOP_K = 4
D_FF = D_MODEL
SWIGLU_LIMIT = 7.0
SWIGLU_ALPHA = 1.702
MOE_BLOCK = 256
LN_EPS = 1e-5
DEEPNORM_ALPHA = (2.0 * DEPTH) ** 0.25
DEEPNORM_BETA = (8.0 * DEPTH) ** -0.25

kernel_name = "hybrid_pool_conv_memattn_moe_deepnorm"


def layer_norm(x, g, b):
    xf = x.astype(jnp.float32)
    mu = jnp.mean(xf, axis=-1, keepdims=True)
    var = jnp.mean(jnp.square(xf - mu), axis=-1, keepdims=True)
    y = (xf - mu) * lax.rsqrt(var + LN_EPS) * g.astype(jnp.float32) + b.astype(jnp.float32)
    return y.astype(x.dtype)


def multiscale_pool(u, w_grp, scale):
    b_, s_, _ = u.shape
    uf = u.astype(jnp.float32)
    cs = jnp.concatenate([jnp.zeros((b_, 1, POOL_WIDTH), jnp.float32),
                          jnp.cumsum(uf, axis=1)], axis=1)
    t = jnp.arange(s_)
    outs = []
    for g, w in enumerate(POOL_WINDOWS):
        sl = slice(g * POOL_GROUP_DIM, (g + 1) * POOL_GROUP_DIM)
        lo = jnp.maximum(t + 1 - w, 0)
        win_sum = cs[:, 1:, sl] - cs[:, lo, sl]
        cnt = (t + 1 - lo).astype(jnp.float32)[None, :, None]
        outs.append(win_sum / cnt - uf[:, :, sl])
    pooled = jnp.stack(outs, axis=2).astype(u.dtype)
    mixed = jnp.einsum('bsgc,gcd->bsgd', pooled, w_grp).reshape(b_, s_, POOL_WIDTH)
    return mixed * scale


def conformer_conv(u, w_dw, b_dw, g_ln, b_ln, w_pw, b_pw):
    a, gate = jnp.split(u, 2, axis=-1)
    h = a * jax.nn.sigmoid(gate)
    h = lax.conv_general_dilated(
        h, w_dw[:, None, :].astype(h.dtype), window_strides=(1,),
        padding=[(DW_WIDTH - 1, 0)], dimension_numbers=('NWC', 'WIO', 'NWC'),
        feature_group_count=CONV_WIDTH) + b_dw
    h = jax.nn.silu(layer_norm(h, g_ln, b_ln))
    return h @ w_pw + b_pw


def memory_attention(q, k, v):
    s = jnp.einsum('bshd,bmhd->bhsm', q, k).astype(jnp.float32) * (ATTN_HEAD_DIM ** -0.5)
    p = jax.nn.softmax(s, axis=-1).astype(v.dtype)
    return jnp.einsum('bhsm,bmhd->bshd', p, v)


def moe_ffn(x, w_router, b_router, w_up, b_up, w_down, b_down):
    b_, s_, d_ = x.shape
    n_tok = b_ * s_
    n_asg = n_tok * TOP_K
    xt = x.reshape(n_tok, d_)
    logits = (xt @ w_router).astype(jnp.float32) + b_router.astype(jnp.float32)
    top_logit, top_e = lax.top_k(logits, TOP_K)
    gate = jax.nn.softmax(top_logit, axis=-1)
    flat_e = top_e.reshape(-1)
    flat_tok = jnp.arange(n_asg, dtype=jnp.int32) // TOP_K
    flat_gate = gate.reshape(-1)
    order = jnp.argsort(flat_e)
    sorted_e = flat_e[order]
    counts = jnp.bincount(flat_e, length=N_EXPERTS)
    padded = (counts + MOE_BLOCK - 1) // MOE_BLOCK * MOE_BLOCK
    start = jnp.cumsum(counts) - counts
    pend = jnp.cumsum(padded)
    pstart = pend - padded
    dest = pstart[sorted_e] + (jnp.arange(n_asg, dtype=jnp.int32) - start[sorted_e])
    n_blocks = -(-n_asg // MOE_BLOCK) + N_EXPERTS
    n_rows = n_blocks * MOE_BLOCK
    row_tok = jnp.zeros((n_rows,), jnp.int32).at[dest].set(flat_tok[order])
    row_gate = jnp.zeros((n_rows,), jnp.float32).at[dest].set(flat_gate[order])
    block_e = jnp.minimum(
        jnp.searchsorted(pend, jnp.arange(n_blocks) * MOE_BLOCK, side='right'),
        N_EXPERTS - 1)
    xb = xt[row_tok].reshape(n_blocks, MOE_BLOCK, d_)

    def expert_block(args):
        xblk, e = args
        h = xblk @ w_up[e] + b_up[e]
        glu, lin = jnp.split(h, 2, axis=-1)
        glu = jnp.minimum(glu, SWIGLU_LIMIT)
        lin = jnp.clip(lin, -SWIGLU_LIMIT, SWIGLU_LIMIT)
        act = glu * jax.nn.sigmoid(SWIGLU_ALPHA * glu) * (lin + 1.0)
        return act @ w_down[e] + b_down[e]

    yb = lax.map(expert_block, (xb, block_e)).reshape(n_rows, d_)
    y = jnp.zeros((n_tok, d_), yb.dtype).at[row_tok].add(yb * row_gate[:, None].astype(yb.dtype))
    return y.reshape(b_, s_, d_).astype(x.dtype)


def setup_inputs(seed: int = 0) -> dict:
    key = jax.random.key(seed)
    ks = jax.random.split(key, 32)
    L, D = DEPTH, D_MODEL
    f32 = jnp.float32

    def nrm(k, shape, scale):
        return jax.random.normal(k, shape, f32) * scale

    w_kv_k = nrm(ks[9], (L, D, ATTN_WIDTH), D ** -0.5)
    w_kv_v = nrm(ks[10], (L, D, ATTN_WIDTH), D ** -0.5 * DEEPNORM_BETA)
    return {
        "x": nrm(ks[0], (BATCH, SEQ, D), 1.0),
        "mem": nrm(ks[1], (BATCH, MEM_LEN, D), 1.0),
        "mem_ln_g": 1.0 + nrm(ks[2], (D,), 0.02),
        "mem_ln_b": nrm(ks[3], (D,), 0.02),
        "w_in": nrm(ks[4], (L, D, IN_COLS), D ** -0.5),
        "b_in": nrm(ks[5], (L, IN_COLS), 0.01),
        "pool_w": nrm(ks[6], (L, POOL_GROUPS, POOL_GROUP_DIM, POOL_GROUP_DIM), POOL_GROUP_DIM ** -0.5),
        "pool_scale": 1.0 + nrm(ks[7], (L, POOL_WIDTH), 0.02),
        "pool_proj": nrm(ks[8], (L, POOL_WIDTH, D), POOL_WIDTH ** -0.5),
        "conv_dw": nrm(ks[11], (L, DW_WIDTH, CONV_WIDTH), DW_WIDTH ** -0.5),
        "conv_dw_b": nrm(ks[12], (L, CONV_WIDTH), 0.01),
        "conv_ln_g": 1.0 + nrm(ks[13], (L, CONV_WIDTH), 0.02),
        "conv_ln_b": nrm(ks[14], (L, CONV_WIDTH), 0.02),
        "conv_pw": nrm(ks[15], (L, CONV_WIDTH, D), CONV_WIDTH ** -0.5),
        "conv_pw_b": nrm(ks[16], (L, D), 0.01),
        "w_kv": jnp.concatenate([w_kv_k, w_kv_v], axis=-1),
        "attn_o": nrm(ks[17], (L, ATTN_WIDTH, D), ATTN_WIDTH ** -0.5),
        "w_out": nrm(ks[18], (L, D, D), D ** -0.5 * DEEPNORM_BETA),
        "b_out": nrm(ks[19], (L, D), 0.01),
        "ln1_g": 1.0 + nrm(ks[20], (L, D), 0.02),
        "ln1_b": nrm(ks[21], (L, D), 0.02),
        "router_w": nrm(ks[22], (L, D, N_EXPERTS), D ** -0.5),
        "router_b": nrm(ks[23], (L, N_EXPERTS), 0.01),
        "exp_up": nrm(ks[24], (L, N_EXPERTS, D, 2 * D_FF), D ** -0.5),
        "exp_up_b": nrm(ks[25], (L, N_EXPERTS, 2 * D_FF), 0.01),
        "exp_down": nrm(ks[26], (L, N_EXPERTS, D_FF, D), D_FF ** -0.5 * DEEPNORM_BETA),
        "exp_down_b": nrm(ks[27], (L, N_EXPERTS, D), 0.01),
        "ln2_g": 1.0 + nrm(ks[28], (L, D), 0.02),
        "ln2_b": nrm(ks[29], (L, D), 0.02),
    }


def reference(x, mem, mem_ln_g, mem_ln_b, w_in, b_in, pool_w, pool_scale, pool_proj,
              conv_dw, conv_dw_b, conv_ln_g, conv_ln_b, conv_pw, conv_pw_b,
              w_kv, attn_o, w_out, b_out, ln1_g, ln1_b,
              router_w, router_b, exp_up, exp_up_b, exp_down, exp_down_b,
              ln2_g, ln2_b):
    b_, s_, d_ = x.shape
    m_ = mem.shape[1]
    mem_n = layer_norm(mem, mem_ln_g, mem_ln_b)
    for l in range(DEPTH):
        h = x @ w_in[l] + b_in[l]
        u_pool = h[..., :COL_POOL]
        u_conv = h[..., COL_POOL:COL_CONV]
        q = h[..., COL_CONV:COL_Q].reshape(b_, s_, ATTN_HEADS, ATTN_HEAD_DIM)
        gates = jax.nn.sigmoid(h[..., COL_Q:].reshape(b_, s_, N_BRANCH, d_))

        y_pool = multiscale_pool(u_pool, pool_w[l], pool_scale[l]) @ pool_proj[l]
        y_conv = conformer_conv(u_conv, conv_dw[l], conv_dw_b[l], conv_ln_g[l], conv_ln_b[l],
                                conv_pw[l], conv_pw_b[l])
        kv = mem_n @ w_kv[l]
        k = kv[..., :ATTN_WIDTH].reshape(b_, m_, ATTN_HEADS, ATTN_HEAD_DIM)
        v = kv[..., ATTN_WIDTH:].reshape(b_, m_, ATTN_HEADS, ATTN_HEAD_DIM)
        y_attn = memory_attention(q, k, v).reshape(b_, s_, ATTN_WIDTH) @ attn_o[l]

        merged = (gates[:, :, 0, :] * y_pool + gates[:, :, 1, :] * y_conv
                  + gates[:, :, 2, :] * y_attn)
        x = layer_norm(DEEPNORM_ALPHA * x + (merged @ w_out[l] + b_out[l]), ln1_g[l], ln1_b[l])

        y_moe = moe_ffn(x, router_w[l], router_b[l], exp_up[l], exp_up_b[l],
                        exp_down[l], exp_down_b[l])
        x = layer_norm(DEEPNORM_ALPHA * x + y_moe, ln2_g[l], ln2_b[l])
    return x
```

```python
import functools

import jax
import jax.numpy as jnp
from jax import lax
from jax.experimental import pallas as pl
from jax.experimental.pallas import tpu as pltpu

F32 = jnp.float32
BF16 = jnp.bfloat16
I32 = jnp.int32

POOL_WINDOWS = (2, 4, 8, 16)
POOL_HALO = 16
CONV_HALO = 32
HEAD_DIM = 128
TOP_K = 4
SWIGLU_LIMIT = 7.0
SWIGLU_ALPHA = 1.702
LN_EPS = 1e-5

LANES = 128
SEG_ALIGN = 16
TOKEN_TILE = 256
EXPERT_BLOCK = 256
CONV_ROWS = 64
VMEM_LIMIT = 56 * 1024 * 1024


def _layer_norm(x, g, b):
    mu = jnp.mean(x, axis=-1, keepdims=True)
    xc = x - mu
    var = jnp.mean(xc * xc, axis=-1, keepdims=True)
    return xc * lax.rsqrt(var + LN_EPS) * g + b


def _dot(a, b):
    return jnp.dot(a, b, preferred_element_type=F32)


def _kv_kernel(mem_ref, g_ref, b_ref, wkv_ref, kt_ref, v_ref):
    a = kt_ref.shape[2]
    mem_n = _layer_norm(mem_ref[0], g_ref[...], b_ref[...]).astype(BF16)
    kv = _dot(mem_n, wkv_ref[0].astype(BF16))
    kt_ref[0, 0] = kv[:, :a].T.astype(BF16)
    v_ref[0, 0] = kv[:, a:].astype(BF16)


def _memory_kv(mem, g, b, w_kv):
    n_b, m, d = mem.shape
    n_l, _, a2 = w_kv.shape
    a = a2 // 2
    return pl.pallas_call(
        _kv_kernel,
        grid=(n_l, n_b),
        in_specs=[
            pl.BlockSpec((1, m, d), lambda l, bb: (bb, 0, 0)),
            pl.BlockSpec((1, d), lambda l, bb: (0, 0)),
            pl.BlockSpec((1, d), lambda l, bb: (0, 0)),
            pl.BlockSpec((1, d, a2), lambda l, bb: (l, 0, 0)),
        ],
        out_specs=[
            pl.BlockSpec((1, 1, a, m), lambda l, bb: (l, bb, 0, 0)),
            pl.BlockSpec((1, 1, m, a), lambda l, bb: (l, bb, 0, 0)),
        ],
        out_shape=[
            jax.ShapeDtypeStruct((n_l, n_b, a, m), BF16),
            jax.ShapeDtypeStruct((n_l, n_b, m, a), BF16),
        ],
        compiler_params=pltpu.CompilerParams(
            dimension_semantics=("arbitrary", "arbitrary"), vmem_limit_bytes=VMEM_LIMIT),
        name="memory_kv",
    )(mem, g.reshape(1, d), b.reshape(1, d), w_kv)


def _mixer_kernel(x_ref, w_in_ref, b_in_ref, pool_w_ref, pool_scale_ref, pool_proj_ref,
                  dw_ref, dwb_ref, cg_ref, cb_ref, cpw_ref, cpwb_ref,
                  kt_ref, v_ref, ao_ref, wout_ref, bout_ref, g1_ref, b1_ref,
                  wr_ref, rb_ref,
                  x1_ref, ids_ref, gates_ref, cnt_ref,
                  pool_ext, conv_ext, conv_out, *, alpha):
    i = pl.program_id(1)
    ts, d = x_ref.shape[1], x_ref.shape[2]
    pw = pool_ext.shape[1]
    cw = conv_ext.shape[1]
    aw = kt_ref.shape[1]
    n_e = wr_ref.shape[0]
    c_conv = pw
    c_q = pw + 2 * cw
    c_gate = c_q + aw

    @pl.when(i == 0)
    def _():
        pool_ext[0:POOL_HALO, :] = jnp.zeros((POOL_HALO, pw), F32)
        conv_ext[0:CONV_HALO, :] = jnp.zeros((CONV_HALO, cw), F32)

    x = x_ref[0]
    xb = x.astype(BF16)

    def proj(lo, hi):
        return _dot(xb, w_in_ref[:, lo:hi]) + b_in_ref[:, lo:hi]

    u = proj(0, pw)
    pool_ext[POOL_HALO:POOL_HALO + ts, :] = u
    t_glob = i * ts + lax.broadcasted_iota(I32, (ts, 1), 0)
    gd = pw // len(POOL_WINDOWS)
    mixed = []
    for g, w in enumerate(POOL_WINDOWS):
        lo = g * gd
        ug = u[:, lo:lo + gd]
        acc = ug
        for k in range(1, w):
            acc = acc + pool_ext[POOL_HALO - k:POOL_HALO - k + ts, lo:lo + gd]
        cnt = jnp.minimum(t_glob + 1, w).astype(F32)
        pooled = acc / cnt - ug
        mixed.append(_dot(pooled.astype(BF16), pool_w_ref[g]))
    mixed = jnp.concatenate(mixed, axis=1) * pool_scale_ref[...]
    y_pool = _dot(mixed.astype(BF16), pool_proj_ref[...])
    pool_ext[0:POOL_HALO, :] = pool_ext[ts:ts + POOL_HALO, :]

    glu = proj(c_conv, c_conv + cw) * jax.nn.sigmoid(proj(c_conv + cw, c_conv + 2 * cw))
    conv_ext[CONV_HALO:CONV_HALO + ts, :] = glu
    taps = dw_ref.shape[0]
    base = CONV_HALO - (taps - 1)
    for r0 in range(0, ts, CONV_ROWS):
        for c0 in range(0, cw, LANES):
            acc = jnp.broadcast_to(dwb_ref[:, c0:c0 + LANES], (CONV_ROWS, LANES))
            for k in range(taps):
                acc = acc + dw_ref[k:k + 1, c0:c0 + LANES] * conv_ext[
                    base + k + r0:base + k + r0 + CONV_ROWS, c0:c0 + LANES]
            conv_out[r0:r0 + CONV_ROWS, c0:c0 + LANES] = acc
    conv_ext[0:CONV_HALO, :] = conv_ext[ts:ts + CONV_HALO, :]
    hc = _layer_norm(conv_out[...], cg_ref[...], cb_ref[...])
    hc = hc * jax.nn.sigmoid(hc)
    y_conv = _dot(hc.astype(BF16), cpw_ref[...]) + cpwb_ref[...]

    q = proj(c_q, c_q + aw)
    heads = []
    for h in range(aw // HEAD_DIM):
        lo = h * HEAD_DIM
        s = _dot(q[:, lo:lo + HEAD_DIM].astype(BF16), kt_ref[0, lo:lo + HEAD_DIM, :])
        s = s * (HEAD_DIM ** -0.5)
        p = jnp.exp(s - jnp.max(s, axis=-1, keepdims=True))
        o = _dot(p.astype(BF16), v_ref[0, :, lo:lo + HEAD_DIM])
        heads.append(o / jnp.sum(p, axis=-1, keepdims=True))
    y_attn = _dot(jnp.concatenate(heads, axis=1).astype(BF16), ao_ref[...])

    merged = jax.nn.sigmoid(proj(c_gate, c_gate + d)) * y_pool
    merged = merged + jax.nn.sigmoid(proj(c_gate + d, c_gate + 2 * d)) * y_conv
    merged = merged + jax.nn.sigmoid(proj(c_gate + 2 * d, c_gate + 3 * d)) * y_attn
    out = _dot(merged.astype(BF16), wout_ref[...]) + bout_ref[...]
    x1 = _layer_norm(alpha * x + out, g1_ref[...], b1_ref[...])
    x1_ref[0] = x1

    logits = lax.dot_general(wr_ref[...], x1, (((1,), (1,)), ((), ())),
                             precision=lax.Precision.HIGHEST,
                             preferred_element_type=F32) + rb_ref[...]
    e_iota = lax.broadcasted_iota(I32, (n_e, ts), 0)
    top_v, top_i = [], []
    hot = jnp.zeros((n_e, ts), F32)
    for _ in range(TOP_K):
        m = jnp.max(logits, axis=0, keepdims=True)
        idx = jnp.min(jnp.where(logits == m, e_iota, n_e), axis=0, keepdims=True)
        sel = e_iota == idx
        hot = hot + sel.astype(F32)
        logits = jnp.where(sel, -jnp.inf, logits)
        top_v.append(m)
        top_i.append(idx)
    ex = [jnp.exp(v - top_v[0]) for v in top_v]
    den = ex[0] + ex[1] + ex[2] + ex[3]
    for k in range(TOP_K):
        ids_ref[k:k + 1, :] = top_i[k]
        gates_ref[k:k + 1, :] = ex[k] / den
    cnt_ref[0] = jnp.broadcast_to(jnp.sum(hot, axis=1, keepdims=True), (n_e, LANES)).astype(I32)


def _mixer(x, lw, kt, v, alpha):
    n_b, s, d = x.shape
    ts = TOKEN_TILE
    n_s = s // ts
    n_t = n_b * n_s
    pw = lw["pool_proj"].shape[0]
    cw = lw["conv_pw"].shape[0]
    n_e = lw["router_wt"].shape[0]

    def full(arr):
        zeros = (0,) * arr.ndim
        return pl.BlockSpec(arr.shape, lambda bb, i: zeros, pipeline_mode=pl.Buffered(1))

    weights = [lw[k] for k in ("w_in", "b_in", "pool_w", "pool_scale", "pool_proj",
                               "conv_dw", "conv_dw_b", "conv_ln_g", "conv_ln_b",
                               "conv_pw", "conv_pw_b")]
    weights2 = [lw[k] for k in ("attn_o", "w_out", "b_out", "ln1_g", "ln1_b",
                                "router_wt", "router_b")]
    in_specs = ([pl.BlockSpec((1, ts, d), lambda bb, i: (bb, i, 0))]
                + [full(w) for w in weights]
                + [pl.BlockSpec((1,) + kt.shape[1:], lambda bb, i: (bb, 0, 0)),
                   pl.BlockSpec((1,) + v.shape[1:], lambda bb, i: (bb, 0, 0))]
                + [full(w) for w in weights2])
    out_specs = [
        pl.BlockSpec((1, ts, d), lambda bb, i: (bb, i, 0)),
        pl.BlockSpec((TOP_K, ts), lambda bb, i: (0, bb * n_s + i)),
        pl.BlockSpec((TOP_K, ts), lambda bb, i: (0, bb * n_s + i)),
        pl.BlockSpec((1, n_e, LANES), lambda bb, i: (bb * n_s + i, 0, 0)),
    ]
    out_shape = [
        jax.ShapeDtypeStruct((n_b, s, d), F32),
        jax.ShapeDtypeStruct((TOP_K, n_b * s), I32),
        jax.ShapeDtypeStruct((TOP_K, n_b * s), F32),
        jax.ShapeDtypeStruct((n_t, n_e, LANES), I32),
    ]
    return pl.pallas_call(
        functools.partial(_mixer_kernel, alpha=alpha),
        grid=(n_b, n_s),
        in_specs=in_specs,
        out_specs=out_specs,
        out_shape=out_shape,
        scratch_shapes=[
            pltpu.VMEM((ts + POOL_HALO, pw), F32),
            pltpu.VMEM((ts + CONV_HALO, cw), F32),
            pltpu.VMEM((ts, cw), F32),
        ],
        compiler_params=pltpu.CompilerParams(
            dimension_semantics=("arbitrary", "arbitrary"), vmem_limit_bytes=VMEM_LIMIT),
        name="mixer",
    )(x, *weights, kt, v, *weights2)


def _segment_copies(seg_off_ref, seg_len_ref, dest_ref, tile, n_e, make_copy):
    def start(e, c):
        n = pl.multiple_of(seg_len_ref[tile * n_e + e], SEG_ALIGN)

        @pl.when(n > 0)
        def _():
            make_copy(pl.multiple_of(seg_off_ref[tile * n_e + e], SEG_ALIGN),
                      pl.multiple_of(dest_ref[tile * n_e + e], SEG_ALIGN), n).start()
        return c

    def wait(e, c):
        n = pl.multiple_of(seg_len_ref[tile * n_e + e], SEG_ALIGN)

        @pl.when(n > 0)
        def _():
            make_copy(pl.multiple_of(seg_off_ref[tile * n_e + e], SEG_ALIGN),
                      pl.multiple_of(dest_ref[tile * n_e + e], SEG_ALIGN), n).wait()
        return c

    return start, wait


def _dispatch_kernel(seg_off_ref, seg_len_ref, dest_ref, pad_dst_ref, pad_len_ref,
                     ids_ref, off_col_ref, x1_ref, xs_ref, buf, zbuf, sem, zsem):
    tile = pl.program_id(0)
    ts = x1_ref.shape[0]
    rows = buf.shape[0]
    n_e = off_col_ref.shape[1]

    @pl.when(tile == 0)
    def _():
        zbuf[...] = jnp.zeros(zbuf.shape, BF16)

        def zero_copy(e):
            n = pl.multiple_of(pad_len_ref[e], SEG_ALIGN)
            dst = pl.multiple_of(pad_dst_ref[e], SEG_ALIGN)
            return n, pltpu.make_async_copy(zbuf.at[pl.ds(0, n)], xs_ref.at[pl.ds(dst, n)], zsem)

        def zstart(e, c):
            n, cp = zero_copy(e)
            pl.when(n > 0)(cp.start)
            return c

        def zwait(e, c):
            n, cp = zero_copy(e)
            pl.when(n > 0)(cp.wait)
            return c

        lax.fori_loop(0, n_e, zstart, 0)
        lax.fori_loop(0, n_e, zwait, 0)

    ids = ids_ref[...]
    e_iota = lax.broadcasted_iota(I32, (n_e, ts), 0)
    hots = [(e_iota == ids[k:k + 1, :]).astype(F32) for k in range(TOP_K)]
    hot = hots[0] + hots[1] + hots[2] + hots[3]
    upper = (lax.broadcasted_iota(I32, (ts, ts), 0)
             < lax.broadcasted_iota(I32, (ts, ts), 1)).astype(BF16)
    slot = _dot(hot.astype(BF16), upper) + off_col_ref[0][:, 0:1].astype(F32)
    r_iota = lax.broadcasted_iota(I32, (rows, ts), 0)
    perm = jnp.zeros((rows, ts), F32)
    for k in range(TOP_K):
        pos = jnp.sum(hots[k] * slot, axis=0, keepdims=True).astype(I32)
        perm = perm + (r_iota == pos).astype(F32)
    buf[...] = _dot(perm.astype(BF16), x1_ref[...].astype(BF16)).astype(BF16)

    def make_copy(off, dst, n):
        return pltpu.make_async_copy(buf.at[pl.ds(off, n)], xs_ref.at[pl.ds(dst, n)], sem)

    start, wait = _segment_copies(seg_off_ref, seg_len_ref, dest_ref, tile, n_e, make_copy)
    lax.fori_loop(0, n_e, start, 0)
    lax.fori_loop(0, n_e, wait, 0)


def _tile_rows(ts, n_e):
    return TOP_K * ts + n_e * SEG_ALIGN


def _dispatch(tables, pad_tables, ids, off_lanes, x1, n_rows):
    n_tok, d = x1.shape
    ts = TOKEN_TILE
    n_t = n_tok // ts
    n_e = off_lanes.shape[1]
    rows = _tile_rows(ts, n_e)
    return pl.pallas_call(
        _dispatch_kernel,
        grid_spec=pltpu.PrefetchScalarGridSpec(
            num_scalar_prefetch=5,
            grid=(n_t,),
            in_specs=[
                pl.BlockSpec((TOP_K, ts), lambda t, *_: (0, t)),
                pl.BlockSpec((1, n_e, LANES), lambda t, *_: (t, 0, 0)),
                pl.BlockSpec((ts, d), lambda t, *_: (t, 0)),
            ],
            out_specs=pl.BlockSpec(memory_space=pl.ANY),
            scratch_shapes=[pltpu.VMEM((rows, d), BF16), pltpu.VMEM((EXPERT_BLOCK, d), BF16),
                            pltpu.SemaphoreType.DMA(()), pltpu.SemaphoreType.DMA(())],
        ),
        out_shape=jax.ShapeDtypeStruct((n_rows, d), BF16),
        compiler_params=pltpu.CompilerParams(
            dimension_semantics=("arbitrary",), vmem_limit_bytes=VMEM_LIMIT),
        name="dispatch",
    )(*tables, *pad_tables, ids, off_lanes, x1)


def _expert_kernel(blk_e_ref, n_used_ref, xs_ref, wup_ref, bup_ref, wdn_ref, bdn_ref,
                   ys_ref, wup_bf, wdn_bf):
    j = pl.program_id(0)
    f = wdn_ref.shape[1]

    @pl.when(j < n_used_ref[0])
    def _():
        prev = blk_e_ref[jnp.maximum(j - 1, 0)]

        @pl.when((j == 0) | (blk_e_ref[j] != prev))
        def _():
            wup_bf[...] = wup_ref[0].astype(BF16)
            wdn_bf[...] = wdn_ref[0].astype(BF16)

        h = _dot(xs_ref[...], wup_bf[...]) + bup_ref[0]
        glu = jnp.minimum(h[:, :f], SWIGLU_LIMIT)
        lin = jnp.clip(h[:, f:], -SWIGLU_LIMIT, SWIGLU_LIMIT)
        act = glu * jax.nn.sigmoid(SWIGLU_ALPHA * glu) * (lin + 1.0)
        ys_ref[...] = (_dot(act.astype(BF16), wdn_bf[...]) + bdn_ref[0]).astype(BF16)


def _experts(blk_e, n_used, xs, w_up, b_up, w_down, b_down):
    n_rows, d = xs.shape
    n_e, _, f2 = w_up.shape
    f = f2 // 2
    bm = EXPERT_BLOCK
    n_blk = n_rows // bm

    def row_blk(j, blk_e, n_used):
        return (jnp.minimum(j, n_used[0] - 1), 0)

    def w_blk(j, blk_e, n_used):
        return (blk_e[jnp.minimum(j, n_used[0] - 1)], 0, 0)

    return pl.pallas_call(
        _expert_kernel,
        grid_spec=pltpu.PrefetchScalarGridSpec(
            num_scalar_prefetch=2,
            grid=(n_blk,),
            in_specs=[
                pl.BlockSpec((bm, d), row_blk),
                pl.BlockSpec((1, d, f2), w_blk),
                pl.BlockSpec((1, 1, f2), w_blk),
                pl.BlockSpec((1, f, d), w_blk),
                pl.BlockSpec((1, 1, d), w_blk),
            ],
            out_specs=pl.BlockSpec((bm, d), row_blk),
            scratch_shapes=[pltpu.VMEM((d, f2), BF16), pltpu.VMEM((f, d), BF16)],
        ),
        out_shape=jax.ShapeDtypeStruct((n_rows, d), BF16),
        compiler_params=pltpu.CompilerParams(
            dimension_semantics=("arbitrary",), vmem_limit_bytes=VMEM_LIMIT),
        name="experts",
    )(blk_e, n_used, xs, w_up, b_up.reshape(n_e, 1, f2), w_down, b_down.reshape(n_e, 1, d))


def _combine_kernel(seg_off_ref, seg_len_ref, dest_ref,
                    ids_ref, gates_ref, off_row_ref, x1_ref, g2_ref, b2_ref, ys_ref,
                    out_ref, buf, sem, *, alpha, n_e):
    tile = pl.program_id(0)
    ts = x1_ref.shape[0]
    rows = buf.shape[0]

    @pl.when(tile == 0)
    def _():
        buf[...] = jnp.zeros(buf.shape, BF16)

    def make_copy(off, dst, n):
        return pltpu.make_async_copy(ys_ref.at[pl.ds(dst, n)], buf.at[pl.ds(off, n)], sem)

    start, wait = _segment_copies(seg_off_ref, seg_len_ref, dest_ref, tile, n_e, make_copy)
    lax.fori_loop(0, n_e, start, 0)

    ids = ids_ref[...]
    gates = gates_ref[...]
    l_iota = lax.broadcasted_iota(I32, (ts, LANES), 1)
    hots = [(l_iota == ids[:, k:k + 1]).astype(F32) for k in range(TOP_K)]
    hot = hots[0] + hots[1] + hots[2] + hots[3]
    lower = (lax.broadcasted_iota(I32, (ts, ts), 1)
             < lax.broadcasted_iota(I32, (ts, ts), 0)).astype(BF16)
    slot = _dot(lower, hot.astype(BF16)) + off_row_ref[0].astype(F32)
    r_iota = lax.broadcasted_iota(I32, (ts, rows), 1)
    weight = jnp.zeros((ts, rows), F32)
    for k in range(TOP_K):
        pos = jnp.sum(hots[k] * slot, axis=1, keepdims=True).astype(I32)
        weight = weight + jnp.where(r_iota == pos, gates[:, k:k + 1], 0.0)

    lax.fori_loop(0, n_e, wait, 0)
    y = _dot(weight.astype(BF16), buf[...])
    out_ref[...] = _layer_norm(alpha * x1_ref[...] + y, g2_ref[...], b2_ref[...])


def _combine(tables, ids_t, gates_t, off_rows, x1, g2, b2, ys, alpha):
    n_tok, d = x1.shape
    ts = TOKEN_TILE
    n_t = n_tok // ts
    n_e = tables[0].shape[0] // n_t
    rows = _tile_rows(ts, n_e)
    return pl.pallas_call(
        functools.partial(_combine_kernel, alpha=alpha, n_e=n_e),
        grid_spec=pltpu.PrefetchScalarGridSpec(
            num_scalar_prefetch=3,
            grid=(n_t,),
            in_specs=[
                pl.BlockSpec((ts, TOP_K), lambda t, *_: (t, 0)),
                pl.BlockSpec((ts, TOP_K), lambda t, *_: (t, 0)),
                pl.BlockSpec((1, 1, LANES), lambda t, *_: (t, 0, 0)),
                pl.BlockSpec((ts, d), lambda t, *_: (t, 0)),
                pl.BlockSpec((1, d), lambda t, *_: (0, 0)),
                pl.BlockSpec((1, d), lambda t, *_: (0, 0)),
                pl.BlockSpec(memory_space=pl.ANY),
            ],
            out_specs=pl.BlockSpec((ts, d), lambda t, *_: (t, 0)),
            scratch_shapes=[pltpu.VMEM((rows, d), BF16), pltpu.SemaphoreType.DMA(())],
        ),
        out_shape=jax.ShapeDtypeStruct((n_tok, d), F32),
        compiler_params=pltpu.CompilerParams(
            dimension_semantics=("arbitrary",), vmem_limit_bytes=VMEM_LIMIT),
        name="combine",
    )(*tables, ids_t, gates_t, off_rows, x1, g2.reshape(1, d), b2.reshape(1, d), ys)


def _routing_tables(counts, n_rows):
    n_t, n_e = counts.shape
    padded = (counts + SEG_ALIGN - 1) // SEG_ALIGN * SEG_ALIGN
    seg_off = jnp.cumsum(padded, axis=1) - padded
    tot = jnp.sum(padded, axis=0)
    ptot = (tot + EXPERT_BLOCK - 1) // EXPERT_BLOCK * EXPERT_BLOCK
    e_end = jnp.cumsum(ptot)
    e_start = e_end - ptot
    dest = e_start[None, :] + jnp.cumsum(padded, axis=0) - padded
    n_blk = n_rows // EXPERT_BLOCK
    blk_e = jnp.minimum(
        jnp.searchsorted(e_end, jnp.arange(n_blk, dtype=I32) * EXPERT_BLOCK, side="right"),
        n_e - 1).astype(I32)
    n_used = (e_end[-1] // EXPERT_BLOCK).astype(I32).reshape(1)
    tables = (seg_off.reshape(-1).astype(I32), padded.reshape(-1).astype(I32),
              dest.reshape(-1).astype(I32))
    pad_tables = ((e_start + tot).astype(I32), (ptot - tot).astype(I32))
    return tables, pad_tables, seg_off.astype(I32), blk_e, n_used


def kernel(x, mem, mem_ln_g, mem_ln_b, w_in, b_in, pool_w, pool_scale, pool_proj, conv_dw, conv_dw_b, conv_ln_g, conv_ln_b, conv_pw, conv_pw_b, w_kv, attn_o, w_out, b_out, ln1_g, ln1_b, router_w, router_b, exp_up, exp_up_b, exp_down, exp_down_b, ln2_g, ln2_b):
    n_b, s, d = x.shape
    depth = w_in.shape[0]
    n_e = router_w.shape[-1]
    n_tok = n_b * s
    n_t = n_tok // TOKEN_TILE
    alpha = (2.0 * depth) ** 0.25
    n_rows = TOP_K * n_tok + n_t * n_e * SEG_ALIGN + n_e * EXPERT_BLOCK
    n_rows = -(-n_rows // EXPERT_BLOCK) * EXPERT_BLOCK

    kt_all, v_all = _memory_kv(mem, mem_ln_g, mem_ln_b, w_kv)
    w_in_b, pool_w_b, pool_proj_b = w_in.astype(BF16), pool_w.astype(BF16), pool_proj.astype(BF16)
    conv_pw_b16, attn_o_b, w_out_b = conv_pw.astype(BF16), attn_o.astype(BF16), w_out.astype(BF16)
    router_wt = jnp.swapaxes(router_w, 1, 2)

    def row(a):
        return a.reshape(1, -1)

    for l in range(depth):
        lw = dict(
            w_in=w_in_b[l], b_in=row(b_in[l]), pool_w=pool_w_b[l], pool_scale=row(pool_scale[l]),
            pool_proj=pool_proj_b[l], conv_dw=conv_dw[l], conv_dw_b=row(conv_dw_b[l]),
            conv_ln_g=row(conv_ln_g[l]), conv_ln_b=row(conv_ln_b[l]), conv_pw=conv_pw_b16[l],
            conv_pw_b=row(conv_pw_b[l]), attn_o=attn_o_b[l], w_out=w_out_b[l], b_out=row(b_out[l]),
            ln1_g=row(ln1_g[l]), ln1_b=row(ln1_b[l]), router_wt=router_wt[l],
            router_b=router_b[l].reshape(n_e, 1))
        x1, ids, gates, cnt = _mixer(x, lw, kt_all[l], v_all[l], alpha)
        x1 = x1.reshape(n_tok, d)
        tables, pad_tables, seg_off, blk_e, n_used = _routing_tables(cnt[:, :, 0], n_rows)
        off_lanes = jnp.broadcast_to(seg_off[:, :, None], (n_t, n_e, LANES))
        off_rows = jnp.pad(seg_off, ((0, 0), (0, LANES - n_e))).reshape(n_t, 1, LANES)
        xs = _dispatch(tables, pad_tables, ids, off_lanes, x1, n_rows)
        ys = _experts(blk_e, n_used, xs, exp_up[l], exp_up_b[l], exp_down[l], exp_down_b[l])
        x = _combine(tables, ids.T, gates.T, off_rows, x1, ln2_g[l], ln2_b[l], ys, alpha)
        x = x.reshape(n_b, s, d)
    return x
```

```python
import functools

import jax
import jax.numpy as jnp
from jax import lax
from jax.experimental import pallas as pl
from jax.experimental.pallas import tpu as pltpu

F32 = jnp.float32
BF16 = jnp.bfloat16
I32 = jnp.int32

POOL_WINDOWS = (2, 4, 8, 16)
POOL_HALO = 16
CONV_HALO = 32
HEAD_DIM = 128
TOP_K = 4
SWIGLU_LIMIT = 7.0
SWIGLU_ALPHA = 1.702
LN_EPS = 1e-5

LANES = 128
SUBLANES = 8
SEG_ALIGN = 16
TOKEN_TILE = 256
EXPERT_BLOCK = 256
FF_CHUNK = 256
CONV_ROWS = 64
GATE_CHUNK = 256
META_LANES = 512
META_BLK_E, META_PAD_DST, META_PAD_LEN, META_N_USED = 0, 1, 2, 3
VMEM_LIMIT = 56 * 1024 * 1024


def _layer_norm(x, g, b):
    mu = jnp.mean(x, axis=-1, keepdims=True)
    xc = x - mu
    var = jnp.mean(xc * xc, axis=-1, keepdims=True)
    return xc * lax.rsqrt(var + LN_EPS) * g + b


def _dot(a, b):
    return jnp.dot(a, b, preferred_element_type=F32)


def _dot_exact(a, b):
    return jnp.dot(a, b, preferred_element_type=F32, precision=lax.Precision.HIGHEST)


def _kv_kernel(mem_ref, g_ref, b_ref, wkv_ref, kt_ref, v_ref):
    a = kt_ref.shape[2]
    mem_n = _layer_norm(mem_ref[0], g_ref[...], b_ref[...]).astype(BF16)
    kv = _dot(mem_n, wkv_ref[0].astype(BF16))
    kt_ref[0, 0] = kv[:, :a].T.astype(BF16)
    v_ref[0, 0] = kv[:, a:].astype(BF16)


def _memory_kv(mem, g, b, w_kv):
    n_b, m, d = mem.shape
    n_l, _, a2 = w_kv.shape
    a = a2 // 2
    return pl.pallas_call(
        _kv_kernel,
        grid=(n_l, n_b),
        in_specs=[
            pl.BlockSpec((1, m, d), lambda l, bb: (bb, 0, 0)),
            pl.BlockSpec((1, d), lambda l, bb: (0, 0)),
            pl.BlockSpec((1, d), lambda l, bb: (0, 0)),
            pl.BlockSpec((1, d, a2), lambda l, bb: (l, 0, 0)),
        ],
        out_specs=[
            pl.BlockSpec((1, 1, a, m), lambda l, bb: (l, bb, 0, 0)),
            pl.BlockSpec((1, 1, m, a), lambda l, bb: (l, bb, 0, 0)),
        ],
        out_shape=[
            jax.ShapeDtypeStruct((n_l, n_b, a, m), BF16),
            jax.ShapeDtypeStruct((n_l, n_b, m, a), BF16),
        ],
        compiler_params=pltpu.CompilerParams(
            dimension_semantics=("arbitrary", "arbitrary"), vmem_limit_bytes=VMEM_LIMIT),
        name="memory_kv",
    )(mem, g.reshape(1, d), b.reshape(1, d), w_kv)


def _routing_tables(padded, prefix, n_e, seg_off_ref, seg_len_ref, dest_ref, meta_ref):
    row = lax.broadcasted_iota(I32, (LANES, LANES), 0)
    col = lax.broadcasted_iota(I32, (LANES, LANES), 1)
    seg_off = _dot_exact(padded, (row < col).astype(F32))
    tot = jnp.sum(padded, axis=0, keepdims=True)
    ptot = jnp.ceil(tot * (1.0 / EXPERT_BLOCK)) * EXPERT_BLOCK
    e_end = _dot_exact(jnp.broadcast_to(ptot, (SUBLANES, LANES)),
                       (row <= col).astype(F32))[0:1]
    e_start = e_end - ptot
    seg_off_ref[...] = seg_off.astype(I32)
    seg_len_ref[...] = padded.astype(I32)
    dest_ref[...] = (prefix + e_start).astype(I32)

    e_end_col = jnp.sum(jnp.where(row == col, jnp.broadcast_to(e_end, (LANES, LANES)), 0.0),
                        axis=1, keepdims=True)
    blk_row = (lax.broadcasted_iota(I32, (LANES, META_LANES), 1) * EXPERT_BLOCK).astype(F32)
    is_real = lax.broadcasted_iota(I32, (LANES, META_LANES), 0) < n_e
    blk_e = jnp.sum(jnp.where(is_real & (e_end_col <= blk_row), 1.0, 0.0), axis=0, keepdims=True)
    blk_e = jnp.minimum(blk_e, n_e - 1.0)
    lane = lax.broadcasted_iota(I32, (1, LANES), 1)
    n_used = jnp.sum(jnp.where(lane == n_e - 1, e_end, 0.0), axis=1, keepdims=True) * (
        1.0 / EXPERT_BLOCK)

    def wide(r):
        return jnp.concatenate([r, jnp.zeros((1, META_LANES - LANES), F32)], axis=1)

    sub = lax.broadcasted_iota(I32, (SUBLANES, META_LANES), 0)
    meta = jnp.zeros((SUBLANES, META_LANES), F32)
    for r, val in ((META_BLK_E, blk_e), (META_PAD_DST, wide(e_start + tot)),
                   (META_PAD_LEN, wide(ptot - tot)),
                   (META_N_USED, jnp.broadcast_to(n_used, (1, META_LANES)))):
        meta = jnp.where(sub == r, jnp.broadcast_to(val, (SUBLANES, META_LANES)), meta)
    meta_ref[...] = meta.astype(I32)


def _mixer_kernel(x_ref, w_in_ref, b_in_ref, pool_w_ref, pool_scale_ref, pool_proj_ref,
                  dw_ref, dwb_ref, cg_ref, cb_ref, cpw_ref, cpwb_ref,
                  kt_ref, v_ref, ao_ref, wout_ref, bout_ref, g1_ref, b1_ref,
                  wr_ref, rb_ref,
                  x1_ref, ids_ref, route_t_ref, seg_off_ref, seg_len_ref, dest_ref, meta_ref,
                  pool_ext, conv_ext, conv_out, gate_scr, cnt_scr, pre_scr, run_scr, *, alpha):
    i = pl.program_id(1)
    tile = pl.program_id(0) * pl.num_programs(1) + i
    n_tiles = pl.num_programs(0) * pl.num_programs(1)
    ts, d = x_ref.shape[1], x_ref.shape[2]
    pw = pool_ext.shape[1]
    cw = conv_ext.shape[1]
    aw = kt_ref.shape[2]
    n_e = wr_ref.shape[1]
    c_conv = pw
    c_q = pw + 2 * cw
    c_gate = c_q + aw

    @pl.when(i == 0)
    def _():
        pool_ext[0:POOL_HALO, :] = jnp.zeros((POOL_HALO, pw), F32)
        conv_ext[0:CONV_HALO, :] = jnp.zeros((CONV_HALO, cw), F32)

    @pl.when(tile == 0)
    def _():
        run_scr[...] = jnp.zeros(run_scr.shape, F32)

    x = x_ref[0]
    xb = x.astype(BF16)

    def proj(lo, hi):
        return _dot(xb, w_in_ref[0, :, lo:hi]) + b_in_ref[0, :, lo:hi]

    u = proj(0, pw)
    pool_ext[POOL_HALO:POOL_HALO + ts, :] = u
    t_glob = i * ts + lax.broadcasted_iota(I32, (ts, 1), 0)
    gd = pw // len(POOL_WINDOWS)
    mixed = []
    for g, w in enumerate(POOL_WINDOWS):
        lo = g * gd
        ug = u[:, lo:lo + gd]
        acc = ug
        for k in range(1, w):
            acc = acc + pool_ext[POOL_HALO - k:POOL_HALO - k + ts, lo:lo + gd]
        cnt = jnp.minimum(t_glob + 1, w).astype(F32)
        pooled = acc / cnt - ug
        mixed.append(_dot(pooled.astype(BF16), pool_w_ref[0, g]))
    mixed = jnp.concatenate(mixed, axis=1) * pool_scale_ref[0]
    y_pool = _dot(mixed.astype(BF16), pool_proj_ref[0])
    pool_ext[0:POOL_HALO, :] = pool_ext[ts:ts + POOL_HALO, :]

    glu = proj(c_conv, c_conv + cw) * jax.nn.sigmoid(proj(c_conv + cw, c_conv + 2 * cw))
    conv_ext[CONV_HALO:CONV_HALO + ts, :] = glu
    taps = dw_ref.shape[1]
    base = CONV_HALO - (taps - 1)
    span = CONV_ROWS + CONV_HALO
    gate_chunks = [(c, min(c + GATE_CHUNK, c_gate + 3 * d))
                   for c in range(c_gate, c_gate + 3 * d, GATE_CHUNK)]
    for r0 in range(0, ts, CONV_ROWS):
        for c0 in range(0, cw, LANES):
            if gate_chunks:
                lo, hi = gate_chunks.pop(0)
                gate_scr[:, lo - c_gate:hi - c_gate] = proj(lo, hi)
            window = conv_ext[r0:r0 + span, c0:c0 + LANES]
            acc = jnp.broadcast_to(dwb_ref[0, :, c0:c0 + LANES], (CONV_ROWS, LANES))
            for res in range(SUBLANES):
                offs = [o for o in range(base, base + taps) if o % SUBLANES == res]
                if not offs:
                    continue
                shifted = window if res == 0 else pltpu.roll(window, span - res, axis=0)
                for o in offs:
                    k = o - base
                    acc = acc + dw_ref[0, k:k + 1, c0:c0 + LANES] * shifted[
                        o - res:o - res + CONV_ROWS]
            conv_out[r0:r0 + CONV_ROWS, c0:c0 + LANES] = acc
    for lo, hi in gate_chunks:
        gate_scr[:, lo - c_gate:hi - c_gate] = proj(lo, hi)
    conv_ext[0:CONV_HALO, :] = conv_ext[ts:ts + CONV_HALO, :]
    hc = _layer_norm(conv_out[...], cg_ref[0], cb_ref[0])
    hc = hc * jax.nn.sigmoid(hc)
    y_conv = _dot(hc.astype(BF16), cpw_ref[0]) + cpwb_ref[0]

    q = proj(c_q, c_q + aw)
    heads = []
    for h in range(aw // HEAD_DIM):
        lo = h * HEAD_DIM
        s = _dot(q[:, lo:lo + HEAD_DIM].astype(BF16), kt_ref[0, 0, lo:lo + HEAD_DIM, :])
        s = s * (HEAD_DIM ** -0.5)
        p = jnp.exp(s - jnp.max(s, axis=-1, keepdims=True))
        o = _dot(p.astype(BF16), v_ref[0, 0, :, lo:lo + HEAD_DIM])
        heads.append(o / jnp.sum(p, axis=-1, keepdims=True))
    y_attn = _dot(jnp.concatenate(heads, axis=1).astype(BF16), ao_ref[0])

    merged = jax.nn.sigmoid(gate_scr[:, 0:d]) * y_pool
    merged = merged + jax.nn.sigmoid(gate_scr[:, d:2 * d]) * y_conv
    merged = merged + jax.nn.sigmoid(gate_scr[:, 2 * d:3 * d]) * y_attn
    out = _dot(merged.astype(BF16), wout_ref[0]) + bout_ref[0]
    x1 = _layer_norm(alpha * x + out, g1_ref[0], b1_ref[0])
    x1_ref[0] = x1

    logits = lax.dot_general(wr_ref[0], x1, (((1,), (1,)), ((), ())),
                             precision=lax.Precision.HIGHEST,
                             preferred_element_type=F32) + rb_ref[0]
    e_iota = lax.broadcasted_iota(I32, (n_e, ts), 0)
    top_v, top_i = [], []
    hot = jnp.zeros((n_e, ts), F32)
    for _ in range(TOP_K):
        m = jnp.max(logits, axis=0, keepdims=True)
        idx = jnp.min(jnp.where(logits == m, e_iota, n_e), axis=0, keepdims=True)
        sel = e_iota == idx
        hot = hot + sel.astype(F32)
        logits = jnp.where(sel, -jnp.inf, logits)
        top_v.append(m)
        top_i.append(idx)
    ex = [jnp.exp(v - top_v[0]) for v in top_v]
    den = ex[0] + ex[1] + ex[2] + ex[3]
    sub = lax.broadcasted_iota(I32, (SUBLANES, ts), 0)
    route = jnp.zeros((SUBLANES, ts), F32)
    for k in range(TOP_K):
        ids_ref[k:k + 1, :] = top_i[k]
        route = jnp.where(sub == k, top_i[k].astype(F32), route)
        route = jnp.where(sub == TOP_K + k, ex[k] / den, route)
    route_t_ref[...] = jnp.concatenate(
        [route, jnp.zeros((LANES - SUBLANES, ts), F32)], axis=0).T

    hot_wide = jnp.concatenate([hot, jnp.zeros((LANES - n_e, ts), F32)], axis=0).astype(BF16)
    cnt = lax.dot_general(jnp.ones((SUBLANES, ts), BF16), hot_wide, (((1,), (1,)), ((), ())),
                          preferred_element_type=F32)[0:1]
    padded = jnp.ceil(cnt * (1.0 / SEG_ALIGN)) * SEG_ALIGN
    cnt_scr[pl.ds(tile, 1), :] = padded
    pre_scr[pl.ds(tile, 1), :] = run_scr[0:1, :]
    run_scr[0:1, :] = run_scr[0:1, :] + padded

    @pl.when(tile == n_tiles - 1)
    def _():
        _routing_tables(cnt_scr[...], pre_scr[...], n_e,
                        seg_off_ref, seg_len_ref, dest_ref, meta_ref)


def _mixer(x, weights, kt, v, layer, alpha):
    n_b, s, d = x.shape
    ts = TOKEN_TILE
    n_s = s // ts
    n_t = n_b * n_s
    names = ("w_in", "b_in", "pool_w", "pool_scale", "pool_proj", "conv_dw", "conv_dw_b",
             "conv_ln_g", "conv_ln_b", "conv_pw", "conv_pw_b")
    names2 = ("attn_o", "w_out", "b_out", "ln1_g", "ln1_b", "router_wt", "router_b")
    pw = weights["pool_proj"].shape[1]
    cw = weights["conv_pw"].shape[1]

    def of_layer(arr):
        tail = (0,) * (arr.ndim - 1)
        return pl.BlockSpec((1,) + arr.shape[1:], lambda bb, i: (layer,) + tail,
                            pipeline_mode=pl.Buffered(1))

    def resident(shape):
        return pl.BlockSpec(shape, lambda bb, i: (0, 0))

    in_specs = ([pl.BlockSpec((1, ts, d), lambda bb, i: (bb, i, 0))]
                + [of_layer(weights[k]) for k in names]
                + [pl.BlockSpec((1, 1) + kt.shape[2:], lambda bb, i: (layer, bb, 0, 0)),
                   pl.BlockSpec((1, 1) + v.shape[2:], lambda bb, i: (layer, bb, 0, 0))]
                + [of_layer(weights[k]) for k in names2])
    out_specs = [
        pl.BlockSpec((1, ts, d), lambda bb, i: (bb, i, 0)),
        pl.BlockSpec((TOP_K, ts), lambda bb, i: (0, bb * n_s + i)),
        pl.BlockSpec((ts, LANES), lambda bb, i: (bb * n_s + i, 0)),
        resident((n_t, LANES)), resident((n_t, LANES)), resident((n_t, LANES)),
        resident((SUBLANES, META_LANES)),
    ]
    out_shape = [
        jax.ShapeDtypeStruct((n_b, s, d), F32),
        jax.ShapeDtypeStruct((TOP_K, n_b * s), I32),
        jax.ShapeDtypeStruct((n_b * s, LANES), F32),
        jax.ShapeDtypeStruct((n_t, LANES), I32),
        jax.ShapeDtypeStruct((n_t, LANES), I32),
        jax.ShapeDtypeStruct((n_t, LANES), I32),
        jax.ShapeDtypeStruct((SUBLANES, META_LANES), I32),
    ]
    return pl.pallas_call(
        functools.partial(_mixer_kernel, alpha=alpha),
        grid=(n_b, n_s),
        in_specs=in_specs,
        out_specs=out_specs,
        out_shape=out_shape,
        scratch_shapes=[
            pltpu.VMEM((ts + POOL_HALO, pw), F32),
            pltpu.VMEM((ts + CONV_HALO, cw), F32),
            pltpu.VMEM((ts, cw), F32),
            pltpu.VMEM((ts, 3 * d), F32),
            pltpu.VMEM((n_t, LANES), F32),
            pltpu.VMEM((n_t, LANES), F32),
            pltpu.VMEM((SUBLANES, LANES), F32),
        ],
        compiler_params=pltpu.CompilerParams(
            dimension_semantics=("arbitrary", "arbitrary"), vmem_limit_bytes=VMEM_LIMIT),
        name="mixer",
    )(x, *[weights[k] for k in names], kt, v, *[weights[k] for k in names2])


def _segment_copies(seg_off_ref, seg_len_ref, dest_ref, tile, make_copy):
    def copy_of(e):
        n = pl.multiple_of(seg_len_ref[tile, e], SEG_ALIGN)
        return n, make_copy(pl.multiple_of(seg_off_ref[tile, e], SEG_ALIGN),
                            pl.multiple_of(dest_ref[tile, e], SEG_ALIGN), n)

    def start(e, c):
        n, cp = copy_of(e)
        pl.when(n > 0)(cp.start)
        return c

    def wait(e, c):
        n, cp = copy_of(e)
        pl.when(n > 0)(cp.wait)
        return c

    return start, wait


def _dispatch_kernel(seg_off_ref, seg_len_ref, dest_ref, meta_ref,
                     ids_ref, off_ref, x1_ref, xs_ref, buf, zbuf, sem, zsem, *, n_e):
    tile = pl.program_id(0)
    ts = x1_ref.shape[0]
    rows = buf.shape[0]

    @pl.when(tile == 0)
    def _():
        zbuf[...] = jnp.zeros(zbuf.shape, BF16)

        def zero_copy(e):
            n = pl.multiple_of(meta_ref[META_PAD_LEN, e], SEG_ALIGN)
            dst = pl.multiple_of(meta_ref[META_PAD_DST, e], SEG_ALIGN)
            return n, pltpu.make_async_copy(zbuf.at[pl.ds(0, n)], xs_ref.at[pl.ds(dst, n)], zsem)

        def zstart(e, c):
            n, cp = zero_copy(e)
            pl.when(n > 0)(cp.start)
            return c

        def zwait(e, c):
            n, cp = zero_copy(e)
            pl.when(n > 0)(cp.wait)
            return c

        lax.fori_loop(0, n_e, zstart, 0)
        lax.fori_loop(0, n_e, zwait, 0)

    ids = ids_ref[...]
    e_iota = lax.broadcasted_iota(I32, (n_e, ts), 0)
    hots = [(e_iota == ids[k:k + 1, :]).astype(F32) for k in range(TOP_K)]
    hot = hots[0] + hots[1] + hots[2] + hots[3]
    off_row = off_ref[pl.ds(tile, 1), :].astype(F32)
    eye = (lax.broadcasted_iota(I32, (n_e, LANES), 0)
           == lax.broadcasted_iota(I32, (n_e, LANES), 1))
    off_col = jnp.sum(jnp.where(eye, off_row, 0.0), axis=1, keepdims=True)
    upper = (lax.broadcasted_iota(I32, (ts, ts), 0)
             < lax.broadcasted_iota(I32, (ts, ts), 1)).astype(BF16)
    slot = _dot(hot.astype(BF16), upper) + off_col
    r_iota = lax.broadcasted_iota(I32, (rows, ts), 0)
    perm = None
    for k in range(TOP_K):
        pos = jnp.sum(hots[k] * slot, axis=0, keepdims=True).astype(I32)
        perm = (r_iota == pos) if perm is None else perm | (r_iota == pos)
    perm = jnp.where(perm, 1.0, 0.0).astype(BF16)
    buf[...] = _dot(perm, x1_ref[...].astype(BF16)).astype(BF16)

    def make_copy(off, dst, n):
        return pltpu.make_async_copy(buf.at[pl.ds(off, n)], xs_ref.at[pl.ds(dst, n)], sem)

    start, wait = _segment_copies(seg_off_ref, seg_len_ref, dest_ref, tile, make_copy)
    lax.fori_loop(0, n_e, start, 0)
    lax.fori_loop(0, n_e, wait, 0)


def _tile_rows(ts, n_e):
    return TOP_K * ts + n_e * SEG_ALIGN


def _dispatch(tables, meta, ids, x1, n_rows, n_e):
    n_tok, d = x1.shape
    ts = TOKEN_TILE
    n_t = n_tok // ts
    rows = _tile_rows(ts, n_e)
    return pl.pallas_call(
        functools.partial(_dispatch_kernel, n_e=n_e),
        grid_spec=pltpu.PrefetchScalarGridSpec(
            num_scalar_prefetch=4,
            grid=(n_t,),
            in_specs=[
                pl.BlockSpec((TOP_K, ts), lambda t, *_: (0, t)),
                pl.BlockSpec((n_t, LANES), lambda t, *_: (0, 0)),
                pl.BlockSpec((ts, d), lambda t, *_: (t, 0)),
            ],
            out_specs=pl.BlockSpec(memory_space=pl.ANY),
            scratch_shapes=[pltpu.VMEM((rows, d), BF16), pltpu.VMEM((EXPERT_BLOCK, d), BF16),
                            pltpu.SemaphoreType.DMA(()), pltpu.SemaphoreType.DMA(())],
        ),
        out_shape=jax.ShapeDtypeStruct((n_rows, d), BF16),
        compiler_params=pltpu.CompilerParams(
            dimension_semantics=("arbitrary",), vmem_limit_bytes=VMEM_LIMIT),
        name="dispatch",
    )(*tables, meta, ids, tables[0], x1)


def _expert_kernel(meta_ref, xs_ref, wup_ref, bup_ref, wdn_ref, bdn_ref,
                   ys_ref, wup_bf, wdn_bf):
    j = pl.program_id(0)
    f = wdn_ref.shape[2]

    @pl.when(j < meta_ref[META_N_USED, 0])
    def _():
        prev = meta_ref[META_BLK_E, jnp.maximum(j - 1, 0)]

        @pl.when((j == 0) | (meta_ref[META_BLK_E, j] != prev))
        def _():
            wup_bf[...] = wup_ref[0, 0].astype(BF16)
            wdn_bf[...] = wdn_ref[0, 0].astype(BF16)

        x = xs_ref[...]

        def up(c0):
            glu = _dot(x, wup_bf[:, c0:c0 + FF_CHUNK]) + bup_ref[0, 0, :, c0:c0 + FF_CHUNK]
            lin = (_dot(x, wup_bf[:, f + c0:f + c0 + FF_CHUNK])
                   + bup_ref[0, 0, :, f + c0:f + c0 + FF_CHUNK])
            return glu, lin

        y = jnp.broadcast_to(bdn_ref[0, 0], ys_ref.shape)
        nxt = up(0)
        for c0 in range(0, f, FF_CHUNK):
            glu, lin = nxt
            if c0 + FF_CHUNK < f:
                nxt = up(c0 + FF_CHUNK)
            glu = jnp.minimum(glu, SWIGLU_LIMIT)
            lin = jnp.clip(lin, -SWIGLU_LIMIT, SWIGLU_LIMIT)
            act = glu * jax.nn.sigmoid(SWIGLU_ALPHA * glu) * (lin + 1.0)
            y = y + _dot(act.astype(BF16), wdn_bf[c0:c0 + FF_CHUNK, :])
        ys_ref[...] = y.astype(BF16)


def _experts(meta, xs, w_up, b_up, w_down, b_down, layer):
    n_rows, d = xs.shape
    f2 = w_up.shape[-1]
    f = f2 // 2
    bm = EXPERT_BLOCK
    n_blk = n_rows // bm

    def last_used(j, meta):
        return jnp.minimum(j, meta[META_N_USED, 0] - 1)

    def row_blk(j, meta):
        return (last_used(j, meta), 0)

    def w_blk(j, meta):
        return (layer, meta[META_BLK_E, last_used(j, meta)], 0, 0)

    return pl.pallas_call(
        _expert_kernel,
        grid_spec=pltpu.PrefetchScalarGridSpec(
            num_scalar_prefetch=1,
            grid=(n_blk,),
            in_specs=[
                pl.BlockSpec((bm, d), row_blk),
                pl.BlockSpec((1, 1, d, f2), w_blk),
                pl.BlockSpec((1, 1, 1, f2), w_blk),
                pl.BlockSpec((1, 1, f, d), w_blk),
                pl.BlockSpec((1, 1, 1, d), w_blk),
            ],
            out_specs=pl.BlockSpec((bm, d), row_blk),
            scratch_shapes=[pltpu.VMEM((d, f2), BF16), pltpu.VMEM((f, d), BF16)],
        ),
        out_shape=jax.ShapeDtypeStruct((n_rows, d), BF16),
        compiler_params=pltpu.CompilerParams(
            dimension_semantics=("arbitrary",), vmem_limit_bytes=VMEM_LIMIT),
        name="experts",
    )(meta, xs, w_up, b_up, w_down, b_down)


def _combine_kernel(seg_off_ref, seg_len_ref, dest_ref,
                    route_ref, off_ref, x1_ref, g2_ref, b2_ref, ys_ref,
                    out_ref, buf, sem, *, alpha, n_e):
    tile = pl.program_id(0)
    ts = x1_ref.shape[0]
    rows = buf.shape[0]

    @pl.when(tile == 0)
    def _():
        buf[...] = jnp.zeros(buf.shape, BF16)

    def make_copy(off, dst, n):
        return pltpu.make_async_copy(ys_ref.at[pl.ds(dst, n)], buf.at[pl.ds(off, n)], sem)

    start, wait = _segment_copies(seg_off_ref, seg_len_ref, dest_ref, tile, make_copy)
    lax.fori_loop(0, n_e, start, 0)

    route = route_ref[...]
    l_iota = lax.broadcasted_iota(I32, (ts, LANES), 1)
    hots = [(l_iota == route[:, k:k + 1].astype(I32)).astype(F32) for k in range(TOP_K)]
    hot = hots[0] + hots[1] + hots[2] + hots[3]
    lower = (lax.broadcasted_iota(I32, (ts, ts), 1)
             < lax.broadcasted_iota(I32, (ts, ts), 0)).astype(BF16)
    slot = _dot(lower, hot.astype(BF16)) + off_ref[pl.ds(tile, 1), :].astype(F32)
    r_iota = lax.broadcasted_iota(I32, (ts, rows), 1)
    weight = jnp.zeros((ts, rows), F32)
    for k in range(TOP_K):
        pos = jnp.sum(hots[k] * slot, axis=1, keepdims=True).astype(I32)
        weight = jnp.where(r_iota == pos, route[:, TOP_K + k:TOP_K + k + 1], weight)

    lax.fori_loop(0, n_e, wait, 0)
    y = _dot(weight.astype(BF16), buf[...])
    out_ref[...] = _layer_norm(alpha * x1_ref[...] + y, g2_ref[0], b2_ref[0])


def _combine(tables, route_t, x1, g2, b2, ys, layer, alpha, n_e):
    n_tok, d = x1.shape
    ts = TOKEN_TILE
    n_t = n_tok // ts
    rows = _tile_rows(ts, n_e)
    return pl.pallas_call(
        functools.partial(_combine_kernel, alpha=alpha, n_e=n_e),
        grid_spec=pltpu.PrefetchScalarGridSpec(
            num_scalar_prefetch=3,
            grid=(n_t,),
            in_specs=[
                pl.BlockSpec((ts, LANES), lambda t, *_: (t, 0)),
                pl.BlockSpec((n_t, LANES), lambda t, *_: (0, 0)),
                pl.BlockSpec((ts, d), lambda t, *_: (t, 0)),
                pl.BlockSpec((1, 1, d), lambda t, *_: (layer, 0, 0)),
                pl.BlockSpec((1, 1, d), lambda t, *_: (layer, 0, 0)),
                pl.BlockSpec(memory_space=pl.ANY),
            ],
            out_specs=pl.BlockSpec((ts, d), lambda t, *_: (t, 0)),
            scratch_shapes=[pltpu.VMEM((rows, d), BF16), pltpu.SemaphoreType.DMA(())],
        ),
        out_shape=jax.ShapeDtypeStruct((n_tok, d), F32),
        compiler_params=pltpu.CompilerParams(
            dimension_semantics=("arbitrary",), vmem_limit_bytes=VMEM_LIMIT),
        name="combine",
    )(*tables, route_t, tables[0], x1, g2, b2, ys)


def kernel(x, mem, mem_ln_g, mem_ln_b, w_in, b_in, pool_w, pool_scale, pool_proj, conv_dw, conv_dw_b, conv_ln_g, conv_ln_b, conv_pw, conv_pw_b, w_kv, attn_o, w_out, b_out, ln1_g, ln1_b, router_w, router_b, exp_up, exp_up_b, exp_down, exp_down_b, ln2_g, ln2_b):
    n_b, s, d = x.shape
    depth = w_in.shape[0]
    n_e = router_w.shape[-1]
    n_tok = n_b * s
    n_t = n_tok // TOKEN_TILE
    alpha = (2.0 * depth) ** 0.25
    n_rows = TOP_K * n_tok + n_t * n_e * SEG_ALIGN + n_e * EXPERT_BLOCK
    n_rows = -(-n_rows // EXPERT_BLOCK) * EXPERT_BLOCK
    assert n_rows // EXPERT_BLOCK <= META_LANES and n_e <= LANES

    def rows3(a):
        return a.reshape(a.shape[0], 1, a.shape[1])

    weights = dict(
        w_in=w_in.astype(BF16), b_in=rows3(b_in), pool_w=pool_w.astype(BF16),
        pool_scale=rows3(pool_scale), pool_proj=pool_proj.astype(BF16), conv_dw=conv_dw,
        conv_dw_b=rows3(conv_dw_b), conv_ln_g=rows3(conv_ln_g), conv_ln_b=rows3(conv_ln_b),
        conv_pw=conv_pw.astype(BF16), conv_pw_b=rows3(conv_pw_b), attn_o=attn_o.astype(BF16),
        w_out=w_out.astype(BF16), b_out=rows3(b_out), ln1_g=rows3(ln1_g), ln1_b=rows3(ln1_b),
        router_wt=jnp.swapaxes(router_w, 1, 2), router_b=router_b.reshape(depth, n_e, 1))
    up_b = exp_up_b.reshape(depth, n_e, 1, exp_up_b.shape[-1])
    down_b = exp_down_b.reshape(depth, n_e, 1, d)
    g2, b2 = rows3(ln2_g), rows3(ln2_b)

    kt_all, v_all = _memory_kv(mem, mem_ln_g, mem_ln_b, w_kv)
    for layer in range(depth):
        x1, ids, route_t, seg_off, seg_len, dest, meta = _mixer(
            x, weights, kt_all, v_all, layer, alpha)
        x1 = x1.reshape(n_tok, d)
        tables = (seg_off, seg_len, dest)
        xs = _dispatch(tables, meta, ids, x1, n_rows, n_e)
        ys = _experts(meta, xs, exp_up, up_b, exp_down, down_b, layer)
        x = _combine(tables, route_t, x1, g2, b2, ys, layer, alpha, n_e).reshape(n_b, s, d)
    return x
```

```python
import functools

import jax
import jax.numpy as jnp
from jax import lax
from jax.experimental import pallas as pl
from jax.experimental.pallas import tpu as pltpu

F32 = jnp.float32
BF16 = jnp.bfloat16
I32 = jnp.int32

POOL_WINDOWS = (2, 4, 8, 16)
POOL_HALO = 16
CONV_HALO = 32
HEAD_DIM = 128
TOP_K = 4
SWIGLU_LIMIT = 7.0
SWIGLU_ALPHA = 1.702
LN_EPS = 1e-5

LANES = 128
SUBLANES = 8
SEG_ALIGN = 16
TOKEN_TILE = 256
EXPERT_BLOCK = 512
FF_CHUNK = 256
CONV_ROWS = 64
GATE_CHUNK = 256
META_LANES = 512
META_BLK_E, META_PAD_DST, META_PAD_LEN, META_N_USED = 0, 1, 2, 3
VMEM_LIMIT = 56 * 1024 * 1024


def _layer_norm(x, g, b):
    mu = jnp.mean(x, axis=-1, keepdims=True)
    xc = x - mu
    var = jnp.mean(xc * xc, axis=-1, keepdims=True)
    return xc * lax.rsqrt(var + LN_EPS) * g + b


def _dot(a, b):
    return jnp.dot(a, b, preferred_element_type=F32)


def _dot_exact(a, b):
    return jnp.dot(a, b, preferred_element_type=F32, precision=lax.Precision.HIGHEST)


def _kv_kernel(mem_ref, g_ref, b_ref, wkv_ref, kt_ref, v_ref):
    a = kt_ref.shape[2]
    mem_n = _layer_norm(mem_ref[0], g_ref[...], b_ref[...]).astype(BF16)
    kv = _dot(mem_n, wkv_ref[0].astype(BF16))
    kt_ref[0, 0] = kv[:, :a].T.astype(BF16)
    v_ref[0, 0] = kv[:, a:].astype(BF16)


def _memory_kv(mem, g, b, w_kv):
    n_b, m, d = mem.shape
    n_l, _, a2 = w_kv.shape
    a = a2 // 2
    return pl.pallas_call(
        _kv_kernel,
        grid=(n_l, n_b),
        in_specs=[
            pl.BlockSpec((1, m, d), lambda l, bb: (bb, 0, 0)),
            pl.BlockSpec((1, d), lambda l, bb: (0, 0)),
            pl.BlockSpec((1, d), lambda l, bb: (0, 0)),
            pl.BlockSpec((1, d, a2), lambda l, bb: (l, 0, 0)),
        ],
        out_specs=[
            pl.BlockSpec((1, 1, a, m), lambda l, bb: (l, bb, 0, 0)),
            pl.BlockSpec((1, 1, m, a), lambda l, bb: (l, bb, 0, 0)),
        ],
        out_shape=[
            jax.ShapeDtypeStruct((n_l, n_b, a, m), BF16),
            jax.ShapeDtypeStruct((n_l, n_b, m, a), BF16),
        ],
        compiler_params=pltpu.CompilerParams(
            dimension_semantics=("arbitrary", "arbitrary"), vmem_limit_bytes=VMEM_LIMIT),
        name="memory_kv",
    )(mem, g.reshape(1, d), b.reshape(1, d), w_kv)


def _routing_tables(padded, prefix, n_e, seg_off_ref, seg_len_ref, dest_ref, meta_ref):
    row = lax.broadcasted_iota(I32, (LANES, LANES), 0)
    col = lax.broadcasted_iota(I32, (LANES, LANES), 1)
    seg_off = _dot_exact(padded, (row < col).astype(F32))
    tot = jnp.sum(padded, axis=0, keepdims=True)
    ptot = jnp.ceil(tot * (1.0 / EXPERT_BLOCK)) * EXPERT_BLOCK
    e_end = _dot_exact(jnp.broadcast_to(ptot, (SUBLANES, LANES)),
                       (row <= col).astype(F32))[0:1]
    e_start = e_end - ptot
    seg_off_ref[...] = seg_off.astype(I32)
    seg_len_ref[...] = padded.astype(I32)
    dest_ref[...] = (prefix + e_start).astype(I32)

    e_end_col = jnp.sum(jnp.where(row == col, jnp.broadcast_to(e_end, (LANES, LANES)), 0.0),
                        axis=1, keepdims=True)
    blk_row = (lax.broadcasted_iota(I32, (LANES, META_LANES), 1) * EXPERT_BLOCK).astype(F32)
    is_real = lax.broadcasted_iota(I32, (LANES, META_LANES), 0) < n_e
    blk_e = jnp.sum(jnp.where(is_real & (e_end_col <= blk_row), 1.0, 0.0), axis=0, keepdims=True)
    blk_e = jnp.minimum(blk_e, n_e - 1.0)
    lane = lax.broadcasted_iota(I32, (1, LANES), 1)
    n_used = jnp.sum(jnp.where(lane == n_e - 1, e_end, 0.0), axis=1, keepdims=True) * (
        1.0 / EXPERT_BLOCK)

    def wide(r):
        return jnp.concatenate([r, jnp.zeros((1, META_LANES - LANES), F32)], axis=1)

    sub = lax.broadcasted_iota(I32, (SUBLANES, META_LANES), 0)
    meta = jnp.zeros((SUBLANES, META_LANES), F32)
    for r, val in ((META_BLK_E, blk_e), (META_PAD_DST, wide(e_start + tot)),
                   (META_PAD_LEN, wide(ptot - tot)),
                   (META_N_USED, jnp.broadcast_to(n_used, (1, META_LANES)))):
        meta = jnp.where(sub == r, jnp.broadcast_to(val, (SUBLANES, META_LANES)), meta)
    meta_ref[...] = meta.astype(I32)


def _mixer_kernel(x_ref, w_in_ref, b_in_ref, pool_w_ref, pool_scale_ref, pool_proj_ref,
                  dw_ref, dwb_ref, cg_ref, cb_ref, cpw_ref, cpwb_ref,
                  kt_ref, v_ref, ao_ref, wout_ref, bout_ref, g1_ref, b1_ref,
                  wr_ref, rb_ref,
                  x1_ref, ids_ref, route_t_ref, seg_off_ref, seg_len_ref, dest_ref, meta_ref,
                  pool_ext, conv_ext, conv_out, gate_scr, cnt_scr, pre_scr, run_scr, *, alpha):
    i = pl.program_id(1)
    tile = pl.program_id(0) * pl.num_programs(1) + i
    n_tiles = pl.num_programs(0) * pl.num_programs(1)
    ts, d = x_ref.shape[1], x_ref.shape[2]
    pw = pool_ext.shape[1]
    cw = conv_ext.shape[1]
    aw = kt_ref.shape[2]
    n_e = wr_ref.shape[1]
    c_conv = pw
    c_q = pw + 2 * cw
    c_gate = c_q + aw

    @pl.when(i == 0)
    def _():
        pool_ext[0:POOL_HALO, :] = jnp.zeros((POOL_HALO, pw), F32)
        conv_ext[0:CONV_HALO, :] = jnp.zeros((CONV_HALO, cw), F32)

    @pl.when(tile == 0)
    def _():
        run_scr[...] = jnp.zeros(run_scr.shape, F32)

    x = x_ref[0]
    xb = x.astype(BF16)

    def proj(lo, hi):
        return _dot(xb, w_in_ref[0, :, lo:hi]) + b_in_ref[0, :, lo:hi]

    u = proj(0, pw)
    pool_ext[POOL_HALO:POOL_HALO + ts, :] = u
    t_glob = i * ts + lax.broadcasted_iota(I32, (ts, 1), 0)
    gd = pw // len(POOL_WINDOWS)
    mixed = []
    for g, w in enumerate(POOL_WINDOWS):
        lo = g * gd
        ug = u[:, lo:lo + gd]
        run = pool_ext[:, lo:lo + gd]
        step = 1
        while step < w:
            run = run + pltpu.roll(run, step, axis=0)
            step *= 2
        cnt = jnp.minimum(t_glob + 1, w).astype(F32)
        pooled = run[POOL_HALO:POOL_HALO + ts] / cnt - ug
        mixed.append(_dot(pooled.astype(BF16), pool_w_ref[0, g]))
    mixed = jnp.concatenate(mixed, axis=1) * pool_scale_ref[0]
    y_pool = _dot(mixed.astype(BF16), pool_proj_ref[0])
    pool_ext[0:POOL_HALO, :] = pool_ext[ts:ts + POOL_HALO, :]

    glu = proj(c_conv, c_conv + cw) * jax.nn.sigmoid(proj(c_conv + cw, c_conv + 2 * cw))
    conv_ext[CONV_HALO:CONV_HALO + ts, :] = glu
    taps = dw_ref.shape[1]
    base = CONV_HALO - (taps - 1)
    span = CONV_ROWS + CONV_HALO
    gate_chunks = [(c, min(c + GATE_CHUNK, c_gate + 3 * d))
                   for c in range(c_gate, c_gate + 3 * d, GATE_CHUNK)]
    for r0 in range(0, ts, CONV_ROWS):
        for c0 in range(0, cw, LANES):
            if gate_chunks:
                lo, hi = gate_chunks.pop(0)
                gate_scr[:, lo - c_gate:hi - c_gate] = proj(lo, hi)
            window = conv_ext[r0:r0 + span, c0:c0 + LANES]
            acc = jnp.broadcast_to(dwb_ref[0, :, c0:c0 + LANES], (CONV_ROWS, LANES))
            for res in range(SUBLANES):
                offs = [o for o in range(base, base + taps) if o % SUBLANES == res]
                if not offs:
                    continue
                shifted = window if res == 0 else pltpu.roll(window, span - res, axis=0)
                for o in offs:
                    k = o - base
                    acc = acc + dw_ref[0, k:k + 1, c0:c0 + LANES] * shifted[
                        o - res:o - res + CONV_ROWS]
            conv_out[r0:r0 + CONV_ROWS, c0:c0 + LANES] = acc
    for lo, hi in gate_chunks:
        gate_scr[:, lo - c_gate:hi - c_gate] = proj(lo, hi)
    conv_ext[0:CONV_HALO, :] = conv_ext[ts:ts + CONV_HALO, :]
    hc = _layer_norm(conv_out[...], cg_ref[0], cb_ref[0])
    hc = hc * jax.nn.sigmoid(hc)
    y_conv = _dot(hc.astype(BF16), cpw_ref[0]) + cpwb_ref[0]

    q = proj(c_q, c_q + aw)
    heads = []
    for h in range(aw // HEAD_DIM):
        lo = h * HEAD_DIM
        s = _dot(q[:, lo:lo + HEAD_DIM].astype(BF16), kt_ref[0, 0, lo:lo + HEAD_DIM, :])
        s = s * (HEAD_DIM ** -0.5)
        p = jnp.exp(s - jnp.max(s, axis=-1, keepdims=True))
        o = _dot(p.astype(BF16), v_ref[0, 0, :, lo:lo + HEAD_DIM])
        heads.append(o / jnp.sum(p, axis=-1, keepdims=True))
    y_attn = _dot(jnp.concatenate(heads, axis=1).astype(BF16), ao_ref[0])

    merged = jax.nn.sigmoid(gate_scr[:, 0:d]) * y_pool
    merged = merged + jax.nn.sigmoid(gate_scr[:, d:2 * d]) * y_conv
    merged = merged + jax.nn.sigmoid(gate_scr[:, 2 * d:3 * d]) * y_attn
    out = _dot(merged.astype(BF16), wout_ref[0]) + bout_ref[0]
    x1 = _layer_norm(alpha * x + out, g1_ref[0], b1_ref[0])
    x1_ref[0] = x1

    logits = lax.dot_general(wr_ref[0], x1, (((1,), (1,)), ((), ())),
                             precision=lax.Precision.HIGHEST,
                             preferred_element_type=F32) + rb_ref[0]
    e_iota = lax.broadcasted_iota(I32, (n_e, ts), 0)
    top_v, top_i = [], []
    hot = jnp.zeros((n_e, ts), F32)
    for _ in range(TOP_K):
        m = jnp.max(logits, axis=0, keepdims=True)
        idx = jnp.min(jnp.where(logits == m, e_iota, n_e), axis=0, keepdims=True)
        sel = e_iota == idx
        hot = hot + sel.astype(F32)
        logits = jnp.where(sel, -jnp.inf, logits)
        top_v.append(m)
        top_i.append(idx)
    ex = [jnp.exp(v - top_v[0]) for v in top_v]
    den = ex[0] + ex[1] + ex[2] + ex[3]
    sub = lax.broadcasted_iota(I32, (SUBLANES, ts), 0)
    route = jnp.zeros((SUBLANES, ts), F32)
    for k in range(TOP_K):
        ids_ref[k:k + 1, :] = top_i[k]
        route = jnp.where(sub == k, top_i[k].astype(F32), route)
        route = jnp.where(sub == TOP_K + k, ex[k] / den, route)
    route_t_ref[...] = jnp.concatenate(
        [route, jnp.zeros((LANES - SUBLANES, ts), F32)], axis=0).T

    hot_wide = jnp.concatenate([hot, jnp.zeros((LANES - n_e, ts), F32)], axis=0).astype(BF16)
    cnt = lax.dot_general(jnp.ones((SUBLANES, ts), BF16), hot_wide, (((1,), (1,)), ((), ())),
                          preferred_element_type=F32)[0:1]
    padded = jnp.ceil(cnt * (1.0 / SEG_ALIGN)) * SEG_ALIGN
    cnt_scr[pl.ds(tile, 1), :] = padded
    pre_scr[pl.ds(tile, 1), :] = run_scr[0:1, :]
    run_scr[0:1, :] = run_scr[0:1, :] + padded

    @pl.when(tile == n_tiles - 1)
    def _():
        _routing_tables(cnt_scr[...], pre_scr[...], n_e,
                        seg_off_ref, seg_len_ref, dest_ref, meta_ref)


def _mixer(x, weights, kt, v, layer, alpha):
    n_b, s, d = x.shape
    ts = TOKEN_TILE
    n_s = s // ts
    n_t = n_b * n_s
    names = ("w_in", "b_in", "pool_w", "pool_scale", "pool_proj", "conv_dw", "conv_dw_b",
             "conv_ln_g", "conv_ln_b", "conv_pw", "conv_pw_b")
    names2 = ("attn_o", "w_out", "b_out", "ln1_g", "ln1_b", "router_wt", "router_b")
    pw = weights["pool_proj"].shape[1]
    cw = weights["conv_pw"].shape[1]

    def of_layer(arr):
        tail = (0,) * (arr.ndim - 1)
        return pl.BlockSpec((1,) + arr.shape[1:], lambda bb, i: (layer,) + tail,
                            pipeline_mode=pl.Buffered(1))

    def resident(shape):
        return pl.BlockSpec(shape, lambda bb, i: (0, 0))

    in_specs = ([pl.BlockSpec((1, ts, d), lambda bb, i: (bb, i, 0))]
                + [of_layer(weights[k]) for k in names]
                + [pl.BlockSpec((1, 1) + kt.shape[2:], lambda bb, i: (layer, bb, 0, 0)),
                   pl.BlockSpec((1, 1) + v.shape[2:], lambda bb, i: (layer, bb, 0, 0))]
                + [of_layer(weights[k]) for k in names2])
    out_specs = [
        pl.BlockSpec((1, ts, d), lambda bb, i: (bb, i, 0)),
        pl.BlockSpec((TOP_K, ts), lambda bb, i: (0, bb * n_s + i)),
        pl.BlockSpec((ts, LANES), lambda bb, i: (bb * n_s + i, 0)),
        resident((n_t, LANES)), resident((n_t, LANES)), resident((n_t, LANES)),
        resident((SUBLANES, META_LANES)),
    ]
    out_shape = [
        jax.ShapeDtypeStruct((n_b, s, d), F32),
        jax.ShapeDtypeStruct((TOP_K, n_b * s), I32),
        jax.ShapeDtypeStruct((n_b * s, LANES), F32),
        jax.ShapeDtypeStruct((n_t, LANES), I32),
        jax.ShapeDtypeStruct((n_t, LANES), I32),
        jax.ShapeDtypeStruct((n_t, LANES), I32),
        jax.ShapeDtypeStruct((SUBLANES, META_LANES), I32),
    ]
    return pl.pallas_call(
        functools.partial(_mixer_kernel, alpha=alpha),
        grid=(n_b, n_s),
        in_specs=in_specs,
        out_specs=out_specs,
        out_shape=out_shape,
        scratch_shapes=[
            pltpu.VMEM((ts + POOL_HALO, pw), F32),
            pltpu.VMEM((ts + CONV_HALO, cw), F32),
            pltpu.VMEM((ts, cw), F32),
            pltpu.VMEM((ts, 3 * d), F32),
            pltpu.VMEM((n_t, LANES), F32),
            pltpu.VMEM((n_t, LANES), F32),
            pltpu.VMEM((SUBLANES, LANES), F32),
        ],
        compiler_params=pltpu.CompilerParams(
            dimension_semantics=("arbitrary", "arbitrary"), vmem_limit_bytes=VMEM_LIMIT),
        name="mixer",
    )(x, *[weights[k] for k in names], kt, v, *[weights[k] for k in names2])


def _start_segment_copies(seg_off_ref, seg_len_ref, dest_ref, tile, n_e, make_copy):
    def start(e, c):
        n = pl.multiple_of(seg_len_ref[tile, e], SEG_ALIGN)

        @pl.when(n > 0)
        def _():
            make_copy(pl.multiple_of(seg_off_ref[tile, e], SEG_ALIGN),
                      pl.multiple_of(dest_ref[tile, e], SEG_ALIGN), n).start()
        return c

    lax.fori_loop(0, n_e, start, 0)


def _wait_segment_copies(seg_off_ref, seg_len_ref, tile, n_e, make_copy):
    total = seg_off_ref[tile, n_e - 1] + seg_len_ref[tile, n_e - 1]
    make_copy(0, 0, pl.multiple_of(total, SEG_ALIGN)).wait()


def _dispatch_kernel(seg_off_ref, seg_len_ref, dest_ref, meta_ref,
                     ids_ref, off_ref, x1_ref, xs_ref, buf, zbuf, sem, zsem, *, n_e):
    tile = pl.program_id(0)
    n_tiles = pl.num_programs(0)
    slot = tile % 2
    ts = x1_ref.shape[0]
    rows = buf.shape[1]

    @pl.when(tile == 0)
    def _():
        zbuf[...] = jnp.zeros(zbuf.shape, BF16)

        def zero_copy(e):
            n = pl.multiple_of(meta_ref[META_PAD_LEN, e], SEG_ALIGN)
            dst = pl.multiple_of(meta_ref[META_PAD_DST, e], SEG_ALIGN)
            return n, pltpu.make_async_copy(zbuf.at[pl.ds(0, n)], xs_ref.at[pl.ds(dst, n)], zsem)

        def zstart(e, c):
            n, cp = zero_copy(e)
            pl.when(n > 0)(cp.start)
            return c

        def zwait(e, c):
            n, cp = zero_copy(e)
            pl.when(n > 0)(cp.wait)
            return c

        lax.fori_loop(0, n_e, zstart, 0)
        lax.fori_loop(0, n_e, zwait, 0)

    ids = ids_ref[...]
    e_iota = lax.broadcasted_iota(I32, (n_e, ts), 0)
    hots = [(e_iota == ids[k:k + 1, :]).astype(F32) for k in range(TOP_K)]
    hot = hots[0] + hots[1] + hots[2] + hots[3]
    off_row = off_ref[pl.ds(tile, 1), :].astype(F32)
    eye = (lax.broadcasted_iota(I32, (n_e, LANES), 0)
           == lax.broadcasted_iota(I32, (n_e, LANES), 1))
    off_col = jnp.sum(jnp.where(eye, off_row, 0.0), axis=1, keepdims=True)
    upper = (lax.broadcasted_iota(I32, (ts, ts), 0)
             < lax.broadcasted_iota(I32, (ts, ts), 1)).astype(BF16)
    place = _dot(hot.astype(BF16), upper) + off_col
    r_iota = lax.broadcasted_iota(I32, (rows, ts), 0)
    perm = None
    for k in range(TOP_K):
        pos = jnp.sum(hots[k] * place, axis=0, keepdims=True).astype(I32)
        perm = (r_iota == pos) if perm is None else perm | (r_iota == pos)
    perm = jnp.where(perm, 1.0, 0.0).astype(BF16)
    sorted_rows = _dot(perm, x1_ref[...].astype(BF16)).astype(BF16)

    def copies_from(s):
        def make_copy(off, dst, n):
            return pltpu.make_async_copy(buf.at[s, pl.ds(off, n)], xs_ref.at[pl.ds(dst, n)],
                                         sem.at[s])
        return make_copy

    @pl.when(tile >= 2)
    def _():
        _wait_segment_copies(seg_off_ref, seg_len_ref, tile - 2, n_e, copies_from(slot))

    buf[slot] = sorted_rows
    _start_segment_copies(seg_off_ref, seg_len_ref, dest_ref, tile, n_e, copies_from(slot))

    @pl.when(tile == n_tiles - 1)
    def _():
        @pl.when(tile >= 1)
        def _():
            _wait_segment_copies(seg_off_ref, seg_len_ref, tile - 1, n_e, copies_from(1 - slot))

        _wait_segment_copies(seg_off_ref, seg_len_ref, tile, n_e, copies_from(slot))


def _tile_rows(ts, n_e):
    return TOP_K * ts + n_e * SEG_ALIGN


def _dispatch(tables, meta, ids, x1, n_rows, n_e):
    n_tok, d = x1.shape
    ts = TOKEN_TILE
    n_t = n_tok // ts
    rows = _tile_rows(ts, n_e)
    return pl.pallas_call(
        functools.partial(_dispatch_kernel, n_e=n_e),
        grid_spec=pltpu.PrefetchScalarGridSpec(
            num_scalar_prefetch=4,
            grid=(n_t,),
            in_specs=[
                pl.BlockSpec((TOP_K, ts), lambda t, *_: (0, t)),
                pl.BlockSpec((n_t, LANES), lambda t, *_: (0, 0)),
                pl.BlockSpec((ts, d), lambda t, *_: (t, 0)),
            ],
            out_specs=pl.BlockSpec(memory_space=pl.ANY),
            scratch_shapes=[pltpu.VMEM((2, rows, d), BF16), pltpu.VMEM((EXPERT_BLOCK, d), BF16),
                            pltpu.SemaphoreType.DMA((2,)), pltpu.SemaphoreType.DMA(())],
        ),
        out_shape=jax.ShapeDtypeStruct((n_rows, d), BF16),
        compiler_params=pltpu.CompilerParams(
            dimension_semantics=("arbitrary",), vmem_limit_bytes=VMEM_LIMIT),
        name="dispatch",
    )(*tables, meta, ids, tables[0], x1)


def _expert_kernel(meta_ref, xs_ref, wup_ref, bup_ref, wdn_ref, bdn_ref,
                   ys_ref, wup_bf, wdn_bf):
    j = pl.program_id(0)
    f = wdn_ref.shape[2]

    @pl.when(j < meta_ref[META_N_USED, 0])
    def _():
        prev = meta_ref[META_BLK_E, jnp.maximum(j - 1, 0)]

        @pl.when((j == 0) | (meta_ref[META_BLK_E, j] != prev))
        def _():
            wup_bf[...] = wup_ref[0, 0].astype(BF16)
            wdn_bf[...] = wdn_ref[0, 0].astype(BF16)

        x = xs_ref[...]

        def up(c0):
            glu = _dot(x, wup_bf[:, c0:c0 + FF_CHUNK]) + bup_ref[0, 0, :, c0:c0 + FF_CHUNK]
            lin = (_dot(x, wup_bf[:, f + c0:f + c0 + FF_CHUNK])
                   + bup_ref[0, 0, :, f + c0:f + c0 + FF_CHUNK])
            return glu, lin

        y = jnp.broadcast_to(bdn_ref[0, 0], ys_ref.shape)
        nxt = up(0)
        for c0 in range(0, f, FF_CHUNK):
            glu, lin = nxt
            if c0 + FF_CHUNK < f:
                nxt = up(c0 + FF_CHUNK)
            glu = jnp.minimum(glu, SWIGLU_LIMIT)
            lin = jnp.clip(lin, -SWIGLU_LIMIT, SWIGLU_LIMIT)
            act = glu * jax.nn.sigmoid(SWIGLU_ALPHA * glu) * (lin + 1.0)
            y = y + _dot(act.astype(BF16), wdn_bf[c0:c0 + FF_CHUNK, :])
        ys_ref[...] = y.astype(BF16)


def _experts(meta, xs, w_up, b_up, w_down, b_down, layer):
    n_rows, d = xs.shape
    f2 = w_up.shape[-1]
    f = f2 // 2
    bm = EXPERT_BLOCK
    n_blk = n_rows // bm

    def last_used(j, meta):
        return jnp.maximum(jnp.minimum(j, meta[META_N_USED, 0] - 1), 0)

    def row_blk(j, meta):
        return (last_used(j, meta), 0)

    def w_blk(j, meta):
        return (layer, meta[META_BLK_E, last_used(j, meta)], 0, 0)

    return pl.pallas_call(
        _expert_kernel,
        grid_spec=pltpu.PrefetchScalarGridSpec(
            num_scalar_prefetch=1,
            grid=(n_blk,),
            in_specs=[
                pl.BlockSpec((bm, d), row_blk),
                pl.BlockSpec((1, 1, d, f2), w_blk),
                pl.BlockSpec((1, 1, 1, f2), w_blk),
                pl.BlockSpec((1, 1, f, d), w_blk),
                pl.BlockSpec((1, 1, 1, d), w_blk),
            ],
            out_specs=pl.BlockSpec((bm, d), row_blk),
            scratch_shapes=[pltpu.VMEM((d, f2), BF16), pltpu.VMEM((f, d), BF16)],
        ),
        out_shape=jax.ShapeDtypeStruct((n_rows, d), BF16),
        compiler_params=pltpu.CompilerParams(
            dimension_semantics=("arbitrary",), vmem_limit_bytes=VMEM_LIMIT),
        name="experts",
    )(meta, xs, w_up, b_up, w_down, b_down)


def _combine_kernel(seg_off_ref, seg_len_ref, dest_ref,
                    route_ref, off_ref, x1_ref, g2_ref, b2_ref, ys_ref,
                    out_ref, buf, sem, *, alpha, n_e):
    tile = pl.program_id(0)
    n_tiles = pl.num_programs(0)
    slot = tile % 2
    ts = x1_ref.shape[0]
    rows = buf.shape[1]

    def copies_into(s):
        def make_copy(off, dst, n):
            return pltpu.make_async_copy(ys_ref.at[pl.ds(dst, n)], buf.at[s, pl.ds(off, n)],
                                         sem.at[s])
        return make_copy

    @pl.when(tile == 0)
    def _():
        buf[...] = jnp.zeros(buf.shape, BF16)
        _start_segment_copies(seg_off_ref, seg_len_ref, dest_ref, tile, n_e, copies_into(slot))

    @pl.when(tile + 1 < n_tiles)
    def _():
        _start_segment_copies(seg_off_ref, seg_len_ref, dest_ref, tile + 1, n_e,
                              copies_into(1 - slot))

    route = route_ref[...]
    l_iota = lax.broadcasted_iota(I32, (ts, LANES), 1)
    hots = [(l_iota == route[:, k:k + 1].astype(I32)).astype(F32) for k in range(TOP_K)]
    hot = hots[0] + hots[1] + hots[2] + hots[3]
    lower = (lax.broadcasted_iota(I32, (ts, ts), 1)
             < lax.broadcasted_iota(I32, (ts, ts), 0)).astype(BF16)
    place = _dot(lower, hot.astype(BF16)) + off_ref[pl.ds(tile, 1), :].astype(F32)
    r_iota = lax.broadcasted_iota(I32, (ts, rows), 1)
    weight = jnp.zeros((ts, rows), F32)
    for k in range(TOP_K):
        pos = jnp.sum(hots[k] * place, axis=1, keepdims=True).astype(I32)
        weight = jnp.where(r_iota == pos, route[:, TOP_K + k:TOP_K + k + 1], weight)

    _wait_segment_copies(seg_off_ref, seg_len_ref, tile, n_e, copies_into(slot))
    y = _dot(weight.astype(BF16), buf[slot])
    out_ref[...] = _layer_norm(alpha * x1_ref[...] + y, g2_ref[0], b2_ref[0])


def _combine(tables, route_t, x1, g2, b2, ys, layer, alpha, n_e):
    n_tok, d = x1.shape
    ts = TOKEN_TILE
    n_t = n_tok // ts
    rows = _tile_rows(ts, n_e)
    return pl.pallas_call(
        functools.partial(_combine_kernel, alpha=alpha, n_e=n_e),
        grid_spec=pltpu.PrefetchScalarGridSpec(
            num_scalar_prefetch=3,
            grid=(n_t,),
            in_specs=[
                pl.BlockSpec((ts, LANES), lambda t, *_: (t, 0)),
                pl.BlockSpec((n_t, LANES), lambda t, *_: (0, 0)),
                pl.BlockSpec((ts, d), lambda t, *_: (t, 0)),
                pl.BlockSpec((1, 1, d), lambda t, *_: (layer, 0, 0)),
                pl.BlockSpec((1, 1, d), lambda t, *_: (layer, 0, 0)),
                pl.BlockSpec(memory_space=pl.ANY),
            ],
            out_specs=pl.BlockSpec((ts, d), lambda t, *_: (t, 0)),
            scratch_shapes=[pltpu.VMEM((2, rows, d), BF16), pltpu.SemaphoreType.DMA((2,))],
        ),
        out_shape=jax.ShapeDtypeStruct((n_tok, d), F32),
        compiler_params=pltpu.CompilerParams(
            dimension_semantics=("arbitrary",), vmem_limit_bytes=VMEM_LIMIT),
        name="combine",
    )(*tables, route_t, tables[0], x1, g2, b2, ys)


def kernel(x, mem, mem_ln_g, mem_ln_b, w_in, b_in, pool_w, pool_scale, pool_proj, conv_dw, conv_dw_b, conv_ln_g, conv_ln_b, conv_pw, conv_pw_b, w_kv, attn_o, w_out, b_out, ln1_g, ln1_b, router_w, router_b, exp_up, exp_up_b, exp_down, exp_down_b, ln2_g, ln2_b):
    n_b, s, d = x.shape
    depth = w_in.shape[0]
    n_e = router_w.shape[-1]
    n_tok = n_b * s
    n_t = n_tok // TOKEN_TILE
    alpha = (2.0 * depth) ** 0.25
    n_rows = TOP_K * n_tok + n_t * n_e * SEG_ALIGN + n_e * EXPERT_BLOCK
    n_rows = -(-n_rows // EXPERT_BLOCK) * EXPERT_BLOCK
    assert n_rows // EXPERT_BLOCK <= META_LANES and n_e <= LANES

    def rows3(a):
        return a.reshape(a.shape[0], 1, a.shape[1])

    weights = dict(
        w_in=w_in.astype(BF16), b_in=rows3(b_in), pool_w=pool_w.astype(BF16),
        pool_scale=rows3(pool_scale), pool_proj=pool_proj.astype(BF16), conv_dw=conv_dw,
        conv_dw_b=rows3(conv_dw_b), conv_ln_g=rows3(conv_ln_g), conv_ln_b=rows3(conv_ln_b),
        conv_pw=conv_pw.astype(BF16), conv_pw_b=rows3(conv_pw_b), attn_o=attn_o.astype(BF16),
        w_out=w_out.astype(BF16), b_out=rows3(b_out), ln1_g=rows3(ln1_g), ln1_b=rows3(ln1_b),
        router_wt=jnp.swapaxes(router_w, 1, 2), router_b=router_b.reshape(depth, n_e, 1))
    up_b = exp_up_b.reshape(depth, n_e, 1, exp_up_b.shape[-1])
    down_b = exp_down_b.reshape(depth, n_e, 1, d)
    g2, b2 = rows3(ln2_g), rows3(ln2_b)

    kt_all, v_all = _memory_kv(mem, mem_ln_g, mem_ln_b, w_kv)
    for layer in range(depth):
        x1, ids, route_t, seg_off, seg_len, dest, meta = _mixer(
            x, weights, kt_all, v_all, layer, alpha)
        x1 = x1.reshape(n_tok, d)
        tables = (seg_off, seg_len, dest)
        xs = _dispatch(tables, meta, ids, x1, n_rows, n_e)
        ys = _experts(meta, xs, exp_up, up_b, exp_down, down_b, layer)
        x = _combine(tables, route_t, x1, g2, b2, ys, layer, alpha, n_e).reshape(n_b, s, d)
    return x
```

```python
import functools

import jax
import jax.numpy as jnp
from jax import lax
from jax.experimental import pallas as pl
from jax.experimental.pallas import tpu as pltpu

F32 = jnp.float32
BF16 = jnp.bfloat16
I32 = jnp.int32

POOL_WINDOWS = (2, 4, 8, 16)
POOL_HALO = 16
CONV_HALO = 32
HEAD_DIM = 128
TOP_K = 4
SWIGLU_LIMIT = 7.0
SWIGLU_ALPHA = 1.702
LN_EPS = 1e-5

LANES = 128
SUBLANES = 8
SEG_ALIGN = 16
TOKEN_TILE = 256
EXPERT_BLOCK = 512
FF_CHUNK = 256
CONV_ROWS = 64
PROJ_CHUNK = 256
LOOP_PROJ_CHUNKS = 3
PERM_ROWS = 256
META_LANES = 512
META_BLK_E, META_PAD_DST, META_PAD_LEN, META_N_USED = 0, 1, 2, 3
VMEM_LIMIT = 56 * 1024 * 1024


def _layer_norm(x, g, b):
    mu = jnp.mean(x, axis=-1, keepdims=True)
    xc = x - mu
    var = jnp.mean(xc * xc, axis=-1, keepdims=True)
    return xc * lax.rsqrt(var + LN_EPS) * g + b


def _dot(a, b):
    return jnp.dot(a, b, preferred_element_type=F32)


def _dot_exact(a, b):
    return jnp.dot(a, b, preferred_element_type=F32, precision=lax.Precision.HIGHEST)


def _kv_kernel(mem_ref, g_ref, b_ref, wkv_ref, kt_ref, v_ref):
    a = kt_ref.shape[2]
    mem_n = _layer_norm(mem_ref[0], g_ref[...], b_ref[...]).astype(BF16)
    kv = _dot(mem_n, wkv_ref[0].astype(BF16))
    kt_ref[0, 0] = kv[:, :a].T.astype(BF16)
    v_ref[0, 0] = kv[:, a:].astype(BF16)


def _memory_kv(mem, g, b, w_kv):
    n_b, m, d = mem.shape
    n_l, _, a2 = w_kv.shape
    a = a2 // 2
    return pl.pallas_call(
        _kv_kernel,
        grid=(n_l, n_b),
        in_specs=[
            pl.BlockSpec((1, m, d), lambda l, bb: (bb, 0, 0)),
            pl.BlockSpec((1, d), lambda l, bb: (0, 0)),
            pl.BlockSpec((1, d), lambda l, bb: (0, 0)),
            pl.BlockSpec((1, d, a2), lambda l, bb: (l, 0, 0)),
        ],
        out_specs=[
            pl.BlockSpec((1, 1, a, m), lambda l, bb: (l, bb, 0, 0)),
            pl.BlockSpec((1, 1, m, a), lambda l, bb: (l, bb, 0, 0)),
        ],
        out_shape=[
            jax.ShapeDtypeStruct((n_l, n_b, a, m), BF16),
            jax.ShapeDtypeStruct((n_l, n_b, m, a), BF16),
        ],
        compiler_params=pltpu.CompilerParams(
            dimension_semantics=("arbitrary", "arbitrary"), vmem_limit_bytes=VMEM_LIMIT),
        name="memory_kv",
    )(mem, g.reshape(1, d), b.reshape(1, d), w_kv)


def _routing_tables(padded, prefix, n_e, seg_off_ref, seg_len_ref, dest_ref, meta_ref):
    row = lax.broadcasted_iota(I32, (LANES, LANES), 0)
    col = lax.broadcasted_iota(I32, (LANES, LANES), 1)
    seg_off = _dot_exact(padded, (row < col).astype(F32))
    tot = jnp.sum(padded, axis=0, keepdims=True)
    ptot = jnp.ceil(tot * (1.0 / EXPERT_BLOCK)) * EXPERT_BLOCK
    e_end = _dot_exact(jnp.broadcast_to(ptot, (SUBLANES, LANES)),
                       (row <= col).astype(F32))[0:1]
    e_start = e_end - ptot
    seg_off_ref[...] = seg_off.astype(I32)
    seg_len_ref[...] = padded.astype(I32)
    dest_ref[...] = (prefix + e_start).astype(I32)

    e_end_col = jnp.sum(jnp.where(row == col, jnp.broadcast_to(e_end, (LANES, LANES)), 0.0),
                        axis=1, keepdims=True)
    blk_row = (lax.broadcasted_iota(I32, (LANES, META_LANES), 1) * EXPERT_BLOCK).astype(F32)
    is_real = lax.broadcasted_iota(I32, (LANES, META_LANES), 0) < n_e
    blk_e = jnp.sum(jnp.where(is_real & (e_end_col <= blk_row), 1.0, 0.0), axis=0, keepdims=True)
    blk_e = jnp.minimum(blk_e, n_e - 1.0)
    lane = lax.broadcasted_iota(I32, (1, LANES), 1)
    n_used = jnp.sum(jnp.where(lane == n_e - 1, e_end, 0.0), axis=1, keepdims=True) * (
        1.0 / EXPERT_BLOCK)

    def wide(r):
        return jnp.concatenate([r, jnp.zeros((1, META_LANES - LANES), F32)], axis=1)

    sub = lax.broadcasted_iota(I32, (SUBLANES, META_LANES), 0)
    meta = jnp.zeros((SUBLANES, META_LANES), F32)
    for r, val in ((META_BLK_E, blk_e), (META_PAD_DST, wide(e_start + tot)),
                   (META_PAD_LEN, wide(ptot - tot)),
                   (META_N_USED, jnp.broadcast_to(n_used, (1, META_LANES)))):
        meta = jnp.where(sub == r, jnp.broadcast_to(val, (SUBLANES, META_LANES)), meta)
    meta_ref[...] = meta.astype(I32)


def _mixer_kernel(x_ref, w_in_ref, b_in_ref, pool_w_ref, pool_scale_ref, pool_proj_ref,
                  dw_ref, dwb_ref, cg_ref, cb_ref, cpw_ref, cpwb_ref,
                  kt_ref, v_ref, ao_ref, wout_ref, bout_ref, g1_ref, b1_ref,
                  wr_hi_ref, wr_lo_ref, rb_ref,
                  x1_ref, ids_ref, route_t_ref, seg_off_ref, seg_len_ref, dest_ref, meta_ref,
                  h_cur, h_nxt, x_cur, x_nxt, xb_scr, pool_ext, conv_ext, conv_out,
                  cnt_scr, pre_scr, run_scr,
                  *, alpha, n_s):
    s = pl.program_id(0)
    n_tiles = pl.num_programs(0) - 1
    tile = jnp.maximum(s - 1, 0)
    i = tile % n_s
    ts, d = x_ref.shape[1], x_ref.shape[2]
    n_chunks = w_in_ref.shape[1]
    pw = pool_ext.shape[1]
    cw = conv_ext.shape[1]
    aw = kt_ref.shape[2]
    n_e = rb_ref.shape[1]
    c_conv = pw
    c_q = pw + 2 * cw
    c_gate = c_q + aw

    @pl.when(i == 0)
    def _():
        pool_ext[0:POOL_HALO, :] = jnp.zeros((POOL_HALO, pw), F32)
        conv_ext[0:CONV_HALO, :] = jnp.zeros((CONV_HALO, cw), F32)

    @pl.when(s == 0)
    def _():
        run_scr[...] = jnp.zeros(run_scr.shape, F32)
        h_cur[...] = jnp.zeros(h_cur.shape, F32)
        x_cur[...] = jnp.zeros(x_cur.shape, F32)

    x_in = x_ref[0]
    x_nxt[...] = x_in
    xb_scr[...] = x_in.astype(BF16)

    x = x_cur[...]

    def proj(lo, hi):
        return jnp.concatenate([h_cur[c] for c in range(lo // PROJ_CHUNK, hi // PROJ_CHUNK)],
                               axis=1)

    glu = proj(c_conv, c_conv + cw) * jax.nn.sigmoid(proj(c_conv + cw, c_conv + 2 * cw))
    conv_ext[CONV_HALO:CONV_HALO + ts, :] = glu
    taps = dw_ref.shape[1]
    base = CONV_HALO - (taps - 1)
    span = CONV_ROWS + CONV_HALO
    n_row_chunks = ts // CONV_ROWS
    chunks_per_iter = min(LOOP_PROJ_CHUNKS, n_chunks // n_row_chunks)
    later_chunks = list(range(n_row_chunks * chunks_per_iter, n_chunks))

    def project(c):
        h_nxt[c] = _dot(xb_scr[...], w_in_ref[0, c]) + b_in_ref[0, c]

    def project_some(n):
        for _ in range(min(n, len(later_chunks))):
            project(later_chunks.pop(0))

    def conv_rows_and_projection(j, carry):
        r0 = pl.multiple_of(j * CONV_ROWS, CONV_ROWS)
        for c0 in range(0, cw, LANES):
            window = conv_ext[pl.ds(r0, span), c0:c0 + LANES]
            acc = jnp.broadcast_to(dwb_ref[0, :, c0:c0 + LANES], (CONV_ROWS, LANES))
            for res in range(SUBLANES):
                offs = [o for o in range(base, base + taps) if o % SUBLANES == res]
                if not offs:
                    continue
                shifted = window if res == 0 else pltpu.roll(window, span - res, axis=0)
                for o in offs:
                    k = o - base
                    acc = acc + dw_ref[0, k:k + 1, c0:c0 + LANES] * shifted[
                        o - res:o - res + CONV_ROWS]
            conv_out[pl.ds(r0, CONV_ROWS), c0:c0 + LANES] = acc
        for m in range(chunks_per_iter):
            project(j * chunks_per_iter + m)
        return carry

    lax.fori_loop(0, n_row_chunks, conv_rows_and_projection, 0)
    conv_ext[0:CONV_HALO, :] = conv_ext[ts:ts + CONV_HALO, :]
    hc = _layer_norm(conv_out[...], cg_ref[0], cb_ref[0])
    hc = hc * jax.nn.sigmoid(hc)
    project_some(2)
    y_conv = _dot(hc.astype(BF16), cpw_ref[0]) + cpwb_ref[0]

    u = proj(0, pw)
    pool_ext[POOL_HALO:POOL_HALO + ts, :] = u
    t_glob = i * ts + lax.broadcasted_iota(I32, (ts, 1), 0)
    gd = pw // len(POOL_WINDOWS)
    pooled = []
    for g, w in enumerate(POOL_WINDOWS):
        lo = g * gd
        ug = u[:, lo:lo + gd]
        run = pool_ext[:, lo:lo + gd]
        step = 1
        while step < w:
            run = run + pltpu.roll(run, step, axis=0)
            step *= 2
        cnt = jnp.minimum(t_glob + 1, w).astype(F32)
        pooled.append((run[POOL_HALO:POOL_HALO + ts] / cnt - ug).astype(BF16))
    project_some(2)
    mixed = [_dot(p, pool_w_ref[0, g]) for g, p in enumerate(pooled)]
    mixed = jnp.concatenate(mixed, axis=1) * pool_scale_ref[0]
    y_pool = _dot(mixed.astype(BF16), pool_proj_ref[0])
    pool_ext[0:POOL_HALO, :] = pool_ext[ts:ts + POOL_HALO, :]

    q = proj(c_q, c_q + aw)
    heads = []
    for h in range(aw // HEAD_DIM):
        lo = h * HEAD_DIM
        sc = _dot(q[:, lo:lo + HEAD_DIM].astype(BF16), kt_ref[0, 0, lo:lo + HEAD_DIM, :])
        sc = sc * (HEAD_DIM ** -0.5)
        p = jnp.exp(sc - jnp.max(sc, axis=-1, keepdims=True))
        o = _dot(p.astype(BF16), v_ref[0, 0, :, lo:lo + HEAD_DIM])
        heads.append(o / jnp.sum(p, axis=-1, keepdims=True))
    y_attn = _dot(jnp.concatenate(heads, axis=1).astype(BF16), ao_ref[0])

    merged = jax.nn.sigmoid(proj(c_gate, c_gate + d)) * y_pool
    merged = merged + jax.nn.sigmoid(proj(c_gate + d, c_gate + 2 * d)) * y_conv
    merged = merged + jax.nn.sigmoid(proj(c_gate + 2 * d, c_gate + 3 * d)) * y_attn
    project_some(2)
    out = _dot(merged.astype(BF16), wout_ref[0]) + bout_ref[0]
    x1 = _layer_norm(alpha * x + out, g1_ref[0], b1_ref[0])
    x1_ref[0] = x1

    x1_hi = x1.astype(BF16)
    x1_lo = (x1 - x1_hi.astype(F32)).astype(BF16)
    logits = _dot(x1_hi, wr_hi_ref[0]) + _dot(x1_lo, wr_hi_ref[0]) + _dot(x1_hi, wr_lo_ref[0])
    logits = logits.T[0:n_e, :] + rb_ref[0]
    e_iota = lax.broadcasted_iota(I32, (n_e, ts), 0)
    rank = jnp.zeros((n_e, ts), F32)
    for other in range(n_e):
        row = logits[other:other + 1, :]
        beats = (row > logits) | ((row == logits) & (e_iota > other))
        rank = rank + jnp.where(beats, 1.0, 0.0)
    hot = jnp.where(rank < TOP_K, 1.0, 0.0)
    top_v, top_i = [], []
    for k in range(TOP_K):
        sel = rank == k
        top_i.append(jnp.sum(jnp.where(sel, e_iota, 0), axis=0, keepdims=True))
        top_v.append(jnp.sum(jnp.where(sel, logits, 0.0), axis=0, keepdims=True))
    ex = [jnp.exp(v - top_v[0]) for v in top_v]
    den = ex[0] + ex[1] + ex[2] + ex[3]
    sub = lax.broadcasted_iota(I32, (SUBLANES, ts), 0)
    route = jnp.zeros((SUBLANES, ts), F32)
    for k in range(TOP_K):
        ids_ref[k:k + 1, :] = top_i[k]
        route = jnp.where(sub == k, top_i[k].astype(F32), route)
        route = jnp.where(sub == TOP_K + k, ex[k] / den, route)
    route_t_ref[...] = jnp.concatenate(
        [route, jnp.zeros((LANES - SUBLANES, ts), F32)], axis=0).T

    hot_wide = jnp.concatenate([hot, jnp.zeros((LANES - n_e, ts), F32)], axis=0).astype(BF16)
    cnt = lax.dot_general(jnp.ones((SUBLANES, ts), BF16), hot_wide, (((1,), (1,)), ((), ())),
                          preferred_element_type=F32)[0:1]
    is_expert = lax.broadcasted_iota(I32, (1, LANES), 1) < n_e
    padded = jnp.where(is_expert, jnp.maximum(jnp.ceil(cnt * (1.0 / SEG_ALIGN)), 1.0), 0.0)
    padded = padded * SEG_ALIGN
    padded = jnp.where(s >= 1, padded, 0.0)
    cnt_scr[pl.ds(tile, 1), :] = padded
    pre_scr[pl.ds(tile, 1), :] = run_scr[0:1, :]
    run_scr[0:1, :] = run_scr[0:1, :] + padded

    project_some(len(later_chunks))
    h_cur[...] = h_nxt[...]
    x_cur[...] = x_nxt[...]

    @pl.when(s == n_tiles)
    def _():
        _routing_tables(cnt_scr[...], pre_scr[...], n_e,
                        seg_off_ref, seg_len_ref, dest_ref, meta_ref)


def _mixer(x, weights, kt, v, layer, alpha):
    n_b, s, d = x.shape
    ts = TOKEN_TILE
    n_s = s // ts
    n_t = n_b * n_s
    names = ("w_in", "b_in", "pool_w", "pool_scale", "pool_proj", "conv_dw", "conv_dw_b",
             "conv_ln_g", "conv_ln_b", "conv_pw", "conv_pw_b")
    names2 = ("attn_o", "w_out", "b_out", "ln1_g", "ln1_b", "router_hi", "router_lo", "router_b")
    pw = weights["pool_proj"].shape[1]
    cw = weights["conv_pw"].shape[1]

    n_chunks = weights["w_in"].shape[1]

    def of_layer(arr):
        tail = (0,) * (arr.ndim - 1)
        return pl.BlockSpec((1,) + arr.shape[1:], lambda st: (layer,) + tail,
                            pipeline_mode=pl.Buffered(1))

    def resident(shape):
        return pl.BlockSpec(shape, lambda st: (0, 0))

    def tile_a(st):
        return jnp.minimum(st, n_t - 1)

    def tile_b(st):
        return jnp.maximum(st - 1, 0)

    in_specs = ([pl.BlockSpec((1, ts, d), lambda st: (tile_a(st) // n_s, tile_a(st) % n_s, 0))]
                + [of_layer(weights[k]) for k in names]
                + [pl.BlockSpec((1, 1) + kt.shape[2:],
                                lambda st: (layer, tile_b(st) // n_s, 0, 0)),
                   pl.BlockSpec((1, 1) + v.shape[2:],
                                lambda st: (layer, tile_b(st) // n_s, 0, 0))]
                + [of_layer(weights[k]) for k in names2])
    out_specs = [
        pl.BlockSpec((1, ts, d), lambda st: (tile_b(st) // n_s, tile_b(st) % n_s, 0)),
        pl.BlockSpec((TOP_K, ts), lambda st: (0, tile_b(st))),
        pl.BlockSpec((ts, LANES), lambda st: (tile_b(st), 0)),
        resident((n_t, LANES)), resident((n_t, LANES)), resident((n_t, LANES)),
        resident((SUBLANES, META_LANES)),
    ]
    out_shape = [
        jax.ShapeDtypeStruct((n_b, s, d), F32),
        jax.ShapeDtypeStruct((TOP_K, n_b * s), I32),
        jax.ShapeDtypeStruct((n_b * s, LANES), F32),
        jax.ShapeDtypeStruct((n_t, LANES), I32),
        jax.ShapeDtypeStruct((n_t, LANES), I32),
        jax.ShapeDtypeStruct((n_t, LANES), I32),
        jax.ShapeDtypeStruct((SUBLANES, META_LANES), I32),
    ]
    return pl.pallas_call(
        functools.partial(_mixer_kernel, alpha=alpha, n_s=n_s),
        grid=(n_t + 1,),
        in_specs=in_specs,
        out_specs=out_specs,
        out_shape=out_shape,
        scratch_shapes=[
            pltpu.VMEM((n_chunks, ts, PROJ_CHUNK), F32),
            pltpu.VMEM((n_chunks, ts, PROJ_CHUNK), F32),
            pltpu.VMEM((ts, d), F32),
            pltpu.VMEM((ts, d), F32),
            pltpu.VMEM((ts, d), BF16),
            pltpu.VMEM((ts + POOL_HALO, pw), F32),
            pltpu.VMEM((ts + CONV_HALO, cw), F32),
            pltpu.VMEM((ts, cw), F32),
            pltpu.VMEM((n_t, LANES), F32),
            pltpu.VMEM((n_t, LANES), F32),
            pltpu.VMEM((SUBLANES, LANES), F32),
        ],
        compiler_params=pltpu.CompilerParams(
            dimension_semantics=("arbitrary",), vmem_limit_bytes=VMEM_LIMIT),
        name="mixer",
    )(x, *[weights[k] for k in names], kt, v, *[weights[k] for k in names2])


def _start_segment_copies(seg_off_ref, seg_len_ref, dest_ref, tile, n_e, make_copy):
    for e in range(n_e):
        make_copy(pl.multiple_of(seg_off_ref[tile, e], SEG_ALIGN),
                  pl.multiple_of(dest_ref[tile, e], SEG_ALIGN),
                  pl.multiple_of(seg_len_ref[tile, e], SEG_ALIGN)).start()


def _wait_segment_copies(seg_off_ref, seg_len_ref, tile, n_e, make_copy):
    total = seg_off_ref[tile, n_e - 1] + seg_len_ref[tile, n_e - 1]
    make_copy(0, 0, pl.multiple_of(total, SEG_ALIGN)).wait()


def _dispatch_kernel(seg_off_ref, seg_len_ref, dest_ref, meta_ref,
                     ids_ref, off_ref, x1_ref, xs_ref, buf, zbuf, sem, zsem, *, n_e):
    tile = pl.program_id(0)
    n_tiles = pl.num_programs(0)
    slot = tile % 2
    ts = x1_ref.shape[0]
    rows = buf.shape[1]

    @pl.when(tile == 0)
    def _():
        zbuf[...] = jnp.zeros(zbuf.shape, BF16)

        def zero_copy(e):
            n = pl.multiple_of(meta_ref[META_PAD_LEN, e], SEG_ALIGN)
            dst = pl.multiple_of(meta_ref[META_PAD_DST, e], SEG_ALIGN)
            return n, pltpu.make_async_copy(zbuf.at[pl.ds(0, n)], xs_ref.at[pl.ds(dst, n)], zsem)

        def zstart(e, c):
            n, cp = zero_copy(e)
            pl.when(n > 0)(cp.start)
            return c

        def zwait(e, c):
            n, cp = zero_copy(e)
            pl.when(n > 0)(cp.wait)
            return c

        lax.fori_loop(0, n_e, zstart, 0)
        lax.fori_loop(0, n_e, zwait, 0)

    def copies_from(s):
        def make_copy(off, dst, n):
            return pltpu.make_async_copy(buf.at[s, pl.ds(off, n)], xs_ref.at[pl.ds(dst, n)],
                                         sem.at[s])
        return make_copy

    @pl.when(tile >= 2)
    def _():
        _wait_segment_copies(seg_off_ref, seg_len_ref, tile - 2, n_e, copies_from(slot))

    ids = ids_ref[...]
    e_iota = lax.broadcasted_iota(I32, (n_e, ts), 0)
    hots = [(e_iota == ids[k:k + 1, :]).astype(F32) for k in range(TOP_K)]
    hot = hots[0] + hots[1] + hots[2] + hots[3]
    off_row = off_ref[pl.ds(tile, 1), :].astype(F32)
    eye = (lax.broadcasted_iota(I32, (n_e, LANES), 0)
           == lax.broadcasted_iota(I32, (n_e, LANES), 1))
    off_col = jnp.sum(jnp.where(eye, off_row, 0.0), axis=1, keepdims=True)
    upper = (lax.broadcasted_iota(I32, (ts, ts), 0)
             < lax.broadcasted_iota(I32, (ts, ts), 1)).astype(BF16)
    place = _dot(hot.astype(BF16), upper) + off_col
    pos = [jnp.sum(hots[k] * place, axis=0, keepdims=True).astype(I32)
           for k in range(TOP_K)]
    x1b = x1_ref[...].astype(BF16)
    for r0 in range(0, rows, PERM_ROWS):
        r_iota = r0 + lax.broadcasted_iota(I32, (PERM_ROWS, ts), 0)
        hit = (r_iota == pos[0]) | (r_iota == pos[1]) | (r_iota == pos[2]) | (r_iota == pos[3])
        perm = jnp.where(hit, 1.0, 0.0).astype(BF16)
        buf[slot, r0:r0 + PERM_ROWS] = _dot(perm, x1b).astype(BF16)
    _start_segment_copies(seg_off_ref, seg_len_ref, dest_ref, tile, n_e, copies_from(slot))

    @pl.when(tile == n_tiles - 1)
    def _():
        @pl.when(tile >= 1)
        def _():
            _wait_segment_copies(seg_off_ref, seg_len_ref, tile - 1, n_e, copies_from(1 - slot))

        _wait_segment_copies(seg_off_ref, seg_len_ref, tile, n_e, copies_from(slot))


def _tile_rows(ts, n_e):
    return TOP_K * ts + n_e * SEG_ALIGN


def _dispatch(tables, meta, ids, x1, n_rows, n_e):
    n_tok, d = x1.shape
    ts = TOKEN_TILE
    n_t = n_tok // ts
    rows = _tile_rows(ts, n_e)
    return pl.pallas_call(
        functools.partial(_dispatch_kernel, n_e=n_e),
        grid_spec=pltpu.PrefetchScalarGridSpec(
            num_scalar_prefetch=4,
            grid=(n_t,),
            in_specs=[
                pl.BlockSpec((TOP_K, ts), lambda t, *_: (0, t)),
                pl.BlockSpec((n_t, LANES), lambda t, *_: (0, 0)),
                pl.BlockSpec((ts, d), lambda t, *_: (t, 0)),
            ],
            out_specs=pl.BlockSpec(memory_space=pl.ANY),
            scratch_shapes=[pltpu.VMEM((2, rows, d), BF16), pltpu.VMEM((EXPERT_BLOCK, d), BF16),
                            pltpu.SemaphoreType.DMA((2,)), pltpu.SemaphoreType.DMA(())],
        ),
        out_shape=jax.ShapeDtypeStruct((n_rows, d), BF16),
        compiler_params=pltpu.CompilerParams(
            dimension_semantics=("arbitrary",), vmem_limit_bytes=VMEM_LIMIT),
        name="dispatch",
    )(*tables, meta, ids, tables[0], x1)


def _expert_kernel(meta_ref, xs_ref, wup_ref, bup_ref, wdn_ref, bdn_ref,
                   ys_ref, wup_bf, wdn_bf):
    j = pl.program_id(0)
    f = wdn_ref.shape[2]

    @pl.when(j < meta_ref[META_N_USED, 0])
    def _():
        prev = meta_ref[META_BLK_E, jnp.maximum(j - 1, 0)]

        @pl.when((j == 0) | (meta_ref[META_BLK_E, j] != prev))
        def _():
            wup_bf[...] = wup_ref[0, 0].astype(BF16)
            wdn_bf[...] = wdn_ref[0, 0].astype(BF16)

        x = xs_ref[...]

        def up(c0):
            glu = _dot(x, wup_bf[:, c0:c0 + FF_CHUNK]) + bup_ref[0, 0, :, c0:c0 + FF_CHUNK]
            lin = (_dot(x, wup_bf[:, f + c0:f + c0 + FF_CHUNK])
                   + bup_ref[0, 0, :, f + c0:f + c0 + FF_CHUNK])
            return glu, lin

        y = jnp.broadcast_to(bdn_ref[0, 0], ys_ref.shape)
        nxt = up(0)
        for c0 in range(0, f, FF_CHUNK):
            glu, lin = nxt
            if c0 + FF_CHUNK < f:
                nxt = up(c0 + FF_CHUNK)
            glu = jnp.minimum(glu, SWIGLU_LIMIT)
            lin = jnp.clip(lin, -SWIGLU_LIMIT, SWIGLU_LIMIT)
            act = glu * jax.nn.sigmoid(SWIGLU_ALPHA * glu) * (lin + 1.0)
            y = y + _dot(act.astype(BF16), wdn_bf[c0:c0 + FF_CHUNK, :])
        ys_ref[...] = y.astype(BF16)


def _experts(meta, xs, w_up, b_up, w_down, b_down, layer):
    n_rows, d = xs.shape
    f2 = w_up.shape[-1]
    f = f2 // 2
    bm = EXPERT_BLOCK
    n_blk = n_rows // bm

    def last_used(j, meta):
        return jnp.maximum(jnp.minimum(j, meta[META_N_USED, 0] - 1), 0)

    def row_blk(j, meta):
        return (last_used(j, meta), 0)

    def w_blk(j, meta):
        return (layer, meta[META_BLK_E, last_used(j, meta)], 0, 0)

    return pl.pallas_call(
        _expert_kernel,
        grid_spec=pltpu.PrefetchScalarGridSpec(
            num_scalar_prefetch=1,
            grid=(n_blk,),
            in_specs=[
                pl.BlockSpec((bm, d), row_blk),
                pl.BlockSpec((1, 1, d, f2), w_blk),
                pl.BlockSpec((1, 1, 1, f2), w_blk),
                pl.BlockSpec((1, 1, f, d), w_blk),
                pl.BlockSpec((1, 1, 1, d), w_blk),
            ],
            out_specs=pl.BlockSpec((bm, d), row_blk),
            scratch_shapes=[pltpu.VMEM((d, f2), BF16), pltpu.VMEM((f, d), BF16)],
        ),
        out_shape=jax.ShapeDtypeStruct((n_rows, d), BF16),
        compiler_params=pltpu.CompilerParams(
            dimension_semantics=("arbitrary",), vmem_limit_bytes=VMEM_LIMIT),
        name="experts",
    )(meta, xs, w_up, b_up, w_down, b_down)


def _combine_kernel(seg_off_ref, seg_len_ref, dest_ref,
                    route_ref, off_ref, x1_ref, g2_ref, b2_ref, ys_ref,
                    out_ref, buf, sem, *, alpha, n_e):
    tile = pl.program_id(0)
    n_tiles = pl.num_programs(0)
    slot = tile % 2
    ts = x1_ref.shape[0]
    rows = buf.shape[1]

    def copies_into(s):
        def make_copy(off, dst, n):
            return pltpu.make_async_copy(ys_ref.at[pl.ds(dst, n)], buf.at[s, pl.ds(off, n)],
                                         sem.at[s])
        return make_copy

    @pl.when(tile == 0)
    def _():
        buf[...] = jnp.zeros(buf.shape, BF16)
        _start_segment_copies(seg_off_ref, seg_len_ref, dest_ref, tile, n_e, copies_into(slot))

    nxt = jnp.minimum(tile + 1, n_tiles - 1)
    _start_segment_copies(seg_off_ref, seg_len_ref, dest_ref, nxt, n_e, copies_into(1 - slot))

    route = route_ref[...]
    l_iota = lax.broadcasted_iota(I32, (ts, LANES), 1)
    hots = [(l_iota == route[:, k:k + 1].astype(I32)).astype(F32) for k in range(TOP_K)]
    hot = hots[0] + hots[1] + hots[2] + hots[3]
    lower = (lax.broadcasted_iota(I32, (ts, ts), 1)
             < lax.broadcasted_iota(I32, (ts, ts), 0)).astype(BF16)
    place = _dot(lower, hot.astype(BF16)) + off_ref[pl.ds(tile, 1), :].astype(F32)
    pos = [jnp.sum(hots[k] * place, axis=1, keepdims=True).astype(I32)
           for k in range(TOP_K)]

    _wait_segment_copies(seg_off_ref, seg_len_ref, tile, n_e, copies_into(slot))
    y = jnp.zeros(out_ref.shape, F32)
    for r0 in range(0, rows, PERM_ROWS):
        r_iota = r0 + lax.broadcasted_iota(I32, (ts, PERM_ROWS), 1)
        weight = jnp.zeros((ts, PERM_ROWS), F32)
        for k in range(TOP_K):
            weight = jnp.where(r_iota == pos[k], route[:, TOP_K + k:TOP_K + k + 1], weight)
        y = y + _dot(weight.astype(BF16), buf[slot, r0:r0 + PERM_ROWS])
    out_ref[...] = _layer_norm(alpha * x1_ref[...] + y, g2_ref[0], b2_ref[0])

    @pl.when(tile == n_tiles - 1)
    def _():
        _wait_segment_copies(seg_off_ref, seg_len_ref, tile, n_e, copies_into(1 - slot))


def _combine(tables, route_t, x1, g2, b2, ys, layer, alpha, n_e):
    n_tok, d = x1.shape
    ts = TOKEN_TILE
    n_t = n_tok // ts
    rows = _tile_rows(ts, n_e)
    return pl.pallas_call(
        functools.partial(_combine_kernel, alpha=alpha, n_e=n_e),
        grid_spec=pltpu.PrefetchScalarGridSpec(
            num_scalar_prefetch=3,
            grid=(n_t,),
            in_specs=[
                pl.BlockSpec((ts, LANES), lambda t, *_: (t, 0)),
                pl.BlockSpec((n_t, LANES), lambda t, *_: (0, 0)),
                pl.BlockSpec((ts, d), lambda t, *_: (t, 0)),
                pl.BlockSpec((1, 1, d), lambda t, *_: (layer, 0, 0)),
                pl.BlockSpec((1, 1, d), lambda t, *_: (layer, 0, 0)),
                pl.BlockSpec(memory_space=pl.ANY),
            ],
            out_specs=pl.BlockSpec((ts, d), lambda t, *_: (t, 0)),
            scratch_shapes=[pltpu.VMEM((2, rows, d), BF16), pltpu.SemaphoreType.DMA((2,))],
        ),
        out_shape=jax.ShapeDtypeStruct((n_tok, d), F32),
        compiler_params=pltpu.CompilerParams(
            dimension_semantics=("arbitrary",), vmem_limit_bytes=VMEM_LIMIT),
        name="combine",
    )(*tables, route_t, tables[0], x1, g2, b2, ys)


def kernel(x, mem, mem_ln_g, mem_ln_b, w_in, b_in, pool_w, pool_scale, pool_proj, conv_dw, conv_dw_b, conv_ln_g, conv_ln_b, conv_pw, conv_pw_b, w_kv, attn_o, w_out, b_out, ln1_g, ln1_b, router_w, router_b, exp_up, exp_up_b, exp_down, exp_down_b, ln2_g, ln2_b):
    n_b, s, d = x.shape
    depth = w_in.shape[0]
    n_e = router_w.shape[-1]
    n_tok = n_b * s
    n_t = n_tok // TOKEN_TILE
    alpha = (2.0 * depth) ** 0.25
    n_rows = TOP_K * n_tok + n_t * n_e * SEG_ALIGN + n_e * EXPERT_BLOCK
    n_rows = -(-n_rows // EXPERT_BLOCK) * EXPERT_BLOCK
    assert n_rows // EXPERT_BLOCK <= META_LANES and n_e <= LANES

    def rows3(a):
        return a.reshape(a.shape[0], 1, a.shape[1])

    in_cols = w_in.shape[-1]
    assert in_cols % PROJ_CHUNK == 0
    n_chunks = in_cols // PROJ_CHUNK
    w_in_chunks = jnp.swapaxes(w_in.astype(BF16).reshape(depth, d, n_chunks, PROJ_CHUNK), 1, 2)
    router_pad = jnp.pad(router_w, ((0, 0), (0, 0), (0, LANES - n_e)))
    router_hi = router_pad.astype(BF16)
    weights = dict(
        w_in=w_in_chunks, b_in=b_in.reshape(depth, n_chunks, 1, PROJ_CHUNK),
        pool_w=pool_w.astype(BF16),
        pool_scale=rows3(pool_scale), pool_proj=pool_proj.astype(BF16), conv_dw=conv_dw,
        conv_dw_b=rows3(conv_dw_b), conv_ln_g=rows3(conv_ln_g), conv_ln_b=rows3(conv_ln_b),
        conv_pw=conv_pw.astype(BF16), conv_pw_b=rows3(conv_pw_b), attn_o=attn_o.astype(BF16),
        w_out=w_out.astype(BF16), b_out=rows3(b_out), ln1_g=rows3(ln1_g), ln1_b=rows3(ln1_b),
        router_hi=router_hi, router_lo=(router_pad - router_hi.astype(F32)).astype(BF16),
        router_b=router_b.reshape(depth, n_e, 1))
    up_b = exp_up_b.reshape(depth, n_e, 1, exp_up_b.shape[-1])
    down_b = exp_down_b.reshape(depth, n_e, 1, d)
    g2, b2 = rows3(ln2_g), rows3(ln2_b)

    kt_all, v_all = _memory_kv(mem, mem_ln_g, mem_ln_b, w_kv)
    for layer in range(depth):
        x1, ids, route_t, seg_off, seg_len, dest, meta = _mixer(
            x, weights, kt_all, v_all, layer, alpha)
        x1 = x1.reshape(n_tok, d)
        tables = (seg_off, seg_len, dest)
        xs = _dispatch(tables, meta, ids, x1, n_rows, n_e)
        ys = _experts(meta, xs, exp_up, up_b, exp_down, down_b, layer)
        x = _combine(tables, route_t, x1, g2, b2, ys, layer, alpha, n_e).reshape(n_b, s, d)
    return x
```

```python
import functools

import jax
import jax.numpy as jnp
from jax import lax
from jax.experimental import pallas as pl
from jax.experimental.pallas import tpu as pltpu

F32 = jnp.float32
BF16 = jnp.bfloat16
I32 = jnp.int32

POOL_WINDOWS = (2, 4, 8, 16)
POOL_HALO = 16
CONV_HALO = 32
HEAD_DIM = 128
TOP_K = 4
SWIGLU_LIMIT = 7.0
SWIGLU_ALPHA = 1.702
LN_EPS = 1e-5

LANES = 128
SUBLANES = 8
SEG_ALIGN = 16
TOKEN_TILE = 256
EXPERT_BLOCK = 512
FF_CHUNK = 256
CONV_ROWS = 64
PROJ_CHUNK = 256
LOOP_PROJ_CHUNKS = 3
PERM_ROWS = 256
META_LANES = 512
META_BLK_E, META_PAD_DST, META_PAD_LEN, META_N_USED = 0, 1, 2, 3
VMEM_LIMIT = 56 * 1024 * 1024


def _layer_norm(x, g, b):
    mu = jnp.mean(x, axis=-1, keepdims=True)
    xc = x - mu
    var = jnp.mean(xc * xc, axis=-1, keepdims=True)
    return xc * lax.rsqrt(var + LN_EPS) * g + b


def _dot(a, b):
    return jnp.dot(a, b, preferred_element_type=F32)


def _dot_exact(a, b):
    return jnp.dot(a, b, preferred_element_type=F32, precision=lax.Precision.HIGHEST)


def _kv_kernel(mem_ref, g_ref, b_ref, wkv_ref, kt_ref, v_ref):
    a = kt_ref.shape[2]
    mem_n = _layer_norm(mem_ref[0], g_ref[...], b_ref[...]).astype(BF16)
    kv = _dot(mem_n, wkv_ref[0].astype(BF16))
    kt_ref[0, 0] = kv[:, :a].T.astype(BF16)
    v_ref[0, 0] = kv[:, a:].astype(BF16)


def _memory_kv(mem, g, b, w_kv):
    n_b, m, d = mem.shape
    n_l, _, a2 = w_kv.shape
    a = a2 // 2
    return pl.pallas_call(
        _kv_kernel,
        grid=(n_l, n_b),
        in_specs=[
            pl.BlockSpec((1, m, d), lambda l, bb: (bb, 0, 0)),
            pl.BlockSpec((1, d), lambda l, bb: (0, 0)),
            pl.BlockSpec((1, d), lambda l, bb: (0, 0)),
            pl.BlockSpec((1, d, a2), lambda l, bb: (l, 0, 0)),
        ],
        out_specs=[
            pl.BlockSpec((1, 1, a, m), lambda l, bb: (l, bb, 0, 0)),
            pl.BlockSpec((1, 1, m, a), lambda l, bb: (l, bb, 0, 0)),
        ],
        out_shape=[
            jax.ShapeDtypeStruct((n_l, n_b, a, m), BF16),
            jax.ShapeDtypeStruct((n_l, n_b, m, a), BF16),
        ],
        compiler_params=pltpu.CompilerParams(
            dimension_semantics=("arbitrary", "arbitrary"), vmem_limit_bytes=VMEM_LIMIT),
        name="memory_kv",
    )(mem, g.reshape(1, d), b.reshape(1, d), w_kv)


def _routing_tables(padded, prefix, n_e, seg_off_ref, seg_len_ref, dest_ref, meta_ref):
    row = lax.broadcasted_iota(I32, (LANES, LANES), 0)
    col = lax.broadcasted_iota(I32, (LANES, LANES), 1)
    seg_off = _dot_exact(padded, (row < col).astype(F32))
    tot = jnp.sum(padded, axis=0, keepdims=True)
    ptot = jnp.ceil(tot * (1.0 / EXPERT_BLOCK)) * EXPERT_BLOCK
    e_end = _dot_exact(jnp.broadcast_to(ptot, (SUBLANES, LANES)),
                       (row <= col).astype(F32))[0:1]
    e_start = e_end - ptot
    seg_off_ref[...] = seg_off.astype(I32)
    seg_len_ref[...] = padded.astype(I32)
    dest_ref[...] = (prefix + e_start).astype(I32)

    e_end_col = jnp.sum(jnp.where(row == col, jnp.broadcast_to(e_end, (LANES, LANES)), 0.0),
                        axis=1, keepdims=True)
    blk_row = (lax.broadcasted_iota(I32, (LANES, META_LANES), 1) * EXPERT_BLOCK).astype(F32)
    is_real = lax.broadcasted_iota(I32, (LANES, META_LANES), 0) < n_e
    blk_e = jnp.sum(jnp.where(is_real & (e_end_col <= blk_row), 1.0, 0.0), axis=0, keepdims=True)
    blk_e = jnp.minimum(blk_e, n_e - 1.0)
    lane = lax.broadcasted_iota(I32, (1, LANES), 1)
    n_used = jnp.sum(jnp.where(lane == n_e - 1, e_end, 0.0), axis=1, keepdims=True) * (
        1.0 / EXPERT_BLOCK)

    def wide(r):
        return jnp.concatenate([r, jnp.zeros((1, META_LANES - LANES), F32)], axis=1)

    sub = lax.broadcasted_iota(I32, (SUBLANES, META_LANES), 0)
    meta = jnp.zeros((SUBLANES, META_LANES), F32)
    for r, val in ((META_BLK_E, blk_e), (META_PAD_DST, wide(e_start + tot)),
                   (META_PAD_LEN, wide(ptot - tot)),
                   (META_N_USED, jnp.broadcast_to(n_used, (1, META_LANES)))):
        meta = jnp.where(sub == r, jnp.broadcast_to(val, (SUBLANES, META_LANES)), meta)
    meta_ref[...] = meta.astype(I32)


def _load_matrices_bf16(pieces, stage, sem):
    def fetch(p):
        src, s_idx, _, _, (r, c) = pieces[p]
        return pltpu.make_async_copy(src.at[s_idx], stage.at[p % 2, 0:r, 0:c], sem.at[p % 2])

    fetch(0).start()
    for p, (_, _, dst, d_idx, (r, c)) in enumerate(pieces):
        if p + 1 < len(pieces):
            fetch(p + 1).start()
        fetch(p).wait()
        dst[d_idx] = stage[p % 2, 0:r, 0:c].astype(BF16)


def _mixer_kernel(x_ref, w_in_hbm, b_in_ref, pool_w_hbm, pool_scale_ref, pool_proj_hbm,
                  dw_ref, dwb_ref, cg_ref, cb_ref, cpw_hbm, cpwb_ref,
                  kt_ref, v_ref, ao_hbm, wout_hbm, bout_ref, g1_ref, b1_ref,
                  wr_hi_ref, wr_lo_ref, rb_ref,
                  x1_ref, ids_ref, route_t_ref, seg_off_ref, seg_len_ref, dest_ref, meta_ref,
                  w_in_ref, pool_w_ref, pool_proj_ref, cpw_ref, ao_ref, wout_ref, stage, stage_sem,
                  h_scr, x_scr, xb_scr, pool_ext, conv_ext, conv_out,
                  cnt_scr, pre_scr, run_scr,
                  *, alpha, n_s, layer):
    s = pl.program_id(0)
    n_tiles = pl.num_programs(0) - 1
    cur = (s + 1) % 2
    nxt = s % 2
    tile = jnp.maximum(s - 1, 0)
    i = tile % n_s
    ts, d = x_ref.shape[1], x_ref.shape[2]
    n_chunks = w_in_ref.shape[0]
    pw = pool_ext.shape[1]
    cw = conv_ext.shape[1]
    aw = kt_ref.shape[2]
    n_e = rb_ref.shape[1]
    c_conv = pw
    c_q = pw + 2 * cw
    c_gate = c_q + aw

    @pl.when(i == 0)
    def _():
        pool_ext[0:POOL_HALO, :] = jnp.zeros((POOL_HALO, pw), F32)
        conv_ext[0:CONV_HALO, :] = jnp.zeros((CONV_HALO, cw), F32)

    @pl.when(s == 0)
    def _():
        run_scr[...] = jnp.zeros(run_scr.shape, F32)
        h_scr[1] = jnp.zeros(h_scr.shape[1:], F32)
        x_scr[1] = jnp.zeros(x_scr.shape[1:], F32)
        pieces = []

        def add(src, dst, rows, cols, dst_chunked=False):
            for j in range(cols // PROJ_CHUNK):
                c0 = j * PROJ_CHUNK
                d_idx = (j,) if dst_chunked else (slice(None), slice(c0, c0 + PROJ_CHUNK))
                pieces.append((src, (layer, slice(None), slice(c0, c0 + PROJ_CHUNK)),
                               dst, d_idx, (rows, PROJ_CHUNK)))

        add(w_in_hbm, w_in_ref, d, n_chunks * PROJ_CHUNK, dst_chunked=True)
        add(pool_proj_hbm, pool_proj_ref, pw, d)
        add(cpw_hbm, cpw_ref, cw, d)
        add(ao_hbm, ao_ref, aw, d)
        add(wout_hbm, wout_ref, d, d)
        gd_ = pw // len(POOL_WINDOWS)
        for g in range(len(POOL_WINDOWS)):
            pieces.append((pool_w_hbm, (layer, g), pool_w_ref, (g,), (gd_, gd_)))
        _load_matrices_bf16(pieces, stage, stage_sem)

    x_in = x_ref[0]
    x_scr[nxt] = x_in
    xb_scr[...] = x_in.astype(BF16)

    x = x_scr[cur]

    def proj(lo, hi):
        return jnp.concatenate(
            [h_scr[cur, c] for c in range(lo // PROJ_CHUNK, hi // PROJ_CHUNK)], axis=1)

    glu = proj(c_conv, c_conv + cw) * jax.nn.sigmoid(proj(c_conv + cw, c_conv + 2 * cw))
    conv_ext[CONV_HALO:CONV_HALO + ts, :] = glu
    taps = dw_ref.shape[1]
    base = CONV_HALO - (taps - 1)
    span = CONV_ROWS + CONV_HALO
    n_row_chunks = ts // CONV_ROWS
    chunks_per_iter = min(LOOP_PROJ_CHUNKS, n_chunks // n_row_chunks)
    later_chunks = list(range(n_row_chunks * chunks_per_iter, n_chunks))

    def project(c):
        h_scr[nxt, c] = _dot(xb_scr[...], w_in_ref[c]) + b_in_ref[0, c]

    def project_some(n):
        for _ in range(min(n, len(later_chunks))):
            project(later_chunks.pop(0))

    def conv_rows_and_projection(j, carry):
        r0 = pl.multiple_of(j * CONV_ROWS, CONV_ROWS)
        for c0 in range(0, cw, LANES):
            window = conv_ext[pl.ds(r0, span), c0:c0 + LANES]
            acc = jnp.broadcast_to(dwb_ref[0, :, c0:c0 + LANES], (CONV_ROWS, LANES))
            for res in range(SUBLANES):
                offs = [o for o in range(base, base + taps) if o % SUBLANES == res]
                if not offs:
                    continue
                shifted = window if res == 0 else pltpu.roll(window, span - res, axis=0)
                for o in offs:
                    k = o - base
                    acc = acc + dw_ref[0, k:k + 1, c0:c0 + LANES] * shifted[
                        o - res:o - res + CONV_ROWS]
            conv_out[pl.ds(r0, CONV_ROWS), c0:c0 + LANES] = acc
        for m in range(chunks_per_iter):
            project(j * chunks_per_iter + m)
        return carry

    lax.fori_loop(0, n_row_chunks, conv_rows_and_projection, 0)
    conv_ext[0:CONV_HALO, :] = conv_ext[ts:ts + CONV_HALO, :]
    hc = _layer_norm(conv_out[...], cg_ref[0], cb_ref[0])
    hc = hc * jax.nn.sigmoid(hc)
    project_some(2)
    y_conv = _dot(hc.astype(BF16), cpw_ref[...]) + cpwb_ref[0]

    u = proj(0, pw)
    pool_ext[POOL_HALO:POOL_HALO + ts, :] = u
    t_glob = i * ts + lax.broadcasted_iota(I32, (ts, 1), 0)
    gd = pw // len(POOL_WINDOWS)
    pooled = []
    for g, w in enumerate(POOL_WINDOWS):
        lo = g * gd
        ug = u[:, lo:lo + gd]
        run = pool_ext[:, lo:lo + gd]
        step = 1
        while step < w:
            run = run + pltpu.roll(run, step, axis=0)
            step *= 2
        cnt = jnp.minimum(t_glob + 1, w).astype(F32)
        pooled.append((run[POOL_HALO:POOL_HALO + ts] / cnt - ug).astype(BF16))
    project_some(2)
    mixed = [_dot(p, pool_w_ref[g]) for g, p in enumerate(pooled)]
    mixed = jnp.concatenate(mixed, axis=1) * pool_scale_ref[0]
    y_pool = _dot(mixed.astype(BF16), pool_proj_ref[...])
    pool_ext[0:POOL_HALO, :] = pool_ext[ts:ts + POOL_HALO, :]

    q = proj(c_q, c_q + aw)
    heads = []
    for h in range(aw // HEAD_DIM):
        lo = h * HEAD_DIM
        sc = _dot(q[:, lo:lo + HEAD_DIM].astype(BF16), kt_ref[0, 0, lo:lo + HEAD_DIM, :])
        sc = sc * (HEAD_DIM ** -0.5)
        p = jnp.exp(sc - jnp.max(sc, axis=-1, keepdims=True))
        o = _dot(p.astype(BF16), v_ref[0, 0, :, lo:lo + HEAD_DIM])
        heads.append(o / jnp.sum(p, axis=-1, keepdims=True))
    y_attn = _dot(jnp.concatenate(heads, axis=1).astype(BF16), ao_ref[...])

    merged = jax.nn.sigmoid(proj(c_gate, c_gate + d)) * y_pool
    merged = merged + jax.nn.sigmoid(proj(c_gate + d, c_gate + 2 * d)) * y_conv
    merged = merged + jax.nn.sigmoid(proj(c_gate + 2 * d, c_gate + 3 * d)) * y_attn
    project_some(2)
    out = _dot(merged.astype(BF16), wout_ref[...]) + bout_ref[0]
    x1 = _layer_norm(alpha * x + out, g1_ref[0], b1_ref[0])
    x1_ref[0] = x1

    x1_hi = x1.astype(BF16)
    x1_lo = (x1 - x1_hi.astype(F32)).astype(BF16)
    logits = _dot(x1_hi, wr_hi_ref[0]) + _dot(x1_lo, wr_hi_ref[0]) + _dot(x1_hi, wr_lo_ref[0])
    logits = logits.T[0:n_e, :] + rb_ref[0]
    e_iota = lax.broadcasted_iota(I32, (n_e, ts), 0)
    rank = jnp.zeros((n_e, ts), F32)
    for other in range(n_e):
        row = logits[other:other + 1, :]
        beats = (row > logits) | ((row == logits) & (e_iota > other))
        rank = rank + jnp.where(beats, 1.0, 0.0)
    hot = jnp.where(rank < TOP_K, 1.0, 0.0)
    top_v, top_i = [], []
    for k in range(TOP_K):
        sel = rank == k
        top_i.append(jnp.sum(jnp.where(sel, e_iota, 0), axis=0, keepdims=True))
        top_v.append(jnp.sum(jnp.where(sel, logits, 0.0), axis=0, keepdims=True))
    ex = [jnp.exp(v - top_v[0]) for v in top_v]
    den = ex[0] + ex[1] + ex[2] + ex[3]
    sub = lax.broadcasted_iota(I32, (SUBLANES, ts), 0)
    route = jnp.zeros((SUBLANES, ts), F32)
    for k in range(TOP_K):
        ids_ref[k:k + 1, :] = top_i[k]
        route = jnp.where(sub == k, top_i[k].astype(F32), route)
        route = jnp.where(sub == TOP_K + k, ex[k] / den, route)
    route_t_ref[...] = jnp.concatenate(
        [route, jnp.zeros((LANES - SUBLANES, ts), F32)], axis=0).T

    hot_wide = jnp.concatenate([hot, jnp.zeros((LANES - n_e, ts), F32)], axis=0).astype(BF16)
    cnt = lax.dot_general(jnp.ones((SUBLANES, ts), BF16), hot_wide, (((1,), (1,)), ((), ())),
                          preferred_element_type=F32)[0:1]
    is_expert = lax.broadcasted_iota(I32, (1, LANES), 1) < n_e
    padded = jnp.where(is_expert, jnp.maximum(jnp.ceil(cnt * (1.0 / SEG_ALIGN)), 1.0), 0.0)
    padded = padded * SEG_ALIGN
    padded = jnp.where(s >= 1, padded, 0.0)
    cnt_scr[pl.ds(tile, 1), :] = padded
    pre_scr[pl.ds(tile, 1), :] = run_scr[0:1, :]
    run_scr[0:1, :] = run_scr[0:1, :] + padded

    project_some(len(later_chunks))

    @pl.when(s == n_tiles)
    def _():
        _routing_tables(cnt_scr[...], pre_scr[...], n_e,
                        seg_off_ref, seg_len_ref, dest_ref, meta_ref)


def _mixer(x, weights, kt, v, layer, alpha):
    n_b, s, d = x.shape
    ts = TOKEN_TILE
    n_s = s // ts
    n_t = n_b * n_s
    names = ("w_in", "b_in", "pool_w", "pool_scale", "pool_proj", "conv_dw", "conv_dw_b",
             "conv_ln_g", "conv_ln_b", "conv_pw", "conv_pw_b")
    names2 = ("attn_o", "w_out", "b_out", "ln1_g", "ln1_b", "router_hi", "router_lo", "router_b")
    in_hbm = ("w_in", "pool_w", "pool_proj", "conv_pw", "attn_o", "w_out")
    pw = weights["pool_proj"].shape[1]
    cw = weights["conv_pw"].shape[1]
    aw = weights["attn_o"].shape[1]
    n_chunks = weights["w_in"].shape[2] // PROJ_CHUNK
    gd = pw // len(POOL_WINDOWS)

    def of_layer(arr):
        tail = (0,) * (arr.ndim - 1)
        return pl.BlockSpec((1,) + arr.shape[1:], lambda st: (layer,) + tail,
                            pipeline_mode=pl.Buffered(1))

    def spec(name):
        if name in in_hbm:
            return pl.BlockSpec(memory_space=pl.ANY)
        return of_layer(weights[name])

    def resident(shape):
        return pl.BlockSpec(shape, lambda st: (0, 0))

    def tile_a(st):
        return jnp.minimum(st, n_t - 1)

    def tile_b(st):
        return jnp.maximum(st - 1, 0)

    in_specs = ([pl.BlockSpec((1, ts, d), lambda st: (tile_a(st) // n_s, tile_a(st) % n_s, 0))]
                + [spec(k) for k in names]
                + [pl.BlockSpec((1, 1) + kt.shape[2:],
                                lambda st: (layer, tile_b(st) // n_s, 0, 0)),
                   pl.BlockSpec((1, 1) + v.shape[2:],
                                lambda st: (layer, tile_b(st) // n_s, 0, 0))]
                + [spec(k) for k in names2])
    out_specs = [
        pl.BlockSpec((1, ts, d), lambda st: (tile_b(st) // n_s, tile_b(st) % n_s, 0)),
        pl.BlockSpec((TOP_K, ts), lambda st: (0, tile_b(st))),
        pl.BlockSpec((ts, LANES), lambda st: (tile_b(st), 0)),
        resident((n_t, LANES)), resident((n_t, LANES)), resident((n_t, LANES)),
        resident((SUBLANES, META_LANES)),
    ]
    out_shape = [
        jax.ShapeDtypeStruct((n_b, s, d), F32),
        jax.ShapeDtypeStruct((TOP_K, n_b * s), I32),
        jax.ShapeDtypeStruct((n_b * s, LANES), F32),
        jax.ShapeDtypeStruct((n_t, LANES), I32),
        jax.ShapeDtypeStruct((n_t, LANES), I32),
        jax.ShapeDtypeStruct((n_t, LANES), I32),
        jax.ShapeDtypeStruct((SUBLANES, META_LANES), I32),
    ]
    return pl.pallas_call(
        functools.partial(_mixer_kernel, alpha=alpha, n_s=n_s, layer=layer),
        grid=(n_t + 1,),
        in_specs=in_specs,
        out_specs=out_specs,
        out_shape=out_shape,
        scratch_shapes=[
            pltpu.VMEM((n_chunks, d, PROJ_CHUNK), BF16),
            pltpu.VMEM((len(POOL_WINDOWS), gd, gd), BF16),
            pltpu.VMEM((pw, d), BF16),
            pltpu.VMEM((cw, d), BF16),
            pltpu.VMEM((aw, d), BF16),
            pltpu.VMEM((d, d), BF16),
            pltpu.VMEM((2, max(d, pw, cw, aw), PROJ_CHUNK), F32),
            pltpu.SemaphoreType.DMA((2,)),
            pltpu.VMEM((2, n_chunks, ts, PROJ_CHUNK), F32),
            pltpu.VMEM((2, ts, d), F32),
            pltpu.VMEM((ts, d), BF16),
            pltpu.VMEM((ts + POOL_HALO, pw), F32),
            pltpu.VMEM((ts + CONV_HALO, cw), F32),
            pltpu.VMEM((ts, cw), F32),
            pltpu.VMEM((n_t, LANES), F32),
            pltpu.VMEM((n_t, LANES), F32),
            pltpu.VMEM((SUBLANES, LANES), F32),
        ],
        compiler_params=pltpu.CompilerParams(
            dimension_semantics=("arbitrary",), vmem_limit_bytes=VMEM_LIMIT),
        name="mixer",
    )(x, *[weights[k] for k in names], kt, v, *[weights[k] for k in names2])


def _start_segment_copies(seg_off_ref, seg_len_ref, dest_ref, tile, n_e, make_copy):
    for e in range(n_e):
        make_copy(pl.multiple_of(seg_off_ref[tile, e], SEG_ALIGN),
                  pl.multiple_of(dest_ref[tile, e], SEG_ALIGN),
                  pl.multiple_of(seg_len_ref[tile, e], SEG_ALIGN)).start()


def _wait_segment_copies(seg_off_ref, seg_len_ref, tile, n_e, make_copy):
    total = seg_off_ref[tile, n_e - 1] + seg_len_ref[tile, n_e - 1]
    make_copy(0, 0, pl.multiple_of(total, SEG_ALIGN)).wait()


def _dispatch_kernel(seg_off_ref, seg_len_ref, dest_ref, meta_ref,
                     ids_ref, off_ref, x1_ref, xs_ref, buf, zbuf, sem, zsem, *, n_e):
    tile = pl.program_id(0)
    n_tiles = pl.num_programs(0)
    slot = tile % 2
    ts = x1_ref.shape[0]
    rows = buf.shape[1]

    @pl.when(tile == 0)
    def _():
        zbuf[...] = jnp.zeros(zbuf.shape, BF16)

        def zero_copy(e):
            n = pl.multiple_of(meta_ref[META_PAD_LEN, e], SEG_ALIGN)
            dst = pl.multiple_of(meta_ref[META_PAD_DST, e], SEG_ALIGN)
            return n, pltpu.make_async_copy(zbuf.at[pl.ds(0, n)], xs_ref.at[pl.ds(dst, n)], zsem)

        def zstart(e, c):
            n, cp = zero_copy(e)
            pl.when(n > 0)(cp.start)
            return c

        def zwait(e, c):
            n, cp = zero_copy(e)
            pl.when(n > 0)(cp.wait)
            return c

        lax.fori_loop(0, n_e, zstart, 0)
        lax.fori_loop(0, n_e, zwait, 0)

    def copies_from(s):
        def make_copy(off, dst, n):
            return pltpu.make_async_copy(buf.at[s, pl.ds(off, n)], xs_ref.at[pl.ds(dst, n)],
                                         sem.at[s])
        return make_copy

    @pl.when(tile >= 2)
    def _():
        _wait_segment_copies(seg_off_ref, seg_len_ref, tile - 2, n_e, copies_from(slot))

    ids = ids_ref[...]
    e_iota = lax.broadcasted_iota(I32, (n_e, ts), 0)
    hots = [(e_iota == ids[k:k + 1, :]).astype(F32) for k in range(TOP_K)]
    hot = hots[0] + hots[1] + hots[2] + hots[3]
    off_row = off_ref[pl.ds(tile, 1), :].astype(F32)
    eye = (lax.broadcasted_iota(I32, (n_e, LANES), 0)
           == lax.broadcasted_iota(I32, (n_e, LANES), 1))
    off_col = jnp.sum(jnp.where(eye, off_row, 0.0), axis=1, keepdims=True)
    upper = (lax.broadcasted_iota(I32, (ts, ts), 0)
             < lax.broadcasted_iota(I32, (ts, ts), 1)).astype(BF16)
    place = _dot(hot.astype(BF16), upper) + off_col
    pos = [jnp.sum(hots[k] * place, axis=0, keepdims=True).astype(I32)
           for k in range(TOP_K)]
    x1b = x1_ref[...].astype(BF16)
    for r0 in range(0, rows, PERM_ROWS):
        r_iota = r0 + lax.broadcasted_iota(I32, (PERM_ROWS, ts), 0)
        hit = (r_iota == pos[0]) | (r_iota == pos[1]) | (r_iota == pos[2]) | (r_iota == pos[3])
        perm = jnp.where(hit, 1.0, 0.0).astype(BF16)
        buf[slot, r0:r0 + PERM_ROWS] = _dot(perm, x1b).astype(BF16)
    _start_segment_copies(seg_off_ref, seg_len_ref, dest_ref, tile, n_e, copies_from(slot))

    @pl.when(tile == n_tiles - 1)
    def _():
        @pl.when(tile >= 1)
        def _():
            _wait_segment_copies(seg_off_ref, seg_len_ref, tile - 1, n_e, copies_from(1 - slot))

        _wait_segment_copies(seg_off_ref, seg_len_ref, tile, n_e, copies_from(slot))


def _tile_rows(ts, n_e):
    return TOP_K * ts + n_e * SEG_ALIGN


def _dispatch(tables, meta, ids, x1, n_rows, n_e):
    n_tok, d = x1.shape
    ts = TOKEN_TILE
    n_t = n_tok // ts
    rows = _tile_rows(ts, n_e)
    return pl.pallas_call(
        functools.partial(_dispatch_kernel, n_e=n_e),
        grid_spec=pltpu.PrefetchScalarGridSpec(
            num_scalar_prefetch=4,
            grid=(n_t,),
            in_specs=[
                pl.BlockSpec((TOP_K, ts), lambda t, *_: (0, t)),
                pl.BlockSpec((n_t, LANES), lambda t, *_: (0, 0)),
                pl.BlockSpec((ts, d), lambda t, *_: (t, 0)),
            ],
            out_specs=pl.BlockSpec(memory_space=pl.ANY),
            scratch_shapes=[pltpu.VMEM((2, rows, d), BF16), pltpu.VMEM((EXPERT_BLOCK, d), BF16),
                            pltpu.SemaphoreType.DMA((2,)), pltpu.SemaphoreType.DMA(())],
        ),
        out_shape=jax.ShapeDtypeStruct((n_rows, d), BF16),
        compiler_params=pltpu.CompilerParams(
            dimension_semantics=("arbitrary",), vmem_limit_bytes=VMEM_LIMIT),
        name="dispatch",
    )(*tables, meta, ids, tables[0], x1)


def _expert_kernel(meta_ref, xs_ref, wup_ref, bup_ref, wdn_ref, bdn_ref,
                   ys_ref, wup_bf, wdn_bf):
    j = pl.program_id(0)
    f = wdn_ref.shape[2]

    @pl.when(j < meta_ref[META_N_USED, 0])
    def _():
        prev = meta_ref[META_BLK_E, jnp.maximum(j - 1, 0)]

        @pl.when((j == 0) | (meta_ref[META_BLK_E, j] != prev))
        def _():
            wup_bf[...] = wup_ref[0, 0].astype(BF16)
            wdn_bf[...] = wdn_ref[0, 0].astype(BF16)

        x = xs_ref[...]

        def up(c0):
            glu = _dot(x, wup_bf[:, c0:c0 + FF_CHUNK]) + bup_ref[0, 0, :, c0:c0 + FF_CHUNK]
            lin = (_dot(x, wup_bf[:, f + c0:f + c0 + FF_CHUNK])
                   + bup_ref[0, 0, :, f + c0:f + c0 + FF_CHUNK])
            return glu, lin

        y = jnp.broadcast_to(bdn_ref[0, 0], ys_ref.shape)
        nxt = up(0)
        for c0 in range(0, f, FF_CHUNK):
            glu, lin = nxt
            if c0 + FF_CHUNK < f:
                nxt = up(c0 + FF_CHUNK)
            glu = jnp.minimum(glu, SWIGLU_LIMIT)
            lin = jnp.clip(lin, -SWIGLU_LIMIT, SWIGLU_LIMIT)
            act = glu * jax.nn.sigmoid(SWIGLU_ALPHA * glu) * (lin + 1.0)
            y = y + _dot(act.astype(BF16), wdn_bf[c0:c0 + FF_CHUNK, :])
        ys_ref[...] = y.astype(BF16)


def _experts(meta, xs, w_up, b_up, w_down, b_down, layer):
    n_rows, d = xs.shape
    f2 = w_up.shape[-1]
    f = f2 // 2
    bm = EXPERT_BLOCK
    n_blk = n_rows // bm

    def last_used(j, meta):
        return jnp.maximum(jnp.minimum(j, meta[META_N_USED, 0] - 1), 0)

    def row_blk(j, meta):
        return (last_used(j, meta), 0)

    def w_blk(j, meta):
        return (layer, meta[META_BLK_E, last_used(j, meta)], 0, 0)

    return pl.pallas_call(
        _expert_kernel,
        grid_spec=pltpu.PrefetchScalarGridSpec(
            num_scalar_prefetch=1,
            grid=(n_blk,),
            in_specs=[
                pl.BlockSpec((bm, d), row_blk),
                pl.BlockSpec((1, 1, d, f2), w_blk),
                pl.BlockSpec((1, 1, 1, f2), w_blk),
                pl.BlockSpec((1, 1, f, d), w_blk),
                pl.BlockSpec((1, 1, 1, d), w_blk),
            ],
            out_specs=pl.BlockSpec((bm, d), row_blk),
            scratch_shapes=[pltpu.VMEM((d, f2), BF16), pltpu.VMEM((f, d), BF16)],
        ),
        out_shape=jax.ShapeDtypeStruct((n_rows, d), BF16),
        compiler_params=pltpu.CompilerParams(
            dimension_semantics=("arbitrary",), vmem_limit_bytes=VMEM_LIMIT),
        name="experts",
    )(meta, xs, w_up, b_up, w_down, b_down)


def _combine_kernel(seg_off_ref, seg_len_ref, dest_ref,
                    route_ref, off_ref, x1_ref, g2_ref, b2_ref, ys_ref,
                    out_ref, buf, sem, *, alpha, n_e):
    tile = pl.program_id(0)
    n_tiles = pl.num_programs(0)
    slot = tile % 2
    ts = x1_ref.shape[0]
    rows = buf.shape[1]

    def copies_into(s):
        def make_copy(off, dst, n):
            return pltpu.make_async_copy(ys_ref.at[pl.ds(dst, n)], buf.at[s, pl.ds(off, n)],
                                         sem.at[s])
        return make_copy

    @pl.when(tile == 0)
    def _():
        buf[...] = jnp.zeros(buf.shape, BF16)
        _start_segment_copies(seg_off_ref, seg_len_ref, dest_ref, tile, n_e, copies_into(slot))

    nxt = jnp.minimum(tile + 1, n_tiles - 1)
    _start_segment_copies(seg_off_ref, seg_len_ref, dest_ref, nxt, n_e, copies_into(1 - slot))

    route = route_ref[...]
    l_iota = lax.broadcasted_iota(I32, (ts, LANES), 1)
    hots = [(l_iota == route[:, k:k + 1].astype(I32)).astype(F32) for k in range(TOP_K)]
    hot = hots[0] + hots[1] + hots[2] + hots[3]
    lower = (lax.broadcasted_iota(I32, (ts, ts), 1)
             < lax.broadcasted_iota(I32, (ts, ts), 0)).astype(BF16)
    place = _dot(lower, hot.astype(BF16)) + off_ref[pl.ds(tile, 1), :].astype(F32)
    pos = [jnp.sum(hots[k] * place, axis=1, keepdims=True).astype(I32)
           for k in range(TOP_K)]

    _wait_segment_copies(seg_off_ref, seg_len_ref, tile, n_e, copies_into(slot))
    y = jnp.zeros(out_ref.shape, F32)
    for r0 in range(0, rows, PERM_ROWS):
        r_iota = r0 + lax.broadcasted_iota(I32, (ts, PERM_ROWS), 1)
        weight = jnp.zeros((ts, PERM_ROWS), F32)
        for k in range(TOP_K):
            weight = jnp.where(r_iota == pos[k], route[:, TOP_K + k:TOP_K + k + 1], weight)
        y = y + _dot(weight.astype(BF16), buf[slot, r0:r0 + PERM_ROWS])
    out_ref[...] = _layer_norm(alpha * x1_ref[...] + y, g2_ref[0], b2_ref[0])

    @pl.when(tile == n_tiles - 1)
    def _():
        _wait_segment_copies(seg_off_ref, seg_len_ref, tile, n_e, copies_into(1 - slot))


def _combine(tables, route_t, x1, g2, b2, ys, layer, alpha, n_e):
    n_tok, d = x1.shape
    ts = TOKEN_TILE
    n_t = n_tok // ts
    rows = _tile_rows(ts, n_e)
    return pl.pallas_call(
        functools.partial(_combine_kernel, alpha=alpha, n_e=n_e),
        grid_spec=pltpu.PrefetchScalarGridSpec(
            num_scalar_prefetch=3,
            grid=(n_t,),
            in_specs=[
                pl.BlockSpec((ts, LANES), lambda t, *_: (t, 0)),
                pl.BlockSpec((n_t, LANES), lambda t, *_: (0, 0)),
                pl.BlockSpec((ts, d), lambda t, *_: (t, 0)),
                pl.BlockSpec((1, 1, d), lambda t, *_: (layer, 0, 0)),
                pl.BlockSpec((1, 1, d), lambda t, *_: (layer, 0, 0)),
                pl.BlockSpec(memory_space=pl.ANY),
            ],
            out_specs=pl.BlockSpec((ts, d), lambda t, *_: (t, 0)),
            scratch_shapes=[pltpu.VMEM((2, rows, d), BF16), pltpu.SemaphoreType.DMA((2,))],
        ),
        out_shape=jax.ShapeDtypeStruct((n_tok, d), F32),
        compiler_params=pltpu.CompilerParams(
            dimension_semantics=("arbitrary",), vmem_limit_bytes=VMEM_LIMIT),
        name="combine",
    )(*tables, route_t, tables[0], x1, g2, b2, ys)


def kernel(x, mem, mem_ln_g, mem_ln_b, w_in, b_in, pool_w, pool_scale, pool_proj, conv_dw, conv_dw_b, conv_ln_g, conv_ln_b, conv_pw, conv_pw_b, w_kv, attn_o, w_out, b_out, ln1_g, ln1_b, router_w, router_b, exp_up, exp_up_b, exp_down, exp_down_b, ln2_g, ln2_b):
    n_b, s, d = x.shape
    depth = w_in.shape[0]
    n_e = router_w.shape[-1]
    n_tok = n_b * s
    n_t = n_tok // TOKEN_TILE
    alpha = (2.0 * depth) ** 0.25
    n_rows = TOP_K * n_tok + n_t * n_e * SEG_ALIGN + n_e * EXPERT_BLOCK
    n_rows = -(-n_rows // EXPERT_BLOCK) * EXPERT_BLOCK
    assert n_rows // EXPERT_BLOCK <= META_LANES and n_e <= LANES

    def rows3(a):
        return a.reshape(a.shape[0], 1, a.shape[1])

    in_cols = w_in.shape[-1]
    assert in_cols % PROJ_CHUNK == 0
    n_chunks = in_cols // PROJ_CHUNK
    router_pad = jnp.pad(router_w, ((0, 0), (0, 0), (0, LANES - n_e)))
    router_hi = router_pad.astype(BF16)
    weights = dict(
        w_in=w_in, b_in=b_in.reshape(depth, n_chunks, 1, PROJ_CHUNK), pool_w=pool_w,
        pool_scale=rows3(pool_scale), pool_proj=pool_proj, conv_dw=conv_dw,
        conv_dw_b=rows3(conv_dw_b), conv_ln_g=rows3(conv_ln_g), conv_ln_b=rows3(conv_ln_b),
        conv_pw=conv_pw, conv_pw_b=rows3(conv_pw_b), attn_o=attn_o,
        w_out=w_out, b_out=rows3(b_out), ln1_g=rows3(ln1_g), ln1_b=rows3(ln1_b),
        router_hi=router_hi, router_lo=(router_pad - router_hi.astype(F32)).astype(BF16),
        router_b=router_b.reshape(depth, n_e, 1))
    up_b = exp_up_b.reshape(depth, n_e, 1, exp_up_b.shape[-1])
    down_b = exp_down_b.reshape(depth, n_e, 1, d)
    g2, b2 = rows3(ln2_g), rows3(ln2_b)

    kt_all, v_all = _memory_kv(mem, mem_ln_g, mem_ln_b, w_kv)
    for layer in range(depth):
        x1, ids, route_t, seg_off, seg_len, dest, meta = _mixer(
            x, weights, kt_all, v_all, layer, alpha)
        x1 = x1.reshape(n_tok, d)
        tables = (seg_off, seg_len, dest)
        xs = _dispatch(tables, meta, ids, x1, n_rows, n_e)
        ys = _experts(meta, xs, exp_up, up_b, exp_down, down_b, layer)
        x = _combine(tables, route_t, x1, g2, b2, ys, layer, alpha, n_e).reshape(n_b, s, d)
    return x
```

```python
import functools

import jax
import jax.numpy as jnp
from jax import lax
from jax.experimental import pallas as pl
from jax.experimental.pallas import tpu as pltpu

F32 = jnp.float32
BF16 = jnp.bfloat16
I32 = jnp.int32

POOL_WINDOWS = (2, 4, 8, 16)
POOL_HALO = 16
CONV_HALO = 32
HEAD_DIM = 128
TOP_K = 4
SWIGLU_LIMIT = 7.0
SWIGLU_ALPHA = 1.702
LN_EPS = 1e-5

LANES = 128
SUBLANES = 8
SEG_ALIGN = SUBLANES
U32 = jnp.uint32
TOKEN_TILE = 256
EXPERT_BLOCK = 512
FF_CHUNK = 256
CONV_ROWS = 64
PROJ_CHUNK = 256
LOOP_PROJ_CHUNKS = 3
PERM_ROWS = 256
META_LANES = 512
META_BLK_E, META_PAD_DST, META_PAD_LEN, META_N_USED, META_REAL_ROWS = 0, 1, 2, 3, 4
VMEM_LIMIT = 56 * 1024 * 1024


def _layer_norm(x, g, b):
    mu = jnp.mean(x, axis=-1, keepdims=True)
    xc = x - mu
    var = jnp.mean(xc * xc, axis=-1, keepdims=True)
    return xc * lax.rsqrt(var + LN_EPS) * g + b


def _dot(a, b):
    return jnp.dot(a, b, preferred_element_type=F32)


def _pack_pairs(v):
    n = v.shape[1] // 2
    lo = lax.bitcast_convert_type(v[:, :n], U32)
    hi = lax.bitcast_convert_type(v[:, n:], U32)
    return (hi & jnp.uint32(0xFFFF0000)) | (lo >> 16)


def _unpack_pairs(w):
    lo = lax.bitcast_convert_type(w << 16, F32)
    hi = lax.bitcast_convert_type(w & jnp.uint32(0xFFFF0000), F32)
    return jnp.concatenate([lo, hi], axis=1).astype(BF16)


def _dot_exact(a, b):
    return jnp.dot(a, b, preferred_element_type=F32, precision=lax.Precision.HIGHEST)


def _kv_kernel(mem_ref, g_ref, b_ref, wkv_ref, kt_ref, v_ref):
    a = kt_ref.shape[2]
    mem_n = _layer_norm(mem_ref[0], g_ref[...], b_ref[...]).astype(BF16)
    kv = _dot(mem_n, wkv_ref[0].astype(BF16))
    kt_ref[0, 0] = kv[:, :a].T.astype(BF16)
    v_ref[0, 0] = kv[:, a:].astype(BF16)


def _memory_kv(mem, g, b, w_kv):
    n_b, m, d = mem.shape
    n_l, _, a2 = w_kv.shape
    a = a2 // 2
    return pl.pallas_call(
        _kv_kernel,
        grid=(n_l, n_b),
        in_specs=[
            pl.BlockSpec((1, m, d), lambda l, bb: (bb, 0, 0)),
            pl.BlockSpec((1, d), lambda l, bb: (0, 0)),
            pl.BlockSpec((1, d), lambda l, bb: (0, 0)),
            pl.BlockSpec((1, d, a2), lambda l, bb: (l, 0, 0)),
        ],
        out_specs=[
            pl.BlockSpec((1, 1, a, m), lambda l, bb: (l, bb, 0, 0)),
            pl.BlockSpec((1, 1, m, a), lambda l, bb: (l, bb, 0, 0)),
        ],
        out_shape=[
            jax.ShapeDtypeStruct((n_l, n_b, a, m), BF16),
            jax.ShapeDtypeStruct((n_l, n_b, m, a), BF16),
        ],
        compiler_params=pltpu.CompilerParams(
            dimension_semantics=("arbitrary", "arbitrary"), vmem_limit_bytes=VMEM_LIMIT),
        name="memory_kv",
    )(mem, g.reshape(1, d), b.reshape(1, d), w_kv)


def _routing_tables(padded, prefix, n_e, seg_off_ref, seg_len_ref, dest_ref, meta_ref):
    row = lax.broadcasted_iota(I32, (LANES, LANES), 0)
    col = lax.broadcasted_iota(I32, (LANES, LANES), 1)
    seg_off = _dot_exact(padded, (row < col).astype(F32))
    tot = jnp.sum(padded, axis=0, keepdims=True)
    ptot = jnp.ceil(tot * (1.0 / EXPERT_BLOCK)) * EXPERT_BLOCK
    e_end = _dot_exact(jnp.broadcast_to(ptot, (SUBLANES, LANES)),
                       (row <= col).astype(F32))[0:1]
    e_start = e_end - ptot
    seg_off_ref[...] = seg_off.astype(I32)
    seg_len_ref[...] = padded.astype(I32)
    dest_ref[...] = (prefix + e_start).astype(I32)

    e_end_col = jnp.sum(jnp.where(row == col, jnp.broadcast_to(e_end, (LANES, LANES)), 0.0),
                        axis=1, keepdims=True)
    blk_row = (lax.broadcasted_iota(I32, (LANES, META_LANES), 1) * EXPERT_BLOCK).astype(F32)
    is_real = lax.broadcasted_iota(I32, (LANES, META_LANES), 0) < n_e
    blk_e = jnp.sum(jnp.where(is_real & (e_end_col <= blk_row), 1.0, 0.0), axis=0, keepdims=True)
    blk_e = jnp.minimum(blk_e, n_e - 1.0)
    lane = lax.broadcasted_iota(I32, (1, LANES), 1)
    n_used = jnp.sum(jnp.where(lane == n_e - 1, e_end, 0.0), axis=1, keepdims=True) * (
        1.0 / EXPERT_BLOCK)

    real_end_col = jnp.sum(jnp.where(row == col, jnp.broadcast_to(e_start + tot, (LANES, LANES)),
                                     0.0), axis=1, keepdims=True)
    of_blk = lax.broadcasted_iota(I32, (LANES, META_LANES), 0).astype(F32) == blk_e
    real_end = jnp.sum(jnp.where(of_blk, real_end_col, 0.0), axis=0, keepdims=True)
    real_rows = jnp.clip(real_end - blk_row[0:1], 0.0, 1.0 * EXPERT_BLOCK)

    def wide(r):
        return jnp.concatenate([r, jnp.zeros((1, META_LANES - LANES), F32)], axis=1)

    sub = lax.broadcasted_iota(I32, (SUBLANES, META_LANES), 0)
    meta = jnp.zeros((SUBLANES, META_LANES), F32)
    for r, val in ((META_BLK_E, blk_e), (META_PAD_DST, wide(e_start + tot)),
                   (META_PAD_LEN, wide(ptot - tot)),
                   (META_N_USED, jnp.broadcast_to(n_used, (1, META_LANES))),
                   (META_REAL_ROWS, real_rows)):
        meta = jnp.where(sub == r, jnp.broadcast_to(val, (SUBLANES, META_LANES)), meta)
    meta_ref[...] = meta.astype(I32)


def _load_matrices_bf16(pieces, stage, sem):
    def fetch(p):
        src, s_idx, _, _, (r, c) = pieces[p]
        return pltpu.make_async_copy(src.at[s_idx], stage.at[p % 2, 0:r, 0:c], sem.at[p % 2])

    fetch(0).start()
    for p, (_, _, dst, d_idx, (r, c)) in enumerate(pieces):
        if p + 1 < len(pieces):
            fetch(p + 1).start()
        fetch(p).wait()
        dst[d_idx] = stage[p % 2, 0:r, 0:c].astype(BF16)


def _mixer_kernel(x_ref, w_in_hbm, b_in_ref, pool_w_hbm, pool_scale_ref, pool_proj_hbm,
                  dw_ref, dwb_ref, cg_ref, cb_ref, cpw_hbm, cpwb_ref,
                  kt_ref, v_ref, ao_hbm, wout_hbm, bout_ref, g1_ref, b1_ref,
                  wr_hi_ref, wr_lo_ref, rb_ref,
                  x1_ref, ids_ref, route_t_ref, seg_off_ref, seg_len_ref, dest_ref, meta_ref,
                  w_in_ref, pool_w_ref, pool_proj_ref, cpw_ref, ao_ref, wout_ref, stage, stage_sem,
                  h_scr, x_scr, xb_scr, pool_ext, conv_ext, conv_out,
                  cnt_scr, pre_scr, run_scr,
                  *, alpha, n_s, layer):
    s = pl.program_id(0)
    n_tiles = pl.num_programs(0) - 1
    cur = (s + 1) % 2
    nxt = s % 2
    tile = jnp.maximum(s - 1, 0)
    i = tile % n_s
    ts, d = x_ref.shape[1], x_ref.shape[2]
    n_chunks = w_in_ref.shape[0]
    pw = pool_ext.shape[1]
    cw = conv_ext.shape[1]
    aw = kt_ref.shape[2]
    n_e = rb_ref.shape[1]
    c_conv = pw
    c_q = pw + 2 * cw
    c_gate = c_q + aw

    @pl.when(i == 0)
    def _():
        pool_ext[0:POOL_HALO, :] = jnp.zeros((POOL_HALO, pw), F32)
        conv_ext[0:CONV_HALO, :] = jnp.zeros((CONV_HALO, cw), F32)

    @pl.when(s == 0)
    def _():
        run_scr[...] = jnp.zeros(run_scr.shape, F32)
        h_scr[1] = jnp.zeros(h_scr.shape[1:], F32)
        x_scr[1] = jnp.zeros(x_scr.shape[1:], F32)
        pieces = []

        def add(src, dst, rows, cols, dst_chunked=False):
            for j in range(cols // PROJ_CHUNK):
                c0 = j * PROJ_CHUNK
                d_idx = (j,) if dst_chunked else (slice(None), slice(c0, c0 + PROJ_CHUNK))
                pieces.append((src, (layer, slice(None), slice(c0, c0 + PROJ_CHUNK)),
                               dst, d_idx, (rows, PROJ_CHUNK)))

        add(w_in_hbm, w_in_ref, d, n_chunks * PROJ_CHUNK, dst_chunked=True)
        add(pool_proj_hbm, pool_proj_ref, pw, d)
        add(cpw_hbm, cpw_ref, cw, d)
        add(ao_hbm, ao_ref, aw, d)
        add(wout_hbm, wout_ref, d, d)
        gd_ = pw // len(POOL_WINDOWS)
        for g in range(len(POOL_WINDOWS)):
            pieces.append((pool_w_hbm, (layer, g), pool_w_ref, (g,), (gd_, gd_)))
        _load_matrices_bf16(pieces, stage, stage_sem)

    x_in = x_ref[0]
    x_scr[nxt] = x_in
    xb_scr[...] = x_in.astype(BF16)

    x = x_scr[cur]

    def proj(lo, hi):
        return jnp.concatenate(
            [h_scr[cur, c] for c in range(lo // PROJ_CHUNK, hi // PROJ_CHUNK)], axis=1)

    glu = proj(c_conv, c_conv + cw) * jax.nn.sigmoid(proj(c_conv + cw, c_conv + 2 * cw))
    conv_ext[CONV_HALO:CONV_HALO + ts, :] = glu
    taps = dw_ref.shape[1]
    base = CONV_HALO - (taps - 1)
    span = CONV_ROWS + CONV_HALO
    n_row_chunks = ts // CONV_ROWS
    chunks_per_iter = min(LOOP_PROJ_CHUNKS, n_chunks // n_row_chunks)
    later_chunks = list(range(n_row_chunks * chunks_per_iter, n_chunks))

    def project(c):
        h_scr[nxt, c] = _dot(xb_scr[...], w_in_ref[c]) + b_in_ref[0, c]

    def project_some(n):
        for _ in range(min(n, len(later_chunks))):
            project(later_chunks.pop(0))

    def conv_rows_and_projection(j, carry):
        r0 = pl.multiple_of(j * CONV_ROWS, CONV_ROWS)
        for c0 in range(0, cw, LANES):
            window = conv_ext[pl.ds(r0, span), c0:c0 + LANES]
            acc = jnp.broadcast_to(dwb_ref[0, :, c0:c0 + LANES], (CONV_ROWS, LANES))
            for res in range(SUBLANES):
                offs = [o for o in range(base, base + taps) if o % SUBLANES == res]
                if not offs:
                    continue
                shifted = window if res == 0 else pltpu.roll(window, span - res, axis=0)
                for o in offs:
                    k = o - base
                    acc = acc + dw_ref[0, k:k + 1, c0:c0 + LANES] * shifted[
                        o - res:o - res + CONV_ROWS]
            conv_out[pl.ds(r0, CONV_ROWS), c0:c0 + LANES] = acc
        for m in range(chunks_per_iter):
            project(j * chunks_per_iter + m)
        return carry

    lax.fori_loop(0, n_row_chunks, conv_rows_and_projection, 0)
    conv_ext[0:CONV_HALO, :] = conv_ext[ts:ts + CONV_HALO, :]
    hc = _layer_norm(conv_out[...], cg_ref[0], cb_ref[0])
    hc = hc * jax.nn.sigmoid(hc)
    project_some(2)
    y_conv = _dot(hc.astype(BF16), cpw_ref[...]) + cpwb_ref[0]

    u = proj(0, pw)
    pool_ext[POOL_HALO:POOL_HALO + ts, :] = u
    t_glob = i * ts + lax.broadcasted_iota(I32, (ts, 1), 0)
    gd = pw // len(POOL_WINDOWS)
    pooled = []
    for g, w in enumerate(POOL_WINDOWS):
        lo = g * gd
        ug = u[:, lo:lo + gd]
        run = pool_ext[:, lo:lo + gd]
        step = 1
        while step < w:
            run = run + pltpu.roll(run, step, axis=0)
            step *= 2
        cnt = jnp.minimum(t_glob + 1, w).astype(F32)
        pooled.append((run[POOL_HALO:POOL_HALO + ts] / cnt - ug).astype(BF16))
    project_some(2)
    mixed = [_dot(p, pool_w_ref[g]) for g, p in enumerate(pooled)]
    mixed = jnp.concatenate(mixed, axis=1) * pool_scale_ref[0]
    y_pool = _dot(mixed.astype(BF16), pool_proj_ref[...])
    pool_ext[0:POOL_HALO, :] = pool_ext[ts:ts + POOL_HALO, :]

    q = proj(c_q, c_q + aw)
    heads = []
    for h in range(aw // HEAD_DIM):
        lo = h * HEAD_DIM
        sc = _dot(q[:, lo:lo + HEAD_DIM].astype(BF16), kt_ref[0, 0, lo:lo + HEAD_DIM, :])
        sc = sc * (HEAD_DIM ** -0.5)
        p = jnp.exp(sc - jnp.max(sc, axis=-1, keepdims=True))
        o = _dot(p.astype(BF16), v_ref[0, 0, :, lo:lo + HEAD_DIM])
        heads.append(o / jnp.sum(p, axis=-1, keepdims=True))
    y_attn = _dot(jnp.concatenate(heads, axis=1).astype(BF16), ao_ref[...])

    merged = jax.nn.sigmoid(proj(c_gate, c_gate + d)) * y_pool
    merged = merged + jax.nn.sigmoid(proj(c_gate + d, c_gate + 2 * d)) * y_conv
    merged = merged + jax.nn.sigmoid(proj(c_gate + 2 * d, c_gate + 3 * d)) * y_attn
    project_some(2)
    out = _dot(merged.astype(BF16), wout_ref[...]) + bout_ref[0]
    x1 = _layer_norm(alpha * x + out, g1_ref[0], b1_ref[0])
    x1_ref[0] = x1

    x1_hi = x1.astype(BF16)
    x1_lo = (x1 - x1_hi.astype(F32)).astype(BF16)
    logits = _dot(x1_hi, wr_hi_ref[0]) + _dot(x1_lo, wr_hi_ref[0]) + _dot(x1_hi, wr_lo_ref[0])
    logits = logits.T[0:n_e, :] + rb_ref[0]
    e_iota = lax.broadcasted_iota(I32, (n_e, ts), 0)
    rank = jnp.zeros((n_e, ts), F32)
    for other in range(n_e):
        row = logits[other:other + 1, :]
        beats = (row > logits) | ((row == logits) & (e_iota > other))
        rank = rank + jnp.where(beats, 1.0, 0.0)
    hot = jnp.where(rank < TOP_K, 1.0, 0.0)
    top_v, top_i = [], []
    for k in range(TOP_K):
        sel = rank == k
        top_i.append(jnp.sum(jnp.where(sel, e_iota, 0), axis=0, keepdims=True))
        top_v.append(jnp.sum(jnp.where(sel, logits, 0.0), axis=0, keepdims=True))
    ex = [jnp.exp(v - top_v[0]) for v in top_v]
    den = ex[0] + ex[1] + ex[2] + ex[3]
    sub = lax.broadcasted_iota(I32, (SUBLANES, ts), 0)
    route = jnp.zeros((SUBLANES, ts), F32)
    for k in range(TOP_K):
        ids_ref[k:k + 1, :] = top_i[k]
        route = jnp.where(sub == k, top_i[k].astype(F32), route)
        route = jnp.where(sub == TOP_K + k, ex[k] / den, route)
    route_t_ref[...] = jnp.concatenate(
        [route, jnp.zeros((LANES - SUBLANES, ts), F32)], axis=0).T

    hot_wide = jnp.concatenate([hot, jnp.zeros((LANES - n_e, ts), F32)], axis=0).astype(BF16)
    cnt = lax.dot_general(jnp.ones((SUBLANES, ts), BF16), hot_wide, (((1,), (1,)), ((), ())),
                          preferred_element_type=F32)[0:1]
    is_expert = lax.broadcasted_iota(I32, (1, LANES), 1) < n_e
    padded = jnp.where(is_expert, jnp.maximum(jnp.ceil(cnt * (1.0 / SEG_ALIGN)), 1.0), 0.0)
    padded = padded * SEG_ALIGN
    padded = jnp.where(s >= 1, padded, 0.0)
    cnt_scr[pl.ds(tile, 1), :] = padded
    pre_scr[pl.ds(tile, 1), :] = run_scr[0:1, :]
    run_scr[0:1, :] = run_scr[0:1, :] + padded

    project_some(len(later_chunks))

    @pl.when(s == n_tiles)
    def _():
        _routing_tables(cnt_scr[...], pre_scr[...], n_e,
                        seg_off_ref, seg_len_ref, dest_ref, meta_ref)


def _mixer(x, weights, kt, v, layer, alpha):
    n_b, s, d = x.shape
    ts = TOKEN_TILE
    n_s = s // ts
    n_t = n_b * n_s
    names = ("w_in", "b_in", "pool_w", "pool_scale", "pool_proj", "conv_dw", "conv_dw_b",
             "conv_ln_g", "conv_ln_b", "conv_pw", "conv_pw_b")
    names2 = ("attn_o", "w_out", "b_out", "ln1_g", "ln1_b", "router_hi", "router_lo", "router_b")
    in_hbm = ("w_in", "pool_w", "pool_proj", "conv_pw", "attn_o", "w_out")
    pw = weights["pool_proj"].shape[1]
    cw = weights["conv_pw"].shape[1]
    aw = weights["attn_o"].shape[1]
    n_chunks = weights["w_in"].shape[2] // PROJ_CHUNK
    gd = pw // len(POOL_WINDOWS)

    def of_layer(arr):
        tail = (0,) * (arr.ndim - 1)
        return pl.BlockSpec((1,) + arr.shape[1:], lambda st: (layer,) + tail,
                            pipeline_mode=pl.Buffered(1))

    def spec(name):
        if name in in_hbm:
            return pl.BlockSpec(memory_space=pl.ANY)
        return of_layer(weights[name])

    def resident(shape):
        return pl.BlockSpec(shape, lambda st: (0, 0))

    def tile_a(st):
        return jnp.minimum(st, n_t - 1)

    def tile_b(st):
        return jnp.maximum(st - 1, 0)

    in_specs = ([pl.BlockSpec((1, ts, d), lambda st: (tile_a(st) // n_s, tile_a(st) % n_s, 0))]
                + [spec(k) for k in names]
                + [pl.BlockSpec((1, 1) + kt.shape[2:],
                                lambda st: (layer, tile_b(st) // n_s, 0, 0)),
                   pl.BlockSpec((1, 1) + v.shape[2:],
                                lambda st: (layer, tile_b(st) // n_s, 0, 0))]
                + [spec(k) for k in names2])
    out_specs = [
        pl.BlockSpec((1, ts, d), lambda st: (tile_b(st) // n_s, tile_b(st) % n_s, 0)),
        pl.BlockSpec((TOP_K, ts), lambda st: (0, tile_b(st))),
        pl.BlockSpec((ts, LANES), lambda st: (tile_b(st), 0)),
        resident((n_t, LANES)), resident((n_t, LANES)), resident((n_t, LANES)),
        resident((SUBLANES, META_LANES)),
    ]
    out_shape = [
        jax.ShapeDtypeStruct((n_b, s, d), F32),
        jax.ShapeDtypeStruct((TOP_K, n_b * s), I32),
        jax.ShapeDtypeStruct((n_b * s, LANES), F32),
        jax.ShapeDtypeStruct((n_t, LANES), I32),
        jax.ShapeDtypeStruct((n_t, LANES), I32),
        jax.ShapeDtypeStruct((n_t, LANES), I32),
        jax.ShapeDtypeStruct((SUBLANES, META_LANES), I32),
    ]
    return pl.pallas_call(
        functools.partial(_mixer_kernel, alpha=alpha, n_s=n_s, layer=layer),
        grid=(n_t + 1,),
        in_specs=in_specs,
        out_specs=out_specs,
        out_shape=out_shape,
        scratch_shapes=[
            pltpu.VMEM((n_chunks, d, PROJ_CHUNK), BF16),
            pltpu.VMEM((len(POOL_WINDOWS), gd, gd), BF16),
            pltpu.VMEM((pw, d), BF16),
            pltpu.VMEM((cw, d), BF16),
            pltpu.VMEM((aw, d), BF16),
            pltpu.VMEM((d, d), BF16),
            pltpu.VMEM((2, max(d, pw, cw, aw), PROJ_CHUNK), F32),
            pltpu.SemaphoreType.DMA((2,)),
            pltpu.VMEM((2, n_chunks, ts, PROJ_CHUNK), F32),
            pltpu.VMEM((2, ts, d), F32),
            pltpu.VMEM((ts, d), BF16),
            pltpu.VMEM((ts + POOL_HALO, pw), F32),
            pltpu.VMEM((ts + CONV_HALO, cw), F32),
            pltpu.VMEM((ts, cw), F32),
            pltpu.VMEM((n_t, LANES), F32),
            pltpu.VMEM((n_t, LANES), F32),
            pltpu.VMEM((SUBLANES, LANES), F32),
        ],
        compiler_params=pltpu.CompilerParams(
            dimension_semantics=("arbitrary",), vmem_limit_bytes=VMEM_LIMIT),
        name="mixer",
    )(x, *[weights[k] for k in names], kt, v, *[weights[k] for k in names2])


def _start_segment_copies(seg_off_ref, seg_len_ref, dest_ref, tile, n_e, make_copy):
    for e in range(n_e):
        make_copy(pl.multiple_of(seg_off_ref[tile, e], SEG_ALIGN),
                  pl.multiple_of(dest_ref[tile, e], SEG_ALIGN),
                  pl.multiple_of(seg_len_ref[tile, e], SEG_ALIGN)).start()


def _wait_segment_copies(seg_off_ref, seg_len_ref, tile, n_e, make_copy):
    total = seg_off_ref[tile, n_e - 1] + seg_len_ref[tile, n_e - 1]
    make_copy(0, 0, pl.multiple_of(total, SEG_ALIGN)).wait()


def _dispatch_kernel(seg_off_ref, seg_len_ref, dest_ref, meta_ref,
                     ids_ref, off_ref, x1_ref, xs_ref, buf, zbuf, sem, zsem, *, n_e):
    tile = pl.program_id(0)
    n_tiles = pl.num_programs(0)
    slot = tile % 2
    ts = x1_ref.shape[0]
    rows = buf.shape[1]

    @pl.when(tile == 0)
    def _():
        zbuf[...] = jnp.zeros(zbuf.shape, U32)

        def zero_copy(e):
            n = pl.multiple_of(meta_ref[META_PAD_LEN, e], SEG_ALIGN)
            dst = pl.multiple_of(meta_ref[META_PAD_DST, e], SEG_ALIGN)
            return n, pltpu.make_async_copy(zbuf.at[pl.ds(0, n)], xs_ref.at[pl.ds(dst, n)], zsem)

        def zstart(e, c):
            n, cp = zero_copy(e)
            pl.when(n > 0)(cp.start)
            return c

        def zwait(e, c):
            n, cp = zero_copy(e)
            pl.when(n > 0)(cp.wait)
            return c

        lax.fori_loop(0, n_e, zstart, 0)
        lax.fori_loop(0, n_e, zwait, 0)

    def copies_from(s):
        def make_copy(off, dst, n):
            return pltpu.make_async_copy(buf.at[s, pl.ds(off, n)], xs_ref.at[pl.ds(dst, n)],
                                         sem.at[s])
        return make_copy

    @pl.when(tile >= 2)
    def _():
        _wait_segment_copies(seg_off_ref, seg_len_ref, tile - 2, n_e, copies_from(slot))

    ids = ids_ref[...]
    e_iota = lax.broadcasted_iota(I32, (n_e, ts), 0)
    hots = [(e_iota == ids[k:k + 1, :]).astype(F32) for k in range(TOP_K)]
    hot = hots[0] + hots[1] + hots[2] + hots[3]
    off_row = off_ref[pl.ds(tile, 1), :].astype(F32)
    eye = (lax.broadcasted_iota(I32, (n_e, LANES), 0)
           == lax.broadcasted_iota(I32, (n_e, LANES), 1))
    off_col = jnp.sum(jnp.where(eye, off_row, 0.0), axis=1, keepdims=True)
    upper = (lax.broadcasted_iota(I32, (ts, ts), 0)
             < lax.broadcasted_iota(I32, (ts, ts), 1)).astype(BF16)
    place = _dot(hot.astype(BF16), upper) + off_col
    pos = [jnp.sum(hots[k] * place, axis=0, keepdims=True).astype(I32)
           for k in range(TOP_K)]
    x1b = x1_ref[...].astype(BF16)
    for r0 in range(0, rows, PERM_ROWS):
        r_iota = r0 + lax.broadcasted_iota(I32, (PERM_ROWS, ts), 0)
        hit = (r_iota == pos[0]) | (r_iota == pos[1]) | (r_iota == pos[2]) | (r_iota == pos[3])
        perm = jnp.where(hit, 1.0, 0.0).astype(BF16)
        buf[slot, r0:r0 + PERM_ROWS] = _pack_pairs(_dot(perm, x1b))
    _start_segment_copies(seg_off_ref, seg_len_ref, dest_ref, tile, n_e, copies_from(slot))

    @pl.when(tile == n_tiles - 1)
    def _():
        @pl.when(tile >= 1)
        def _():
            _wait_segment_copies(seg_off_ref, seg_len_ref, tile - 1, n_e, copies_from(1 - slot))

        _wait_segment_copies(seg_off_ref, seg_len_ref, tile, n_e, copies_from(slot))


def _tile_rows(ts, n_e):
    return TOP_K * ts + n_e * SEG_ALIGN


def _dispatch(tables, meta, ids, x1, n_rows, n_e):
    n_tok, d = x1.shape
    ts = TOKEN_TILE
    n_t = n_tok // ts
    rows = _tile_rows(ts, n_e)
    return pl.pallas_call(
        functools.partial(_dispatch_kernel, n_e=n_e),
        grid_spec=pltpu.PrefetchScalarGridSpec(
            num_scalar_prefetch=4,
            grid=(n_t,),
            in_specs=[
                pl.BlockSpec((TOP_K, ts), lambda t, *_: (0, t)),
                pl.BlockSpec((n_t, LANES), lambda t, *_: (0, 0)),
                pl.BlockSpec((ts, d), lambda t, *_: (t, 0)),
            ],
            out_specs=pl.BlockSpec(memory_space=pl.ANY),
            scratch_shapes=[pltpu.VMEM((2, rows, d // 2), U32),
                            pltpu.VMEM((EXPERT_BLOCK, d // 2), U32),
                            pltpu.SemaphoreType.DMA((2,)), pltpu.SemaphoreType.DMA(())],
        ),
        out_shape=jax.ShapeDtypeStruct((n_rows, d // 2), U32),
        compiler_params=pltpu.CompilerParams(
            dimension_semantics=("arbitrary",), vmem_limit_bytes=VMEM_LIMIT),
        name="dispatch",
    )(*tables, meta, ids, tables[0], x1)


def _expert_kernel(meta_ref, xs_ref, wup_ref, bup_ref, wdn_ref, bdn_ref,
                   ys_ref, wup_bf, wdn_bf):
    j = pl.program_id(0)
    f = wdn_ref.shape[2]

    @pl.when(j < meta_ref[META_N_USED, 0])
    def _():
        prev = meta_ref[META_BLK_E, jnp.maximum(j - 1, 0)]

        @pl.when((j == 0) | (meta_ref[META_BLK_E, j] != prev))
        def _():
            wup_bf[...] = wup_ref[0, 0].astype(BF16)
            wdn_bf[...] = wdn_ref[0, 0].astype(BF16)

        def expert_rows(n_rows):
            x = _unpack_pairs(xs_ref[0:n_rows, :])

            def up(c0):
                glu = _dot(x, wup_bf[:, c0:c0 + FF_CHUNK]) + bup_ref[0, 0, :, c0:c0 + FF_CHUNK]
                lin = (_dot(x, wup_bf[:, f + c0:f + c0 + FF_CHUNK])
                       + bup_ref[0, 0, :, f + c0:f + c0 + FF_CHUNK])
                return glu, lin

            y = jnp.broadcast_to(bdn_ref[0, 0], (n_rows, wdn_ref.shape[3]))
            nxt = up(0)
            for c0 in range(0, f, FF_CHUNK):
                glu, lin = nxt
                if c0 + FF_CHUNK < f:
                    nxt = up(c0 + FF_CHUNK)
                glu = jnp.minimum(glu, SWIGLU_LIMIT)
                lin = jnp.clip(lin, -SWIGLU_LIMIT, SWIGLU_LIMIT)
                act = glu * jax.nn.sigmoid(SWIGLU_ALPHA * glu) * (lin + 1.0)
                y = y + _dot(act.astype(BF16), wdn_bf[c0:c0 + FF_CHUNK, :])
            ys_ref[0:n_rows, :] = _pack_pairs(y.astype(BF16).astype(F32))

        bm = xs_ref.shape[0]
        real_rows = meta_ref[META_REAL_ROWS, j]
        pl.when(real_rows > bm // 2)(lambda: expert_rows(bm))
        pl.when(real_rows <= bm // 2)(lambda: expert_rows(bm // 2))


def _experts(meta, xs, w_up, b_up, w_down, b_down, layer):
    n_rows, half_d = xs.shape
    d = 2 * half_d
    f2 = w_up.shape[-1]
    f = f2 // 2
    bm = EXPERT_BLOCK
    n_blk = n_rows // bm

    def last_used(j, meta):
        return jnp.maximum(jnp.minimum(j, meta[META_N_USED, 0] - 1), 0)

    def row_blk(j, meta):
        return (last_used(j, meta), 0)

    def w_blk(j, meta):
        return (layer, meta[META_BLK_E, last_used(j, meta)], 0, 0)

    return pl.pallas_call(
        _expert_kernel,
        grid_spec=pltpu.PrefetchScalarGridSpec(
            num_scalar_prefetch=1,
            grid=(n_blk,),
            in_specs=[
                pl.BlockSpec((bm, half_d), row_blk),
                pl.BlockSpec((1, 1, d, f2), w_blk),
                pl.BlockSpec((1, 1, 1, f2), w_blk),
                pl.BlockSpec((1, 1, f, d), w_blk),
                pl.BlockSpec((1, 1, 1, d), w_blk),
            ],
            out_specs=pl.BlockSpec((bm, half_d), row_blk),
            scratch_shapes=[pltpu.VMEM((d, f2), BF16), pltpu.VMEM((f, d), BF16)],
        ),
        out_shape=jax.ShapeDtypeStruct((n_rows, half_d), U32),
        compiler_params=pltpu.CompilerParams(
            dimension_semantics=("arbitrary",), vmem_limit_bytes=VMEM_LIMIT),
        name="experts",
    )(meta, xs, w_up, b_up, w_down, b_down)


def _combine_kernel(seg_off_ref, seg_len_ref, dest_ref,
                    route_ref, off_ref, x1_ref, g2_ref, b2_ref, ys_ref,
                    out_ref, buf, sem, *, alpha, n_e):
    tile = pl.program_id(0)
    n_tiles = pl.num_programs(0)
    slot = tile % 2
    ts = x1_ref.shape[0]
    rows = buf.shape[1]

    def copies_into(s):
        def make_copy(off, dst, n):
            return pltpu.make_async_copy(ys_ref.at[pl.ds(dst, n)], buf.at[s, pl.ds(off, n)],
                                         sem.at[s])
        return make_copy

    @pl.when(tile == 0)
    def _():
        buf[...] = jnp.zeros(buf.shape, U32)
        _start_segment_copies(seg_off_ref, seg_len_ref, dest_ref, tile, n_e, copies_into(slot))

    nxt = jnp.minimum(tile + 1, n_tiles - 1)
    _start_segment_copies(seg_off_ref, seg_len_ref, dest_ref, nxt, n_e, copies_into(1 - slot))

    route = route_ref[...]
    l_iota = lax.broadcasted_iota(I32, (ts, LANES), 1)
    hots = [(l_iota == route[:, k:k + 1].astype(I32)).astype(F32) for k in range(TOP_K)]
    hot = hots[0] + hots[1] + hots[2] + hots[3]
    lower = (lax.broadcasted_iota(I32, (ts, ts), 1)
             < lax.broadcasted_iota(I32, (ts, ts), 0)).astype(BF16)
    place = _dot(lower, hot.astype(BF16)) + off_ref[pl.ds(tile, 1), :].astype(F32)
    pos = [jnp.sum(hots[k] * place, axis=1, keepdims=True).astype(I32)
           for k in range(TOP_K)]

    _wait_segment_copies(seg_off_ref, seg_len_ref, tile, n_e, copies_into(slot))
    y = jnp.zeros(out_ref.shape, F32)
    for r0 in range(0, rows, PERM_ROWS):
        r_iota = r0 + lax.broadcasted_iota(I32, (ts, PERM_ROWS), 1)
        weight = jnp.zeros((ts, PERM_ROWS), F32)
        for k in range(TOP_K):
            weight = jnp.where(r_iota == pos[k], route[:, TOP_K + k:TOP_K + k + 1], weight)
        y = y + _dot(weight.astype(BF16), _unpack_pairs(buf[slot, r0:r0 + PERM_ROWS]))
    out_ref[...] = _layer_norm(alpha * x1_ref[...] + y, g2_ref[0], b2_ref[0])

    @pl.when(tile == n_tiles - 1)
    def _():
        _wait_segment_copies(seg_off_ref, seg_len_ref, tile, n_e, copies_into(1 - slot))


def _combine(tables, route_t, x1, g2, b2, ys, layer, alpha, n_e):
    n_tok, d = x1.shape
    ts = TOKEN_TILE
    n_t = n_tok // ts
    rows = _tile_rows(ts, n_e)
    return pl.pallas_call(
        functools.partial(_combine_kernel, alpha=alpha, n_e=n_e),
        grid_spec=pltpu.PrefetchScalarGridSpec(
            num_scalar_prefetch=3,
            grid=(n_t,),
            in_specs=[
                pl.BlockSpec((ts, LANES), lambda t, *_: (t, 0)),
                pl.BlockSpec((n_t, LANES), lambda t, *_: (0, 0)),
                pl.BlockSpec((ts, d), lambda t, *_: (t, 0)),
                pl.BlockSpec((1, 1, d), lambda t, *_: (layer, 0, 0)),
                pl.BlockSpec((1, 1, d), lambda t, *_: (layer, 0, 0)),
                pl.BlockSpec(memory_space=pl.ANY),
            ],
            out_specs=pl.BlockSpec((ts, d), lambda t, *_: (t, 0)),
            scratch_shapes=[pltpu.VMEM((2, rows, d // 2), U32), pltpu.SemaphoreType.DMA((2,))],
        ),
        out_shape=jax.ShapeDtypeStruct((n_tok, d), F32),
        compiler_params=pltpu.CompilerParams(
            dimension_semantics=("arbitrary",), vmem_limit_bytes=VMEM_LIMIT),
        name="combine",
    )(*tables, route_t, tables[0], x1, g2, b2, ys)


def kernel(x, mem, mem_ln_g, mem_ln_b, w_in, b_in, pool_w, pool_scale, pool_proj, conv_dw, conv_dw_b, conv_ln_g, conv_ln_b, conv_pw, conv_pw_b, w_kv, attn_o, w_out, b_out, ln1_g, ln1_b, router_w, router_b, exp_up, exp_up_b, exp_down, exp_down_b, ln2_g, ln2_b):
    n_b, s, d = x.shape
    depth = w_in.shape[0]
    n_e = router_w.shape[-1]
    n_tok = n_b * s
    n_t = n_tok // TOKEN_TILE
    alpha = (2.0 * depth) ** 0.25
    n_rows = TOP_K * n_tok + n_t * n_e * SEG_ALIGN + n_e * EXPERT_BLOCK
    n_rows = -(-n_rows // EXPERT_BLOCK) * EXPERT_BLOCK
    assert n_rows // EXPERT_BLOCK <= META_LANES and n_e <= LANES

    def rows3(a):
        return a.reshape(a.shape[0], 1, a.shape[1])

    in_cols = w_in.shape[-1]
    assert in_cols % PROJ_CHUNK == 0
    n_chunks = in_cols // PROJ_CHUNK
    router_pad = jnp.pad(router_w, ((0, 0), (0, 0), (0, LANES - n_e)))
    router_hi = router_pad.astype(BF16)
    weights = dict(
        w_in=w_in, b_in=b_in.reshape(depth, n_chunks, 1, PROJ_CHUNK), pool_w=pool_w,
        pool_scale=rows3(pool_scale), pool_proj=pool_proj, conv_dw=conv_dw,
        conv_dw_b=rows3(conv_dw_b), conv_ln_g=rows3(conv_ln_g), conv_ln_b=rows3(conv_ln_b),
        conv_pw=conv_pw, conv_pw_b=rows3(conv_pw_b), attn_o=attn_o,
        w_out=w_out, b_out=rows3(b_out), ln1_g=rows3(ln1_g), ln1_b=rows3(ln1_b),
        router_hi=router_hi, router_lo=(router_pad - router_hi.astype(F32)).astype(BF16),
        router_b=router_b.reshape(depth, n_e, 1))
    up_b = exp_up_b.reshape(depth, n_e, 1, exp_up_b.shape[-1])
    down_b = exp_down_b.reshape(depth, n_e, 1, d)
    g2, b2 = rows3(ln2_g), rows3(ln2_b)

    kt_all, v_all = _memory_kv(mem, mem_ln_g, mem_ln_b, w_kv)
    for layer in range(depth):
        x1, ids, route_t, seg_off, seg_len, dest, meta = _mixer(
            x, weights, kt_all, v_all, layer, alpha)
        x1 = x1.reshape(n_tok, d)
        tables = (seg_off, seg_len, dest)
        xs = _dispatch(tables, meta, ids, x1, n_rows, n_e)
        ys = _experts(meta, xs, exp_up, up_b, exp_down, down_b, layer)
        x = _combine(tables, route_t, x1, g2, b2, ys, layer, alpha, n_e).reshape(n_b, s, d)
    return x
```

```python
import functools

import jax
import jax.numpy as jnp
from jax import lax
from jax.experimental import pallas as pl
from jax.experimental.pallas import tpu as pltpu

F32 = jnp.float32
BF16 = jnp.bfloat16
I32 = jnp.int32

POOL_WINDOWS = (2, 4, 8, 16)
POOL_HALO = 16
CONV_HALO = 32
HEAD_DIM = 128
TOP_K = 4
SWIGLU_LIMIT = 7.0
SWIGLU_ALPHA = 1.702
LN_EPS = 1e-5

LANES = 128
SUBLANES = 8
SEG_ALIGN = SUBLANES
U32 = jnp.uint32
TOKEN_TILE = 256
EXPERT_BLOCK = 512
FF_CHUNK = 256
CONV_ROWS = 64
PROJ_CHUNK = 256
LOOP_PROJ_CHUNKS = 3
PERM_ROWS = 256
META_LANES = 512
META_BLK_E, META_PAD_DST, META_PAD_LEN, META_N_USED, META_REAL_ROWS = 0, 1, 2, 3, 4
VMEM_LIMIT = 56 * 1024 * 1024


def _layer_norm(x, g, b):
    mu = jnp.mean(x, axis=-1, keepdims=True)
    xc = x - mu
    var = jnp.mean(xc * xc, axis=-1, keepdims=True)
    return xc * lax.rsqrt(var + LN_EPS) * g + b


def _dot(a, b):
    return jnp.dot(a, b, preferred_element_type=F32)


def _pack_pairs(v):
    n = v.shape[1] // 2
    lo = lax.bitcast_convert_type(v[:, :n], U32)
    hi = lax.bitcast_convert_type(v[:, n:], U32)
    return (hi & jnp.uint32(0xFFFF0000)) | (lo >> 16)


def _unpack_pairs(w):
    lo = lax.bitcast_convert_type(w << 16, F32)
    hi = lax.bitcast_convert_type(w & jnp.uint32(0xFFFF0000), F32)
    return jnp.concatenate([lo, hi], axis=1).astype(BF16)


def _dot_exact(a, b):
    return jnp.dot(a, b, preferred_element_type=F32, precision=lax.Precision.HIGHEST)


def _kv_kernel(mem_ref, g_ref, b_ref, wkv_ref, kt_ref, v_ref):
    a = kt_ref.shape[2]
    mem_n = _layer_norm(mem_ref[0], g_ref[...], b_ref[...]).astype(BF16)
    kv = _dot(mem_n, wkv_ref[0].astype(BF16))
    kt_ref[0, 0] = kv[:, :a].T.astype(BF16)
    v_ref[0, 0] = kv[:, a:].astype(BF16)


def _memory_kv(mem, g, b, w_kv):
    n_b, m, d = mem.shape
    n_l, _, a2 = w_kv.shape
    a = a2 // 2
    return pl.pallas_call(
        _kv_kernel,
        grid=(n_l, n_b),
        in_specs=[
            pl.BlockSpec((1, m, d), lambda l, bb: (bb, 0, 0)),
            pl.BlockSpec((1, d), lambda l, bb: (0, 0)),
            pl.BlockSpec((1, d), lambda l, bb: (0, 0)),
            pl.BlockSpec((1, d, a2), lambda l, bb: (l, 0, 0)),
        ],
        out_specs=[
            pl.BlockSpec((1, 1, a, m), lambda l, bb: (l, bb, 0, 0)),
            pl.BlockSpec((1, 1, m, a), lambda l, bb: (l, bb, 0, 0)),
        ],
        out_shape=[
            jax.ShapeDtypeStruct((n_l, n_b, a, m), BF16),
            jax.ShapeDtypeStruct((n_l, n_b, m, a), BF16),
        ],
        compiler_params=pltpu.CompilerParams(
            dimension_semantics=("arbitrary", "arbitrary"), vmem_limit_bytes=VMEM_LIMIT),
        name="memory_kv",
    )(mem, g.reshape(1, d), b.reshape(1, d), w_kv)


def _routing_tables(padded, prefix, n_e, seg_off_ref, seg_len_ref, dest_ref, meta_ref):
    row = lax.broadcasted_iota(I32, (LANES, LANES), 0)
    col = lax.broadcasted_iota(I32, (LANES, LANES), 1)
    seg_off = _dot_exact(padded, (row < col).astype(F32))
    tot = jnp.sum(padded, axis=0, keepdims=True)
    ptot = jnp.ceil(tot * (1.0 / EXPERT_BLOCK)) * EXPERT_BLOCK
    e_end = _dot_exact(jnp.broadcast_to(ptot, (SUBLANES, LANES)),
                       (row <= col).astype(F32))[0:1]
    e_start = e_end - ptot
    seg_off_ref[...] = seg_off.astype(I32)
    seg_len_ref[...] = padded.astype(I32)
    dest_ref[...] = (prefix + e_start).astype(I32)

    e_end_col = jnp.sum(jnp.where(row == col, jnp.broadcast_to(e_end, (LANES, LANES)), 0.0),
                        axis=1, keepdims=True)
    blk_row = (lax.broadcasted_iota(I32, (LANES, META_LANES), 1) * EXPERT_BLOCK).astype(F32)
    is_real = lax.broadcasted_iota(I32, (LANES, META_LANES), 0) < n_e
    blk_e = jnp.sum(jnp.where(is_real & (e_end_col <= blk_row), 1.0, 0.0), axis=0, keepdims=True)
    blk_e = jnp.minimum(blk_e, n_e - 1.0)
    lane = lax.broadcasted_iota(I32, (1, LANES), 1)
    n_used = jnp.sum(jnp.where(lane == n_e - 1, e_end, 0.0), axis=1, keepdims=True) * (
        1.0 / EXPERT_BLOCK)

    real_end_col = jnp.sum(jnp.where(row == col, jnp.broadcast_to(e_start + tot, (LANES, LANES)),
                                     0.0), axis=1, keepdims=True)
    of_blk = lax.broadcasted_iota(I32, (LANES, META_LANES), 0).astype(F32) == blk_e
    real_end = jnp.sum(jnp.where(of_blk, real_end_col, 0.0), axis=0, keepdims=True)
    real_rows = jnp.clip(real_end - blk_row[0:1], 0.0, 1.0 * EXPERT_BLOCK)

    def wide(r):
        return jnp.concatenate([r, jnp.zeros((1, META_LANES - LANES), F32)], axis=1)

    sub = lax.broadcasted_iota(I32, (SUBLANES, META_LANES), 0)
    meta = jnp.zeros((SUBLANES, META_LANES), F32)
    for r, val in ((META_BLK_E, blk_e), (META_PAD_DST, wide(e_start + tot)),
                   (META_PAD_LEN, wide(ptot - tot)),
                   (META_N_USED, jnp.broadcast_to(n_used, (1, META_LANES))),
                   (META_REAL_ROWS, real_rows)):
        meta = jnp.where(sub == r, jnp.broadcast_to(val, (SUBLANES, META_LANES)), meta)
    meta_ref[...] = meta.astype(I32)


def _load_matrices_bf16(pieces, stage, sem):
    def fetch(p):
        src, s_idx, _, _, (r, c) = pieces[p]
        return pltpu.make_async_copy(src.at[s_idx], stage.at[p % 2, 0:r, 0:c], sem.at[p % 2])

    fetch(0).start()
    for p, (_, _, dst, d_idx, (r, c)) in enumerate(pieces):
        if p + 1 < len(pieces):
            fetch(p + 1).start()
        fetch(p).wait()
        dst[d_idx] = stage[p % 2, 0:r, 0:c].astype(BF16)


def _mixer_kernel(x_ref, w_in_hbm, b_in_ref, pool_w_hbm, pool_scale_ref, pool_proj_hbm,
                  dw_ref, dwb_ref, cg_ref, cb_ref, cpw_hbm, cpwb_ref,
                  kt_ref, v_ref, ao_hbm, wout_hbm, bout_ref, g1_ref, b1_ref,
                  wr_hi_ref, wr_lo_ref, rb_ref,
                  x1_ref, ids_ref, route_t_ref, seg_off_ref, seg_len_ref, dest_ref, meta_ref,
                  w_in_ref, pool_w_ref, pool_proj_ref, cpw_ref, ao_ref, wout_ref, stage, stage_sem,
                  h_scr, x_scr, xb_scr, pool_ext, conv_ext, conv_out,
                  cnt_scr, pre_scr, run_scr,
                  *, alpha, n_s, layer):
    s = pl.program_id(0)
    n_tiles = pl.num_programs(0) - 1
    cur = (s + 1) % 2
    nxt = s % 2
    tile = jnp.maximum(s - 1, 0)
    i = tile % n_s
    ts, d = x_ref.shape[1], x_ref.shape[2]
    n_chunks = w_in_ref.shape[0]
    pw = pool_ext.shape[1]
    cw = conv_ext.shape[1]
    aw = kt_ref.shape[2]
    n_e = rb_ref.shape[1]
    c_conv = pw
    c_q = pw + 2 * cw
    c_gate = c_q + aw

    @pl.when(i == 0)
    def _():
        pool_ext[0:POOL_HALO, :] = jnp.zeros((POOL_HALO, pw), F32)
        conv_ext[0:CONV_HALO, :] = jnp.zeros((CONV_HALO, cw), F32)

    @pl.when(s == 0)
    def _():
        run_scr[...] = jnp.zeros(run_scr.shape, F32)
        h_scr[1] = jnp.zeros(h_scr.shape[1:], F32)
        x_scr[1] = jnp.zeros(x_scr.shape[1:], F32)
        pieces = []

        def add(src, dst, rows, cols, dst_chunked=False):
            for j in range(cols // PROJ_CHUNK):
                c0 = j * PROJ_CHUNK
                d_idx = (j,) if dst_chunked else (slice(None), slice(c0, c0 + PROJ_CHUNK))
                pieces.append((src, (layer, slice(None), slice(c0, c0 + PROJ_CHUNK)),
                               dst, d_idx, (rows, PROJ_CHUNK)))

        add(w_in_hbm, w_in_ref, d, n_chunks * PROJ_CHUNK, dst_chunked=True)
        add(pool_proj_hbm, pool_proj_ref, pw, d)
        add(cpw_hbm, cpw_ref, cw, d)
        add(ao_hbm, ao_ref, aw, d)
        add(wout_hbm, wout_ref, d, d)
        gd_ = pw // len(POOL_WINDOWS)
        for g in range(len(POOL_WINDOWS)):
            pieces.append((pool_w_hbm, (layer, g), pool_w_ref, (g,), (gd_, gd_)))
        _load_matrices_bf16(pieces, stage, stage_sem)

    x_in = x_ref[0]
    x_scr[nxt] = x_in
    xb_scr[...] = x_in.astype(BF16)

    x = x_scr[cur]

    def proj(lo, hi):
        return jnp.concatenate(
            [h_scr[cur, c] for c in range(lo // PROJ_CHUNK, hi // PROJ_CHUNK)], axis=1)

    glu = proj(c_conv, c_conv + cw) * jax.nn.sigmoid(proj(c_conv + cw, c_conv + 2 * cw))
    conv_ext[CONV_HALO:CONV_HALO + ts, :] = glu
    taps = dw_ref.shape[1]
    base = CONV_HALO - (taps - 1)
    span = CONV_ROWS + CONV_HALO
    n_row_chunks = ts // CONV_ROWS
    chunks_per_iter = min(LOOP_PROJ_CHUNKS, n_chunks // n_row_chunks)
    later_chunks = list(range(n_row_chunks * chunks_per_iter, n_chunks))

    def project(c):
        h_scr[nxt, c] = _dot(xb_scr[...], w_in_ref[c]) + b_in_ref[0, c]

    def project_some(n):
        for _ in range(min(n, len(later_chunks))):
            project(later_chunks.pop(0))

    def conv_rows_and_projection(j):
        r0 = j * CONV_ROWS
        for c0 in range(0, cw, LANES):
            window = conv_ext[pl.ds(r0, span), c0:c0 + LANES]
            acc = jnp.broadcast_to(dwb_ref[0, :, c0:c0 + LANES], (CONV_ROWS, LANES))
            for res in range(SUBLANES):
                offs = [o for o in range(base, base + taps) if o % SUBLANES == res]
                if not offs:
                    continue
                shifted = window if res == 0 else pltpu.roll(window, span - res, axis=0)
                for o in offs:
                    k = o - base
                    acc = acc + dw_ref[0, k:k + 1, c0:c0 + LANES] * shifted[
                        o - res:o - res + CONV_ROWS]
            conv_out[pl.ds(r0, CONV_ROWS), c0:c0 + LANES] = acc
        for m in range(chunks_per_iter):
            project(j * chunks_per_iter + m)

    for j in range(n_row_chunks):
        conv_rows_and_projection(j)
    conv_ext[0:CONV_HALO, :] = conv_ext[ts:ts + CONV_HALO, :]
    hc = _layer_norm(conv_out[...], cg_ref[0], cb_ref[0])
    hc = hc * jax.nn.sigmoid(hc)
    project_some(2)
    y_conv = _dot(hc.astype(BF16), cpw_ref[...]) + cpwb_ref[0]

    u = proj(0, pw)
    pool_ext[POOL_HALO:POOL_HALO + ts, :] = u
    t_glob = i * ts + lax.broadcasted_iota(I32, (ts, 1), 0)
    gd = pw // len(POOL_WINDOWS)
    pooled = []
    for g, w in enumerate(POOL_WINDOWS):
        lo = g * gd
        ug = u[:, lo:lo + gd]
        run = pool_ext[:, lo:lo + gd]
        step = 1
        while step < w:
            run = run + pltpu.roll(run, step, axis=0)
            step *= 2
        cnt = jnp.minimum(t_glob + 1, w).astype(F32)
        pooled.append((run[POOL_HALO:POOL_HALO + ts] / cnt - ug).astype(BF16))
    project_some(2)
    mixed = [_dot(p, pool_w_ref[g]) for g, p in enumerate(pooled)]
    mixed = jnp.concatenate(mixed, axis=1) * pool_scale_ref[0]
    y_pool = _dot(mixed.astype(BF16), pool_proj_ref[...])
    pool_ext[0:POOL_HALO, :] = pool_ext[ts:ts + POOL_HALO, :]

    q = proj(c_q, c_q + aw)
    heads = []
    for h in range(aw // HEAD_DIM):
        lo = h * HEAD_DIM
        sc = _dot(q[:, lo:lo + HEAD_DIM].astype(BF16), kt_ref[0, 0, lo:lo + HEAD_DIM, :])
        sc = sc * (HEAD_DIM ** -0.5)
        p = jnp.exp(sc - jnp.max(sc, axis=-1, keepdims=True))
        o = _dot(p.astype(BF16), v_ref[0, 0, :, lo:lo + HEAD_DIM])
        heads.append(o / jnp.sum(p, axis=-1, keepdims=True))
    y_attn = _dot(jnp.concatenate(heads, axis=1).astype(BF16), ao_ref[...])

    merged = jax.nn.sigmoid(proj(c_gate, c_gate + d)) * y_pool
    merged = merged + jax.nn.sigmoid(proj(c_gate + d, c_gate + 2 * d)) * y_conv
    merged = merged + jax.nn.sigmoid(proj(c_gate + 2 * d, c_gate + 3 * d)) * y_attn
    project_some(2)
    out = _dot(merged.astype(BF16), wout_ref[...]) + bout_ref[0]
    x1 = _layer_norm(alpha * x + out, g1_ref[0], b1_ref[0])
    x1_ref[0] = x1

    x1_hi = x1.astype(BF16)
    x1_lo = (x1 - x1_hi.astype(F32)).astype(BF16)
    logits = _dot(x1_hi, wr_hi_ref[0]) + _dot(x1_lo, wr_hi_ref[0]) + _dot(x1_hi, wr_lo_ref[0])
    logits = logits.T[0:n_e, :] + rb_ref[0]
    e_iota = lax.broadcasted_iota(I32, (n_e, ts), 0)
    rank = jnp.zeros((n_e, ts), F32)
    for other in range(n_e):
        row = logits[other:other + 1, :]
        beats = (row > logits) | ((row == logits) & (e_iota > other))
        rank = rank + jnp.where(beats, 1.0, 0.0)
    hot = jnp.where(rank < TOP_K, 1.0, 0.0)
    top_v, top_i = [], []
    for k in range(TOP_K):
        sel = rank == k
        top_i.append(jnp.sum(jnp.where(sel, e_iota, 0), axis=0, keepdims=True))
        top_v.append(jnp.sum(jnp.where(sel, logits, 0.0), axis=0, keepdims=True))
    ex = [jnp.exp(v - top_v[0]) for v in top_v]
    den = ex[0] + ex[1] + ex[2] + ex[3]
    sub = lax.broadcasted_iota(I32, (SUBLANES, ts), 0)
    route = jnp.zeros((SUBLANES, ts), F32)
    for k in range(TOP_K):
        ids_ref[k:k + 1, :] = top_i[k]
        route = jnp.where(sub == k, top_i[k].astype(F32), route)
        route = jnp.where(sub == TOP_K + k, ex[k] / den, route)
    route_t_ref[...] = jnp.concatenate(
        [route, jnp.zeros((LANES - SUBLANES, ts), F32)], axis=0).T

    hot_wide = jnp.concatenate([hot, jnp.zeros((LANES - n_e, ts), F32)], axis=0).astype(BF16)
    cnt = lax.dot_general(jnp.ones((SUBLANES, ts), BF16), hot_wide, (((1,), (1,)), ((), ())),
                          preferred_element_type=F32)[0:1]
    is_expert = lax.broadcasted_iota(I32, (1, LANES), 1) < n_e
    padded = jnp.where(is_expert, jnp.maximum(jnp.ceil(cnt * (1.0 / SEG_ALIGN)), 1.0), 0.0)
    padded = padded * SEG_ALIGN
    padded = jnp.where(s >= 1, padded, 0.0)
    cnt_scr[pl.ds(tile, 1), :] = padded
    pre_scr[pl.ds(tile, 1), :] = run_scr[0:1, :]
    run_scr[0:1, :] = run_scr[0:1, :] + padded

    project_some(len(later_chunks))

    @pl.when(s == n_tiles)
    def _():
        _routing_tables(cnt_scr[...], pre_scr[...], n_e,
                        seg_off_ref, seg_len_ref, dest_ref, meta_ref)


def _mixer(x, weights, kt, v, layer, alpha):
    n_b, s, d = x.shape
    ts = TOKEN_TILE
    n_s = s // ts
    n_t = n_b * n_s
    names = ("w_in", "b_in", "pool_w", "pool_scale", "pool_proj", "conv_dw", "conv_dw_b",
             "conv_ln_g", "conv_ln_b", "conv_pw", "conv_pw_b")
    names2 = ("attn_o", "w_out", "b_out", "ln1_g", "ln1_b", "router_hi", "router_lo", "router_b")
    in_hbm = ("w_in", "pool_w", "pool_proj", "conv_pw", "attn_o", "w_out")
    pw = weights["pool_proj"].shape[1]
    cw = weights["conv_pw"].shape[1]
    aw = weights["attn_o"].shape[1]
    n_chunks = weights["w_in"].shape[2] // PROJ_CHUNK
    gd = pw // len(POOL_WINDOWS)

    def of_layer(arr):
        tail = (0,) * (arr.ndim - 1)
        return pl.BlockSpec((1,) + arr.shape[1:], lambda st: (layer,) + tail,
                            pipeline_mode=pl.Buffered(1))

    def spec(name):
        if name in in_hbm:
            return pl.BlockSpec(memory_space=pl.ANY)
        return of_layer(weights[name])

    def resident(shape):
        return pl.BlockSpec(shape, lambda st: (0, 0))

    def tile_a(st):
        return jnp.minimum(st, n_t - 1)

    def tile_b(st):
        return jnp.maximum(st - 1, 0)

    in_specs = ([pl.BlockSpec((1, ts, d), lambda st: (tile_a(st) // n_s, tile_a(st) % n_s, 0))]
                + [spec(k) for k in names]
                + [pl.BlockSpec((1, 1) + kt.shape[2:],
                                lambda st: (layer, tile_b(st) // n_s, 0, 0)),
                   pl.BlockSpec((1, 1) + v.shape[2:],
                                lambda st: (layer, tile_b(st) // n_s, 0, 0))]
                + [spec(k) for k in names2])
    out_specs = [
        pl.BlockSpec((1, ts, d), lambda st: (tile_b(st) // n_s, tile_b(st) % n_s, 0)),
        pl.BlockSpec((TOP_K, ts), lambda st: (0, tile_b(st))),
        pl.BlockSpec((ts, LANES), lambda st: (tile_b(st), 0)),
        resident((n_t, LANES)), resident((n_t, LANES)), resident((n_t, LANES)),
        resident((SUBLANES, META_LANES)),
    ]
    out_shape = [
        jax.ShapeDtypeStruct((n_b, s, d), F32),
        jax.ShapeDtypeStruct((TOP_K, n_b * s), I32),
        jax.ShapeDtypeStruct((n_b * s, LANES), F32),
        jax.ShapeDtypeStruct((n_t, LANES), I32),
        jax.ShapeDtypeStruct((n_t, LANES), I32),
        jax.ShapeDtypeStruct((n_t, LANES), I32),
        jax.ShapeDtypeStruct((SUBLANES, META_LANES), I32),
    ]
    return pl.pallas_call(
        functools.partial(_mixer_kernel, alpha=alpha, n_s=n_s, layer=layer),
        grid=(n_t + 1,),
        in_specs=in_specs,
        out_specs=out_specs,
        out_shape=out_shape,
        scratch_shapes=[
            pltpu.VMEM((n_chunks, d, PROJ_CHUNK), BF16),
            pltpu.VMEM((len(POOL_WINDOWS), gd, gd), BF16),
            pltpu.VMEM((pw, d), BF16),
            pltpu.VMEM((cw, d), BF16),
            pltpu.VMEM((aw, d), BF16),
            pltpu.VMEM((d, d), BF16),
            pltpu.VMEM((2, max(d, pw, cw, aw), PROJ_CHUNK), F32),
            pltpu.SemaphoreType.DMA((2,)),
            pltpu.VMEM((2, n_chunks, ts, PROJ_CHUNK), F32),
            pltpu.VMEM((2, ts, d), F32),
            pltpu.VMEM((ts, d), BF16),
            pltpu.VMEM((ts + POOL_HALO, pw), F32),
            pltpu.VMEM((ts + CONV_HALO, cw), F32),
            pltpu.VMEM((ts, cw), F32),
            pltpu.VMEM((n_t, LANES), F32),
            pltpu.VMEM((n_t, LANES), F32),
            pltpu.VMEM((SUBLANES, LANES), F32),
        ],
        compiler_params=pltpu.CompilerParams(
            dimension_semantics=("arbitrary",), vmem_limit_bytes=VMEM_LIMIT),
        name="mixer",
    )(x, *[weights[k] for k in names], kt, v, *[weights[k] for k in names2])


def _start_segment_copies(seg_off_ref, seg_len_ref, dest_ref, tile, n_e, make_copy):
    for e in range(n_e):
        make_copy(pl.multiple_of(seg_off_ref[tile, e], SEG_ALIGN),
                  pl.multiple_of(dest_ref[tile, e], SEG_ALIGN),
                  pl.multiple_of(seg_len_ref[tile, e], SEG_ALIGN)).start()


def _wait_segment_copies(seg_off_ref, seg_len_ref, tile, n_e, make_copy):
    total = seg_off_ref[tile, n_e - 1] + seg_len_ref[tile, n_e - 1]
    make_copy(0, 0, pl.multiple_of(total, SEG_ALIGN)).wait()


def _dispatch_kernel(seg_off_ref, seg_len_ref, dest_ref, meta_ref,
                     ids_ref, off_ref, x1_ref, xs_ref, buf, zbuf, sem, zsem, *, n_e):
    tile = pl.program_id(0)
    n_tiles = pl.num_programs(0)
    slot = tile % 2
    ts = x1_ref.shape[0]
    rows = buf.shape[1]

    @pl.when(tile == 0)
    def _():
        zbuf[...] = jnp.zeros(zbuf.shape, U32)

        def zero_copy(e):
            n = pl.multiple_of(meta_ref[META_PAD_LEN, e], SEG_ALIGN)
            dst = pl.multiple_of(meta_ref[META_PAD_DST, e], SEG_ALIGN)
            return n, pltpu.make_async_copy(zbuf.at[pl.ds(0, n)], xs_ref.at[pl.ds(dst, n)], zsem)

        def zstart(e, c):
            n, cp = zero_copy(e)
            pl.when(n > 0)(cp.start)
            return c

        def zwait(e, c):
            n, cp = zero_copy(e)
            pl.when(n > 0)(cp.wait)
            return c

        lax.fori_loop(0, n_e, zstart, 0)
        lax.fori_loop(0, n_e, zwait, 0)

    def copies_from(s):
        def make_copy(off, dst, n):
            return pltpu.make_async_copy(buf.at[s, pl.ds(off, n)], xs_ref.at[pl.ds(dst, n)],
                                         sem.at[s])
        return make_copy

    @pl.when(tile >= 2)
    def _():
        _wait_segment_copies(seg_off_ref, seg_len_ref, tile - 2, n_e, copies_from(slot))

    ids = ids_ref[...]
    e_iota = lax.broadcasted_iota(I32, (n_e, ts), 0)
    hots = [(e_iota == ids[k:k + 1, :]).astype(F32) for k in range(TOP_K)]
    hot = hots[0] + hots[1] + hots[2] + hots[3]
    off_row = off_ref[pl.ds(tile, 1), :].astype(F32)
    eye = (lax.broadcasted_iota(I32, (n_e, LANES), 0)
           == lax.broadcasted_iota(I32, (n_e, LANES), 1))
    off_col = jnp.sum(jnp.where(eye, off_row, 0.0), axis=1, keepdims=True)
    upper = (lax.broadcasted_iota(I32, (ts, ts), 0)
             < lax.broadcasted_iota(I32, (ts, ts), 1)).astype(BF16)
    place = _dot(hot.astype(BF16), upper) + off_col
    pos = [jnp.sum(hots[k] * place, axis=0, keepdims=True).astype(I32)
           for k in range(TOP_K)]
    x1b = x1_ref[...].astype(BF16)
    for r0 in range(0, rows, PERM_ROWS):
        r_iota = r0 + lax.broadcasted_iota(I32, (PERM_ROWS, ts), 0)
        hit = (r_iota == pos[0]) | (r_iota == pos[1]) | (r_iota == pos[2]) | (r_iota == pos[3])
        perm = jnp.where(hit, 1.0, 0.0).astype(BF16)
        buf[slot, r0:r0 + PERM_ROWS] = _pack_pairs(_dot(perm, x1b))
    _start_segment_copies(seg_off_ref, seg_len_ref, dest_ref, tile, n_e, copies_from(slot))

    @pl.when(tile == n_tiles - 1)
    def _():
        @pl.when(tile >= 1)
        def _():
            _wait_segment_copies(seg_off_ref, seg_len_ref, tile - 1, n_e, copies_from(1 - slot))

        _wait_segment_copies(seg_off_ref, seg_len_ref, tile, n_e, copies_from(slot))


def _tile_rows(ts, n_e):
    return TOP_K * ts + n_e * SEG_ALIGN


def _dispatch(tables, meta, ids, x1, n_rows, n_e):
    n_tok, d = x1.shape
    ts = TOKEN_TILE
    n_t = n_tok // ts
    rows = _tile_rows(ts, n_e)
    return pl.pallas_call(
        functools.partial(_dispatch_kernel, n_e=n_e),
        grid_spec=pltpu.PrefetchScalarGridSpec(
            num_scalar_prefetch=4,
            grid=(n_t,),
            in_specs=[
                pl.BlockSpec((TOP_K, ts), lambda t, *_: (0, t)),
                pl.BlockSpec((n_t, LANES), lambda t, *_: (0, 0)),
                pl.BlockSpec((ts, d), lambda t, *_: (t, 0)),
            ],
            out_specs=pl.BlockSpec(memory_space=pl.ANY),
            scratch_shapes=[pltpu.VMEM((2, rows, d // 2), U32),
                            pltpu.VMEM((EXPERT_BLOCK, d // 2), U32),
                            pltpu.SemaphoreType.DMA((2,)), pltpu.SemaphoreType.DMA(())],
        ),
        out_shape=jax.ShapeDtypeStruct((n_rows, d // 2), U32),
        compiler_params=pltpu.CompilerParams(
            dimension_semantics=("arbitrary",), vmem_limit_bytes=VMEM_LIMIT),
        name="dispatch",
    )(*tables, meta, ids, tables[0], x1)


def _expert_kernel(meta_ref, xs_ref, wup_ref, bup_ref, wdn_ref, bdn_ref,
                   ys_ref, wup_bf, wdn_bf):
    j = pl.program_id(0)
    f = wdn_ref.shape[2]

    @pl.when(j < meta_ref[META_N_USED, 0])
    def _():
        prev = meta_ref[META_BLK_E, jnp.maximum(j - 1, 0)]

        @pl.when((j == 0) | (meta_ref[META_BLK_E, j] != prev))
        def _():
            wup_bf[...] = wup_ref[0, 0].astype(BF16)
            wdn_bf[...] = wdn_ref[0, 0].astype(BF16)

        def expert_rows(n_rows):
            x = _unpack_pairs(xs_ref[0:n_rows, :])

            def up(c0):
                glu = _dot(x, wup_bf[:, c0:c0 + FF_CHUNK]) + bup_ref[0, 0, :, c0:c0 + FF_CHUNK]
                lin = (_dot(x, wup_bf[:, f + c0:f + c0 + FF_CHUNK])
                       + bup_ref[0, 0, :, f + c0:f + c0 + FF_CHUNK])
                return glu, lin

            y = jnp.broadcast_to(bdn_ref[0, 0], (n_rows, wdn_ref.shape[3]))
            nxt = up(0)
            for c0 in range(0, f, FF_CHUNK):
                glu, lin = nxt
                if c0 + FF_CHUNK < f:
                    nxt = up(c0 + FF_CHUNK)
                glu = jnp.minimum(glu, SWIGLU_LIMIT)
                lin = jnp.clip(lin, -SWIGLU_LIMIT, SWIGLU_LIMIT)
                act = glu * jax.nn.sigmoid(SWIGLU_ALPHA * glu) * (lin + 1.0)
                y = y + _dot(act.astype(BF16), wdn_bf[c0:c0 + FF_CHUNK, :])
            ys_ref[0:n_rows, :] = _pack_pairs(y.astype(BF16).astype(F32))

        bm = xs_ref.shape[0]
        real_rows = meta_ref[META_REAL_ROWS, j]
        pl.when(real_rows > bm // 2)(lambda: expert_rows(bm))
        pl.when(real_rows <= bm // 2)(lambda: expert_rows(bm // 2))


def _experts(meta, xs, w_up, b_up, w_down, b_down, layer):
    n_rows, half_d = xs.shape
    d = 2 * half_d
    f2 = w_up.shape[-1]
    f = f2 // 2
    bm = EXPERT_BLOCK
    n_blk = n_rows // bm

    def last_used(j, meta):
        return jnp.maximum(jnp.minimum(j, meta[META_N_USED, 0] - 1), 0)

    def row_blk(j, meta):
        return (last_used(j, meta), 0)

    def w_blk(j, meta):
        return (layer, meta[META_BLK_E, last_used(j, meta)], 0, 0)

    return pl.pallas_call(
        _expert_kernel,
        grid_spec=pltpu.PrefetchScalarGridSpec(
            num_scalar_prefetch=1,
            grid=(n_blk,),
            in_specs=[
                pl.BlockSpec((bm, half_d), row_blk),
                pl.BlockSpec((1, 1, d, f2), w_blk),
                pl.BlockSpec((1, 1, 1, f2), w_blk),
                pl.BlockSpec((1, 1, f, d), w_blk),
                pl.BlockSpec((1, 1, 1, d), w_blk),
            ],
            out_specs=pl.BlockSpec((bm, half_d), row_blk),
            scratch_shapes=[pltpu.VMEM((d, f2), BF16), pltpu.VMEM((f, d), BF16)],
        ),
        out_shape=jax.ShapeDtypeStruct((n_rows, half_d), U32),
        compiler_params=pltpu.CompilerParams(
            dimension_semantics=("arbitrary",), vmem_limit_bytes=VMEM_LIMIT),
        name="experts",
    )(meta, xs, w_up, b_up, w_down, b_down)


def _combine_kernel(seg_off_ref, seg_len_ref, dest_ref,
                    route_ref, off_ref, x1_ref, g2_ref, b2_ref, ys_ref,
                    out_ref, buf, sem, *, alpha, n_e):
    tile = pl.program_id(0)
    n_tiles = pl.num_programs(0)
    slot = tile % 2
    ts = x1_ref.shape[0]
    rows = buf.shape[1]

    def copies_into(s):
        def make_copy(off, dst, n):
            return pltpu.make_async_copy(ys_ref.at[pl.ds(dst, n)], buf.at[s, pl.ds(off, n)],
                                         sem.at[s])
        return make_copy

    @pl.when(tile == 0)
    def _():
        buf[...] = jnp.zeros(buf.shape, U32)
        _start_segment_copies(seg_off_ref, seg_len_ref, dest_ref, tile, n_e, copies_into(slot))

    nxt = jnp.minimum(tile + 1, n_tiles - 1)
    _start_segment_copies(seg_off_ref, seg_len_ref, dest_ref, nxt, n_e, copies_into(1 - slot))

    route = route_ref[...]
    l_iota = lax.broadcasted_iota(I32, (ts, LANES), 1)
    hots = [(l_iota == route[:, k:k + 1].astype(I32)).astype(F32) for k in range(TOP_K)]
    hot = hots[0] + hots[1] + hots[2] + hots[3]
    lower = (lax.broadcasted_iota(I32, (ts, ts), 1)
             < lax.broadcasted_iota(I32, (ts, ts), 0)).astype(BF16)
    place = _dot(lower, hot.astype(BF16)) + off_ref[pl.ds(tile, 1), :].astype(F32)
    pos = [jnp.sum(hots[k] * place, axis=1, keepdims=True).astype(I32)
           for k in range(TOP_K)]

    _wait_segment_copies(seg_off_ref, seg_len_ref, tile, n_e, copies_into(slot))
    y = jnp.zeros(out_ref.shape, F32)
    for r0 in range(0, rows, PERM_ROWS):
        r_iota = r0 + lax.broadcasted_iota(I32, (ts, PERM_ROWS), 1)
        weight = jnp.zeros((ts, PERM_ROWS), F32)
        for k in range(TOP_K):
            weight = jnp.where(r_iota == pos[k], route[:, TOP_K + k:TOP_K + k + 1], weight)
        y = y + _dot(weight.astype(BF16), _unpack_pairs(buf[slot, r0:r0 + PERM_ROWS]))
    out_ref[...] = _layer_norm(alpha * x1_ref[...] + y, g2_ref[0], b2_ref[0])

    @pl.when(tile == n_tiles - 1)
    def _():
        _wait_segment_copies(seg_off_ref, seg_len_ref, tile, n_e, copies_into(1 - slot))


def _combine(tables, route_t, x1, g2, b2, ys, layer, alpha, n_e):
    n_tok, d = x1.shape
    ts = TOKEN_TILE
    n_t = n_tok // ts
    rows = _tile_rows(ts, n_e)
    return pl.pallas_call(
        functools.partial(_combine_kernel, alpha=alpha, n_e=n_e),
        grid_spec=pltpu.PrefetchScalarGridSpec(
            num_scalar_prefetch=3,
            grid=(n_t,),
            in_specs=[
                pl.BlockSpec((ts, LANES), lambda t, *_: (t, 0)),
                pl.BlockSpec((n_t, LANES), lambda t, *_: (0, 0)),
                pl.BlockSpec((ts, d), lambda t, *_: (t, 0)),
                pl.BlockSpec((1, 1, d), lambda t, *_: (layer, 0, 0)),
                pl.BlockSpec((1, 1, d), lambda t, *_: (layer, 0, 0)),
                pl.BlockSpec(memory_space=pl.ANY),
            ],
            out_specs=pl.BlockSpec((ts, d), lambda t, *_: (t, 0)),
            scratch_shapes=[pltpu.VMEM((2, rows, d // 2), U32), pltpu.SemaphoreType.DMA((2,))],
        ),
        out_shape=jax.ShapeDtypeStruct((n_tok, d), F32),
        compiler_params=pltpu.CompilerParams(
            dimension_semantics=("arbitrary",), vmem_limit_bytes=VMEM_LIMIT),
        name="combine",
    )(*tables, route_t, tables[0], x1, g2, b2, ys)


def kernel(x, mem, mem_ln_g, mem_ln_b, w_in, b_in, pool_w, pool_scale, pool_proj, conv_dw, conv_dw_b, conv_ln_g, conv_ln_b, conv_pw, conv_pw_b, w_kv, attn_o, w_out, b_out, ln1_g, ln1_b, router_w, router_b, exp_up, exp_up_b, exp_down, exp_down_b, ln2_g, ln2_b):
    n_b, s, d = x.shape
    depth = w_in.shape[0]
    n_e = router_w.shape[-1]
    n_tok = n_b * s
    n_t = n_tok // TOKEN_TILE
    alpha = (2.0 * depth) ** 0.25
    n_rows = TOP_K * n_tok + n_t * n_e * SEG_ALIGN + n_e * EXPERT_BLOCK
    n_rows = -(-n_rows // EXPERT_BLOCK) * EXPERT_BLOCK
    assert n_rows // EXPERT_BLOCK <= META_LANES and n_e <= LANES

    def rows3(a):
        return a.reshape(a.shape[0], 1, a.shape[1])

    in_cols = w_in.shape[-1]
    assert in_cols % PROJ_CHUNK == 0
    n_chunks = in_cols // PROJ_CHUNK
    router_pad = jnp.pad(router_w, ((0, 0), (0, 0), (0, LANES - n_e)))
    router_hi = router_pad.astype(BF16)
    weights = dict(
        w_in=w_in, b_in=b_in.reshape(depth, n_chunks, 1, PROJ_CHUNK), pool_w=pool_w,
        pool_scale=rows3(pool_scale), pool_proj=pool_proj, conv_dw=conv_dw,
        conv_dw_b=rows3(conv_dw_b), conv_ln_g=rows3(conv_ln_g), conv_ln_b=rows3(conv_ln_b),
        conv_pw=conv_pw, conv_pw_b=rows3(conv_pw_b), attn_o=attn_o,
        w_out=w_out, b_out=rows3(b_out), ln1_g=rows3(ln1_g), ln1_b=rows3(ln1_b),
        router_hi=router_hi, router_lo=(router_pad - router_hi.astype(F32)).astype(BF16),
        router_b=router_b.reshape(depth, n_e, 1))
    up_b = exp_up_b.reshape(depth, n_e, 1, exp_up_b.shape[-1])
    down_b = exp_down_b.reshape(depth, n_e, 1, d)
    g2, b2 = rows3(ln2_g), rows3(ln2_b)

    kt_all, v_all = _memory_kv(mem, mem_ln_g, mem_ln_b, w_kv)
    for layer in range(depth):
        x1, ids, route_t, seg_off, seg_len, dest, meta = _mixer(
            x, weights, kt_all, v_all, layer, alpha)
        x1 = x1.reshape(n_tok, d)
        tables = (seg_off, seg_len, dest)
        xs = _dispatch(tables, meta, ids, x1, n_rows, n_e)
        ys = _experts(meta, xs, exp_up, up_b, exp_down, down_b, layer)
        x = _combine(tables, route_t, x1, g2, b2, ys, layer, alpha, n_e).reshape(n_b, s, d)
    return x
```

```python
import functools

import jax
import jax.numpy as jnp
from jax import lax
from jax.experimental import pallas as pl
from jax.experimental.pallas import tpu as pltpu

F32 = jnp.float32
BF16 = jnp.bfloat16
I32 = jnp.int32

POOL_WINDOWS = (2, 4, 8, 16)
POOL_HALO = 16
CONV_HALO = 32
HEAD_DIM = 128
TOP_K = 4
SWIGLU_LIMIT = 7.0
SWIGLU_ALPHA = 1.702
LN_EPS = 1e-5

LANES = 128
SUBLANES = 8
SEG_ALIGN = SUBLANES
U32 = jnp.uint32
TOKEN_TILE = 256
EXPERT_BLOCK = 512
FF_CHUNK = 256
CONV_ROWS = 64
PROJ_CHUNK = 256
LOOP_PROJ_CHUNKS = 5
SORT_ROWS = 128
PERM_ROWS = 256
META_LANES = 512
META_BLK_E, META_PAD_DST, META_PAD_LEN, META_N_USED, META_REAL_ROWS = 0, 1, 2, 3, 4
VMEM_LIMIT = 56 * 1024 * 1024


def _layer_norm(x, g, b):
    mu = jnp.mean(x, axis=-1, keepdims=True)
    xc = x - mu
    var = jnp.mean(xc * xc, axis=-1, keepdims=True)
    return xc * lax.rsqrt(var + LN_EPS) * g + b


def _dot(a, b):
    return jnp.dot(a, b, preferred_element_type=F32)


def _pack_pairs(v):
    n = v.shape[1] // 2
    lo = lax.bitcast_convert_type(v[:, :n], U32)
    hi = lax.bitcast_convert_type(v[:, n:], U32)
    return (hi & jnp.uint32(0xFFFF0000)) | (lo >> 16)


def _unpack_pairs(w):
    lo = lax.bitcast_convert_type(w << 16, F32)
    hi = lax.bitcast_convert_type(w & jnp.uint32(0xFFFF0000), F32)
    return jnp.concatenate([lo, hi], axis=1).astype(BF16)


def _dot_exact(a, b):
    return jnp.dot(a, b, preferred_element_type=F32, precision=lax.Precision.HIGHEST)


def _kv_kernel(mem_ref, g_ref, b_ref, wkv_ref, kt_ref, v_ref):
    a = kt_ref.shape[2]
    mem_n = _layer_norm(mem_ref[0], g_ref[...], b_ref[...]).astype(BF16)
    kv = _dot(mem_n, wkv_ref[0].astype(BF16))
    kt_ref[0, 0] = kv[:, :a].T.astype(BF16)
    v_ref[0, 0] = kv[:, a:].astype(BF16)


def _memory_kv(mem, g, b, w_kv):
    n_b, m, d = mem.shape
    n_l, _, a2 = w_kv.shape
    a = a2 // 2
    return pl.pallas_call(
        _kv_kernel,
        grid=(n_l, n_b),
        in_specs=[
            pl.BlockSpec((1, m, d), lambda l, bb: (bb, 0, 0)),
            pl.BlockSpec((1, d), lambda l, bb: (0, 0)),
            pl.BlockSpec((1, d), lambda l, bb: (0, 0)),
            pl.BlockSpec((1, d, a2), lambda l, bb: (l, 0, 0)),
        ],
        out_specs=[
            pl.BlockSpec((1, 1, a, m), lambda l, bb: (l, bb, 0, 0)),
            pl.BlockSpec((1, 1, m, a), lambda l, bb: (l, bb, 0, 0)),
        ],
        out_shape=[
            jax.ShapeDtypeStruct((n_l, n_b, a, m), BF16),
            jax.ShapeDtypeStruct((n_l, n_b, m, a), BF16),
        ],
        compiler_params=pltpu.CompilerParams(
            dimension_semantics=("arbitrary", "arbitrary"), vmem_limit_bytes=VMEM_LIMIT),
        name="memory_kv",
    )(mem, g.reshape(1, d), b.reshape(1, d), w_kv)


def _routing_tables(padded, prefix, n_e, seg_off_ref, seg_len_ref, dest_ref, meta_ref):
    row = lax.broadcasted_iota(I32, (LANES, LANES), 0)
    col = lax.broadcasted_iota(I32, (LANES, LANES), 1)
    seg_off = _dot_exact(padded, (row < col).astype(F32))
    tot = jnp.sum(padded, axis=0, keepdims=True)
    ptot = jnp.ceil(tot * (1.0 / EXPERT_BLOCK)) * EXPERT_BLOCK
    e_end = _dot_exact(jnp.broadcast_to(ptot, (SUBLANES, LANES)),
                       (row <= col).astype(F32))[0:1]
    e_start = e_end - ptot
    seg_off_ref[...] = seg_off.astype(I32)
    seg_len_ref[...] = padded.astype(I32)
    dest_ref[...] = (prefix + e_start).astype(I32)

    e_end_col = jnp.sum(jnp.where(row == col, jnp.broadcast_to(e_end, (LANES, LANES)), 0.0),
                        axis=1, keepdims=True)
    blk_row = (lax.broadcasted_iota(I32, (LANES, META_LANES), 1) * EXPERT_BLOCK).astype(F32)
    is_real = lax.broadcasted_iota(I32, (LANES, META_LANES), 0) < n_e
    blk_e = jnp.sum(jnp.where(is_real & (e_end_col <= blk_row), 1.0, 0.0), axis=0, keepdims=True)
    blk_e = jnp.minimum(blk_e, n_e - 1.0)
    lane = lax.broadcasted_iota(I32, (1, LANES), 1)
    n_used = jnp.sum(jnp.where(lane == n_e - 1, e_end, 0.0), axis=1, keepdims=True) * (
        1.0 / EXPERT_BLOCK)

    real_end_col = jnp.sum(jnp.where(row == col, jnp.broadcast_to(e_start + tot, (LANES, LANES)),
                                     0.0), axis=1, keepdims=True)
    of_blk = lax.broadcasted_iota(I32, (LANES, META_LANES), 0).astype(F32) == blk_e
    real_end = jnp.sum(jnp.where(of_blk, real_end_col, 0.0), axis=0, keepdims=True)
    real_rows = jnp.clip(real_end - blk_row[0:1], 0.0, 1.0 * EXPERT_BLOCK)

    def wide(r):
        return jnp.concatenate([r, jnp.zeros((1, META_LANES - LANES), F32)], axis=1)

    sub = lax.broadcasted_iota(I32, (SUBLANES, META_LANES), 0)
    meta = jnp.zeros((SUBLANES, META_LANES), F32)
    for r, val in ((META_BLK_E, blk_e), (META_PAD_DST, wide(e_start + tot)),
                   (META_PAD_LEN, wide(ptot - tot)),
                   (META_N_USED, jnp.broadcast_to(n_used, (1, META_LANES))),
                   (META_REAL_ROWS, real_rows)):
        meta = jnp.where(sub == r, jnp.broadcast_to(val, (SUBLANES, META_LANES)), meta)
    meta_ref[...] = meta.astype(I32)


def _load_matrices_bf16(pieces, stage, sem):
    def fetch(p):
        src, s_idx, _, _, (r, c) = pieces[p]
        return pltpu.make_async_copy(src.at[s_idx], stage.at[p % 2, 0:r, 0:c], sem.at[p % 2])

    fetch(0).start()
    for p, (_, _, dst, d_idx, (r, c)) in enumerate(pieces):
        if p + 1 < len(pieces):
            fetch(p + 1).start()
        fetch(p).wait()
        dst[d_idx] = stage[p % 2, 0:r, 0:c].astype(BF16)


def _mixer_kernel(x_ref, w_in_hbm, b_in_ref, pool_w_hbm, pool_scale_ref, pool_proj_hbm,
                  dw_ref, dwb_ref, cg_ref, cb_ref, cpw_hbm, cpwb_ref,
                  kt_ref, v_ref, ao_hbm, wout_hbm, bout_ref, g1_ref, b1_ref,
                  wr_hi_ref, wr_lo_ref, rb_ref,
                  x1_ref, ids_ref, route_t_ref, seg_off_ref, seg_len_ref, dest_ref, meta_ref,
                  w_in_ref, pool_w_ref, pool_proj_ref, cpw_ref, ao_ref, wout_ref, stage, stage_sem,
                  h_scr, x_scr, xb_scr, pool_ext, conv_ext, conv_out,
                  cnt_scr, pre_scr, run_scr,
                  *, alpha, n_s, layer):
    s = pl.program_id(0)
    n_tiles = pl.num_programs(0) - 1
    cur = (s + 1) % 2
    nxt = s % 2
    tile = jnp.maximum(s - 1, 0)
    i = tile % n_s
    ts, d = x_ref.shape[1], x_ref.shape[2]
    n_chunks = w_in_ref.shape[0]
    pw = pool_ext.shape[1]
    cw = conv_ext.shape[1]
    aw = kt_ref.shape[2]
    n_e = rb_ref.shape[1]
    c_conv = pw
    c_q = pw + 2 * cw
    c_gate = c_q + aw

    @pl.when(i == 0)
    def _():
        pool_ext[0:POOL_HALO, :] = jnp.zeros((POOL_HALO, pw), F32)
        conv_ext[0:CONV_HALO, :] = jnp.zeros((CONV_HALO, cw), F32)

    @pl.when(s == 0)
    def _():
        run_scr[...] = jnp.zeros(run_scr.shape, F32)
        h_scr[1] = jnp.zeros(h_scr.shape[1:], F32)
        x_scr[1] = jnp.zeros(x_scr.shape[1:], F32)
        pieces = []

        def add(src, dst, rows, cols, dst_chunked=False):
            for j in range(cols // PROJ_CHUNK):
                c0 = j * PROJ_CHUNK
                d_idx = (j,) if dst_chunked else (slice(None), slice(c0, c0 + PROJ_CHUNK))
                pieces.append((src, (layer, slice(None), slice(c0, c0 + PROJ_CHUNK)),
                               dst, d_idx, (rows, PROJ_CHUNK)))

        add(w_in_hbm, w_in_ref, d, n_chunks * PROJ_CHUNK, dst_chunked=True)
        add(pool_proj_hbm, pool_proj_ref, pw, d)
        add(cpw_hbm, cpw_ref, cw, d)
        add(ao_hbm, ao_ref, aw, d)
        add(wout_hbm, wout_ref, d, d)
        gd_ = pw // len(POOL_WINDOWS)
        for g in range(len(POOL_WINDOWS)):
            pieces.append((pool_w_hbm, (layer, g), pool_w_ref, (g,), (gd_, gd_)))
        _load_matrices_bf16(pieces, stage, stage_sem)

    x_in = x_ref[0]
    x_scr[nxt] = x_in
    xb_scr[...] = x_in.astype(BF16)

    x = x_scr[cur]

    def proj(lo, hi):
        return jnp.concatenate(
            [h_scr[cur, c] for c in range(lo // PROJ_CHUNK, hi // PROJ_CHUNK)], axis=1)

    glu = proj(c_conv, c_conv + cw) * jax.nn.sigmoid(proj(c_conv + cw, c_conv + 2 * cw))
    conv_ext[CONV_HALO:CONV_HALO + ts, :] = glu
    taps = dw_ref.shape[1]
    base = CONV_HALO - (taps - 1)
    span = CONV_ROWS + CONV_HALO
    n_row_chunks = ts // CONV_ROWS
    chunks_per_iter = min(LOOP_PROJ_CHUNKS, n_chunks // n_row_chunks)
    later_chunks = list(range(n_row_chunks * chunks_per_iter, n_chunks))

    def project(c):
        h_scr[nxt, c] = _dot(xb_scr[...], w_in_ref[c]) + b_in_ref[0, c]

    def project_some(n):
        for _ in range(min(n, len(later_chunks))):
            project(later_chunks.pop(0))

    def conv_rows_and_projection(j):
        r0 = j * CONV_ROWS
        for c0 in range(0, cw, LANES):
            window = conv_ext[pl.ds(r0, span), c0:c0 + LANES]
            acc = jnp.broadcast_to(dwb_ref[0, :, c0:c0 + LANES], (CONV_ROWS, LANES))
            for res in range(SUBLANES):
                offs = [o for o in range(base, base + taps) if o % SUBLANES == res]
                if not offs:
                    continue
                shifted = window if res == 0 else pltpu.roll(window, span - res, axis=0)
                for o in offs:
                    k = o - base
                    acc = acc + dw_ref[0, k:k + 1, c0:c0 + LANES] * shifted[
                        o - res:o - res + CONV_ROWS]
            conv_out[pl.ds(r0, CONV_ROWS), c0:c0 + LANES] = acc
        for m in range(chunks_per_iter):
            project(j * chunks_per_iter + m)

    for j in range(n_row_chunks):
        conv_rows_and_projection(j)
    conv_ext[0:CONV_HALO, :] = conv_ext[ts:ts + CONV_HALO, :]
    hc = _layer_norm(conv_out[...], cg_ref[0], cb_ref[0])
    hc = hc * jax.nn.sigmoid(hc)
    project_some(2)
    y_conv = _dot(hc.astype(BF16), cpw_ref[...]) + cpwb_ref[0]

    u = proj(0, pw)
    pool_ext[POOL_HALO:POOL_HALO + ts, :] = u
    t_glob = i * ts + lax.broadcasted_iota(I32, (ts, 1), 0)
    gd = pw // len(POOL_WINDOWS)
    pooled = []
    for g, w in enumerate(POOL_WINDOWS):
        lo = g * gd
        ug = u[:, lo:lo + gd]
        run = pool_ext[:, lo:lo + gd]
        step = 1
        while step < w:
            run = run + pltpu.roll(run, step, axis=0)
            step *= 2
        cnt = jnp.minimum(t_glob + 1, w).astype(F32)
        pooled.append((run[POOL_HALO:POOL_HALO + ts] / cnt - ug).astype(BF16))
    project_some(2)
    mixed = [_dot(p, pool_w_ref[g]) for g, p in enumerate(pooled)]
    mixed = jnp.concatenate(mixed, axis=1) * pool_scale_ref[0]
    y_pool = _dot(mixed.astype(BF16), pool_proj_ref[...])
    pool_ext[0:POOL_HALO, :] = pool_ext[ts:ts + POOL_HALO, :]

    q = proj(c_q, c_q + aw)
    heads = []
    for h in range(aw // HEAD_DIM):
        lo = h * HEAD_DIM
        sc = _dot(q[:, lo:lo + HEAD_DIM].astype(BF16), kt_ref[0, 0, lo:lo + HEAD_DIM, :])
        sc = sc * (HEAD_DIM ** -0.5)
        p = jnp.exp(sc - jnp.max(sc, axis=-1, keepdims=True))
        o = _dot(p.astype(BF16), v_ref[0, 0, :, lo:lo + HEAD_DIM])
        heads.append(o / jnp.sum(p, axis=-1, keepdims=True))
    y_attn = _dot(jnp.concatenate(heads, axis=1).astype(BF16), ao_ref[...])

    merged = jax.nn.sigmoid(proj(c_gate, c_gate + d)) * y_pool
    merged = merged + jax.nn.sigmoid(proj(c_gate + d, c_gate + 2 * d)) * y_conv
    merged = merged + jax.nn.sigmoid(proj(c_gate + 2 * d, c_gate + 3 * d)) * y_attn
    project_some(2)
    out = _dot(merged.astype(BF16), wout_ref[...]) + bout_ref[0]
    x1 = _layer_norm(alpha * x + out, g1_ref[0], b1_ref[0])
    x1_ref[0] = x1

    x1_hi = x1.astype(BF16)
    x1_lo = (x1 - x1_hi.astype(F32)).astype(BF16)
    logits = _dot(x1_hi, wr_hi_ref[0]) + _dot(x1_lo, wr_hi_ref[0]) + _dot(x1_hi, wr_lo_ref[0])
    logits = logits.T[0:n_e, :] + rb_ref[0]
    e_iota = lax.broadcasted_iota(I32, (n_e, ts), 0)
    rank = jnp.zeros((n_e, ts), F32)
    for other in range(n_e):
        row = logits[other:other + 1, :]
        beats = (row > logits) | ((row == logits) & (e_iota > other))
        rank = rank + jnp.where(beats, 1.0, 0.0)
    hot = jnp.where(rank < TOP_K, 1.0, 0.0)
    top_v, top_i = [], []
    for k in range(TOP_K):
        sel = rank == k
        top_i.append(jnp.sum(jnp.where(sel, e_iota, 0), axis=0, keepdims=True))
        top_v.append(jnp.sum(jnp.where(sel, logits, 0.0), axis=0, keepdims=True))
    ex = [jnp.exp(v - top_v[0]) for v in top_v]
    den = ex[0] + ex[1] + ex[2] + ex[3]
    sub = lax.broadcasted_iota(I32, (SUBLANES, ts), 0)
    route = jnp.zeros((SUBLANES, ts), F32)
    for k in range(TOP_K):
        ids_ref[k:k + 1, :] = top_i[k]
        route = jnp.where(sub == k, top_i[k].astype(F32), route)
        route = jnp.where(sub == TOP_K + k, ex[k] / den, route)
    route_t_ref[...] = jnp.concatenate(
        [route, jnp.zeros((LANES - SUBLANES, ts), F32)], axis=0).T

    hot_wide = jnp.concatenate([hot, jnp.zeros((LANES - n_e, ts), F32)], axis=0).astype(BF16)
    cnt = lax.dot_general(jnp.ones((SUBLANES, ts), BF16), hot_wide, (((1,), (1,)), ((), ())),
                          preferred_element_type=F32)[0:1]
    is_expert = lax.broadcasted_iota(I32, (1, LANES), 1) < n_e
    padded = jnp.where(is_expert, jnp.maximum(jnp.ceil(cnt * (1.0 / SEG_ALIGN)), 1.0), 0.0)
    padded = padded * SEG_ALIGN
    padded = jnp.where(s >= 1, padded, 0.0)
    cnt_scr[pl.ds(tile, 1), :] = padded
    pre_scr[pl.ds(tile, 1), :] = run_scr[0:1, :]
    run_scr[0:1, :] = run_scr[0:1, :] + padded

    project_some(len(later_chunks))

    @pl.when(s == n_tiles)
    def _():
        _routing_tables(cnt_scr[...], pre_scr[...], n_e,
                        seg_off_ref, seg_len_ref, dest_ref, meta_ref)


def _mixer(x, weights, kt, v, layer, alpha):
    n_b, s, d = x.shape
    ts = TOKEN_TILE
    n_s = s // ts
    n_t = n_b * n_s
    names = ("w_in", "b_in", "pool_w", "pool_scale", "pool_proj", "conv_dw", "conv_dw_b",
             "conv_ln_g", "conv_ln_b", "conv_pw", "conv_pw_b")
    names2 = ("attn_o", "w_out", "b_out", "ln1_g", "ln1_b", "router_hi", "router_lo", "router_b")
    in_hbm = ("w_in", "pool_w", "pool_proj", "conv_pw", "attn_o", "w_out")
    pw = weights["pool_proj"].shape[1]
    cw = weights["conv_pw"].shape[1]
    aw = weights["attn_o"].shape[1]
    n_chunks = weights["w_in"].shape[2] // PROJ_CHUNK
    gd = pw // len(POOL_WINDOWS)

    def of_layer(arr):
        tail = (0,) * (arr.ndim - 1)
        return pl.BlockSpec((1,) + arr.shape[1:], lambda st: (layer,) + tail,
                            pipeline_mode=pl.Buffered(1))

    def spec(name):
        if name in in_hbm:
            return pl.BlockSpec(memory_space=pl.ANY)
        return of_layer(weights[name])

    def resident(shape):
        return pl.BlockSpec(shape, lambda st: (0, 0))

    def tile_a(st):
        return jnp.minimum(st, n_t - 1)

    def tile_b(st):
        return jnp.maximum(st - 1, 0)

    in_specs = ([pl.BlockSpec((1, ts, d), lambda st: (tile_a(st) // n_s, tile_a(st) % n_s, 0))]
                + [spec(k) for k in names]
                + [pl.BlockSpec((1, 1) + kt.shape[2:],
                                lambda st: (layer, tile_b(st) // n_s, 0, 0)),
                   pl.BlockSpec((1, 1) + v.shape[2:],
                                lambda st: (layer, tile_b(st) // n_s, 0, 0))]
                + [spec(k) for k in names2])
    out_specs = [
        pl.BlockSpec((1, ts, d), lambda st: (tile_b(st) // n_s, tile_b(st) % n_s, 0)),
        pl.BlockSpec((TOP_K, ts), lambda st: (0, tile_b(st))),
        pl.BlockSpec((ts, LANES), lambda st: (tile_b(st), 0)),
        resident((n_t, LANES)), resident((n_t, LANES)), resident((n_t, LANES)),
        resident((SUBLANES, META_LANES)),
    ]
    out_shape = [
        jax.ShapeDtypeStruct((n_b, s, d), F32),
        jax.ShapeDtypeStruct((TOP_K, n_b * s), I32),
        jax.ShapeDtypeStruct((n_b * s, LANES), F32),
        jax.ShapeDtypeStruct((n_t, LANES), I32),
        jax.ShapeDtypeStruct((n_t, LANES), I32),
        jax.ShapeDtypeStruct((n_t, LANES), I32),
        jax.ShapeDtypeStruct((SUBLANES, META_LANES), I32),
    ]
    return pl.pallas_call(
        functools.partial(_mixer_kernel, alpha=alpha, n_s=n_s, layer=layer),
        grid=(n_t + 1,),
        in_specs=in_specs,
        out_specs=out_specs,
        out_shape=out_shape,
        scratch_shapes=[
            pltpu.VMEM((n_chunks, d, PROJ_CHUNK), BF16),
            pltpu.VMEM((len(POOL_WINDOWS), gd, gd), BF16),
            pltpu.VMEM((pw, d), BF16),
            pltpu.VMEM((cw, d), BF16),
            pltpu.VMEM((aw, d), BF16),
            pltpu.VMEM((d, d), BF16),
            pltpu.VMEM((2, max(d, pw, cw, aw), PROJ_CHUNK), F32),
            pltpu.SemaphoreType.DMA((2,)),
            pltpu.VMEM((2, n_chunks, ts, PROJ_CHUNK), F32),
            pltpu.VMEM((2, ts, d), F32),
            pltpu.VMEM((ts, d), BF16),
            pltpu.VMEM((ts + POOL_HALO, pw), F32),
            pltpu.VMEM((ts + CONV_HALO, cw), F32),
            pltpu.VMEM((ts, cw), F32),
            pltpu.VMEM((n_t, LANES), F32),
            pltpu.VMEM((n_t, LANES), F32),
            pltpu.VMEM((SUBLANES, LANES), F32),
        ],
        compiler_params=pltpu.CompilerParams(
            dimension_semantics=("arbitrary",), vmem_limit_bytes=VMEM_LIMIT),
        name="mixer",
    )(x, *[weights[k] for k in names], kt, v, *[weights[k] for k in names2])


def _start_segment_copies(seg_off_ref, seg_len_ref, dest_ref, tile, n_e, make_copy):
    for e in range(n_e):
        make_copy(pl.multiple_of(seg_off_ref[tile, e], SEG_ALIGN),
                  pl.multiple_of(dest_ref[tile, e], SEG_ALIGN),
                  pl.multiple_of(seg_len_ref[tile, e], SEG_ALIGN)).start()


def _wait_segment_copies(seg_off_ref, seg_len_ref, tile, n_e, make_copy):
    total = seg_off_ref[tile, n_e - 1] + seg_len_ref[tile, n_e - 1]
    make_copy(0, 0, pl.multiple_of(total, SEG_ALIGN)).wait()


def _dispatch_kernel(seg_off_ref, seg_len_ref, dest_ref, meta_ref,
                     ids_ref, off_ref, x1_ref, xs_ref, buf, zbuf, sem, zsem, *, n_e):
    tile = pl.program_id(0)
    n_tiles = pl.num_programs(0)
    slot = tile % 2
    ts = x1_ref.shape[0]
    rows = buf.shape[1]

    @pl.when(tile == 0)
    def _():
        zbuf[...] = jnp.zeros(zbuf.shape, U32)

        def zero_copy(e):
            n = pl.multiple_of(meta_ref[META_PAD_LEN, e], SEG_ALIGN)
            dst = pl.multiple_of(meta_ref[META_PAD_DST, e], SEG_ALIGN)
            return n, pltpu.make_async_copy(zbuf.at[pl.ds(0, n)], xs_ref.at[pl.ds(dst, n)], zsem)

        def zstart(e, c):
            n, cp = zero_copy(e)
            pl.when(n > 0)(cp.start)
            return c

        def zwait(e, c):
            n, cp = zero_copy(e)
            pl.when(n > 0)(cp.wait)
            return c

        lax.fori_loop(0, n_e, zstart, 0)
        lax.fori_loop(0, n_e, zwait, 0)

    def copies_from(s):
        def make_copy(off, dst, n):
            return pltpu.make_async_copy(buf.at[s, pl.ds(off, n)], xs_ref.at[pl.ds(dst, n)],
                                         sem.at[s])
        return make_copy

    @pl.when(tile >= 2)
    def _():
        _wait_segment_copies(seg_off_ref, seg_len_ref, tile - 2, n_e, copies_from(slot))

    ids = ids_ref[...]
    e_iota = lax.broadcasted_iota(I32, (n_e, ts), 0)
    hots = [(e_iota == ids[k:k + 1, :]).astype(F32) for k in range(TOP_K)]
    hot = hots[0] + hots[1] + hots[2] + hots[3]
    off_row = off_ref[pl.ds(tile, 1), :].astype(F32)
    eye = (lax.broadcasted_iota(I32, (n_e, LANES), 0)
           == lax.broadcasted_iota(I32, (n_e, LANES), 1))
    off_col = jnp.sum(jnp.where(eye, off_row, 0.0), axis=1, keepdims=True)
    upper = (lax.broadcasted_iota(I32, (ts, ts), 0)
             < lax.broadcasted_iota(I32, (ts, ts), 1)).astype(BF16)
    place = _dot(hot.astype(BF16), upper) + off_col
    pos = [jnp.sum(hots[k] * place, axis=0, keepdims=True).astype(I32)
           for k in range(TOP_K)]
    x1b = x1_ref[...].astype(BF16)
    for r0 in range(0, rows, SORT_ROWS):
        r_iota = r0 + lax.broadcasted_iota(I32, (SORT_ROWS, ts), 0)
        hit = (r_iota == pos[0]) | (r_iota == pos[1]) | (r_iota == pos[2]) | (r_iota == pos[3])
        perm = jnp.where(hit, 1.0, 0.0).astype(BF16)
        buf[slot, r0:r0 + SORT_ROWS] = _pack_pairs(_dot(perm, x1b))
    _start_segment_copies(seg_off_ref, seg_len_ref, dest_ref, tile, n_e, copies_from(slot))

    @pl.when(tile == n_tiles - 1)
    def _():
        @pl.when(tile >= 1)
        def _():
            _wait_segment_copies(seg_off_ref, seg_len_ref, tile - 1, n_e, copies_from(1 - slot))

        _wait_segment_copies(seg_off_ref, seg_len_ref, tile, n_e, copies_from(slot))


def _tile_rows(ts, n_e):
    return TOP_K * ts + n_e * SEG_ALIGN


def _dispatch(tables, meta, ids, x1, n_rows, n_e):
    n_tok, d = x1.shape
    ts = TOKEN_TILE
    n_t = n_tok // ts
    rows = _tile_rows(ts, n_e)
    return pl.pallas_call(
        functools.partial(_dispatch_kernel, n_e=n_e),
        grid_spec=pltpu.PrefetchScalarGridSpec(
            num_scalar_prefetch=4,
            grid=(n_t,),
            in_specs=[
                pl.BlockSpec((TOP_K, ts), lambda t, *_: (0, t)),
                pl.BlockSpec((n_t, LANES), lambda t, *_: (0, 0)),
                pl.BlockSpec((ts, d), lambda t, *_: (t, 0)),
            ],
            out_specs=pl.BlockSpec(memory_space=pl.ANY),
            scratch_shapes=[pltpu.VMEM((2, rows, d // 2), U32),
                            pltpu.VMEM((EXPERT_BLOCK, d // 2), U32),
                            pltpu.SemaphoreType.DMA((2,)), pltpu.SemaphoreType.DMA(())],
        ),
        out_shape=jax.ShapeDtypeStruct((n_rows, d // 2), U32),
        compiler_params=pltpu.CompilerParams(
            dimension_semantics=("arbitrary",), vmem_limit_bytes=VMEM_LIMIT),
        name="dispatch",
    )(*tables, meta, ids, tables[0], x1)


def _expert_kernel(meta_ref, xs_ref, wup_ref, bup_ref, wdn_ref, bdn_ref,
                   ys_ref, wup_bf, wdn_bf):
    j = pl.program_id(0)
    f = wdn_ref.shape[2]

    @pl.when(j < meta_ref[META_N_USED, 0])
    def _():
        prev = meta_ref[META_BLK_E, jnp.maximum(j - 1, 0)]

        @pl.when((j == 0) | (meta_ref[META_BLK_E, j] != prev))
        def _():
            wup_bf[...] = wup_ref[0, 0].astype(BF16)
            wdn_bf[...] = wdn_ref[0, 0].astype(BF16)

        def expert_rows(n_rows):
            x = _unpack_pairs(xs_ref[0:n_rows, :])

            def up(c0):
                glu = _dot(x, wup_bf[:, c0:c0 + FF_CHUNK]) + bup_ref[0, 0, :, c0:c0 + FF_CHUNK]
                lin = (_dot(x, wup_bf[:, f + c0:f + c0 + FF_CHUNK])
                       + bup_ref[0, 0, :, f + c0:f + c0 + FF_CHUNK])
                return glu, lin

            y = jnp.broadcast_to(bdn_ref[0, 0], (n_rows, wdn_ref.shape[3]))
            nxt = up(0)
            for c0 in range(0, f, FF_CHUNK):
                glu, lin = nxt
                if c0 + FF_CHUNK < f:
                    nxt = up(c0 + FF_CHUNK)
                glu = jnp.minimum(glu, SWIGLU_LIMIT)
                lin = jnp.clip(lin, -SWIGLU_LIMIT, SWIGLU_LIMIT)
                act = glu * jax.nn.sigmoid(SWIGLU_ALPHA * glu) * (lin + 1.0)
                y = y + _dot(act.astype(BF16), wdn_bf[c0:c0 + FF_CHUNK, :])
            ys_ref[0:n_rows, :] = _pack_pairs(y.astype(BF16).astype(F32))

        bm = xs_ref.shape[0]
        real_rows = meta_ref[META_REAL_ROWS, j]
        pl.when(real_rows > bm // 2)(lambda: expert_rows(bm))
        pl.when(real_rows <= bm // 2)(lambda: expert_rows(bm // 2))


def _experts(meta, xs, w_up, b_up, w_down, b_down, layer):
    n_rows, half_d = xs.shape
    d = 2 * half_d
    f2 = w_up.shape[-1]
    f = f2 // 2
    bm = EXPERT_BLOCK
    n_blk = n_rows // bm

    def last_used(j, meta):
        return jnp.maximum(jnp.minimum(j, meta[META_N_USED, 0] - 1), 0)

    def row_blk(j, meta):
        return (last_used(j, meta), 0)

    def w_blk(j, meta):
        return (layer, meta[META_BLK_E, last_used(j, meta)], 0, 0)

    return pl.pallas_call(
        _expert_kernel,
        grid_spec=pltpu.PrefetchScalarGridSpec(
            num_scalar_prefetch=1,
            grid=(n_blk,),
            in_specs=[
                pl.BlockSpec((bm, half_d), row_blk),
                pl.BlockSpec((1, 1, d, f2), w_blk),
                pl.BlockSpec((1, 1, 1, f2), w_blk),
                pl.BlockSpec((1, 1, f, d), w_blk),
                pl.BlockSpec((1, 1, 1, d), w_blk),
            ],
            out_specs=pl.BlockSpec((bm, half_d), row_blk),
            scratch_shapes=[pltpu.VMEM((d, f2), BF16), pltpu.VMEM((f, d), BF16)],
        ),
        out_shape=jax.ShapeDtypeStruct((n_rows, half_d), U32),
        compiler_params=pltpu.CompilerParams(
            dimension_semantics=("arbitrary",), vmem_limit_bytes=VMEM_LIMIT),
        name="experts",
    )(meta, xs, w_up, b_up, w_down, b_down)


def _combine_kernel(seg_off_ref, seg_len_ref, dest_ref,
                    route_ref, off_ref, x1_ref, g2_ref, b2_ref, ys_ref,
                    out_ref, buf, sem, *, alpha, n_e):
    tile = pl.program_id(0)
    n_tiles = pl.num_programs(0)
    slot = tile % 2
    ts = x1_ref.shape[0]
    rows = buf.shape[1]

    def copies_into(s):
        def make_copy(off, dst, n):
            return pltpu.make_async_copy(ys_ref.at[pl.ds(dst, n)], buf.at[s, pl.ds(off, n)],
                                         sem.at[s])
        return make_copy

    @pl.when(tile == 0)
    def _():
        buf[...] = jnp.zeros(buf.shape, U32)
        _start_segment_copies(seg_off_ref, seg_len_ref, dest_ref, tile, n_e, copies_into(slot))

    nxt = jnp.minimum(tile + 1, n_tiles - 1)
    _start_segment_copies(seg_off_ref, seg_len_ref, dest_ref, nxt, n_e, copies_into(1 - slot))

    route = route_ref[...]
    l_iota = lax.broadcasted_iota(I32, (ts, LANES), 1)
    hots = [(l_iota == route[:, k:k + 1].astype(I32)).astype(F32) for k in range(TOP_K)]
    hot = hots[0] + hots[1] + hots[2] + hots[3]
    lower = (lax.broadcasted_iota(I32, (ts, ts), 1)
             < lax.broadcasted_iota(I32, (ts, ts), 0)).astype(BF16)
    place = _dot(lower, hot.astype(BF16)) + off_ref[pl.ds(tile, 1), :].astype(F32)
    pos = [jnp.sum(hots[k] * place, axis=1, keepdims=True).astype(I32)
           for k in range(TOP_K)]

    _wait_segment_copies(seg_off_ref, seg_len_ref, tile, n_e, copies_into(slot))
    y = jnp.zeros(out_ref.shape, F32)
    for r0 in range(0, rows, PERM_ROWS):
        r_iota = r0 + lax.broadcasted_iota(I32, (ts, PERM_ROWS), 1)
        weight = jnp.zeros((ts, PERM_ROWS), F32)
        for k in range(TOP_K):
            weight = jnp.where(r_iota == pos[k], route[:, TOP_K + k:TOP_K + k + 1], weight)
        y = y + _dot(weight.astype(BF16), _unpack_pairs(buf[slot, r0:r0 + PERM_ROWS]))
    out_ref[...] = _layer_norm(alpha * x1_ref[...] + y, g2_ref[0], b2_ref[0])

    @pl.when(tile == n_tiles - 1)
    def _():
        _wait_segment_copies(seg_off_ref, seg_len_ref, tile, n_e, copies_into(1 - slot))


def _combine(tables, route_t, x1, g2, b2, ys, layer, alpha, n_e):
    n_tok, d = x1.shape
    ts = TOKEN_TILE
    n_t = n_tok // ts
    rows = _tile_rows(ts, n_e)
    return pl.pallas_call(
        functools.partial(_combine_kernel, alpha=alpha, n_e=n_e),
        grid_spec=pltpu.PrefetchScalarGridSpec(
            num_scalar_prefetch=3,
            grid=(n_t,),
            in_specs=[
                pl.BlockSpec((ts, LANES), lambda t, *_: (t, 0)),
                pl.BlockSpec((n_t, LANES), lambda t, *_: (0, 0)),
                pl.BlockSpec((ts, d), lambda t, *_: (t, 0)),
                pl.BlockSpec((1, 1, d), lambda t, *_: (layer, 0, 0)),
                pl.BlockSpec((1, 1, d), lambda t, *_: (layer, 0, 0)),
                pl.BlockSpec(memory_space=pl.ANY),
            ],
            out_specs=pl.BlockSpec((ts, d), lambda t, *_: (t, 0)),
            scratch_shapes=[pltpu.VMEM((2, rows, d // 2), U32), pltpu.SemaphoreType.DMA((2,))],
        ),
        out_shape=jax.ShapeDtypeStruct((n_tok, d), F32),
        compiler_params=pltpu.CompilerParams(
            dimension_semantics=("arbitrary",), vmem_limit_bytes=VMEM_LIMIT),
        name="combine",
    )(*tables, route_t, tables[0], x1, g2, b2, ys)


def kernel(x, mem, mem_ln_g, mem_ln_b, w_in, b_in, pool_w, pool_scale, pool_proj, conv_dw, conv_dw_b, conv_ln_g, conv_ln_b, conv_pw, conv_pw_b, w_kv, attn_o, w_out, b_out, ln1_g, ln1_b, router_w, router_b, exp_up, exp_up_b, exp_down, exp_down_b, ln2_g, ln2_b):
    n_b, s, d = x.shape
    depth = w_in.shape[0]
    n_e = router_w.shape[-1]
    n_tok = n_b * s
    n_t = n_tok // TOKEN_TILE
    alpha = (2.0 * depth) ** 0.25
    n_rows = TOP_K * n_tok + n_t * n_e * SEG_ALIGN + n_e * EXPERT_BLOCK
    n_rows = -(-n_rows // EXPERT_BLOCK) * EXPERT_BLOCK
    assert n_rows // EXPERT_BLOCK <= META_LANES and n_e <= LANES

    def rows3(a):
        return a.reshape(a.shape[0], 1, a.shape[1])

    in_cols = w_in.shape[-1]
    assert in_cols % PROJ_CHUNK == 0
    n_chunks = in_cols // PROJ_CHUNK
    router_pad = jnp.pad(router_w, ((0, 0), (0, 0), (0, LANES - n_e)))
    router_hi = router_pad.astype(BF16)
    weights = dict(
        w_in=w_in, b_in=b_in.reshape(depth, n_chunks, 1, PROJ_CHUNK), pool_w=pool_w,
        pool_scale=rows3(pool_scale), pool_proj=pool_proj, conv_dw=conv_dw,
        conv_dw_b=rows3(conv_dw_b), conv_ln_g=rows3(conv_ln_g), conv_ln_b=rows3(conv_ln_b),
        conv_pw=conv_pw, conv_pw_b=rows3(conv_pw_b), attn_o=attn_o,
        w_out=w_out, b_out=rows3(b_out), ln1_g=rows3(ln1_g), ln1_b=rows3(ln1_b),
        router_hi=router_hi, router_lo=(router_pad - router_hi.astype(F32)).astype(BF16),
        router_b=router_b.reshape(depth, n_e, 1))
    up_b = exp_up_b.reshape(depth, n_e, 1, exp_up_b.shape[-1])
    down_b = exp_down_b.reshape(depth, n_e, 1, d)
    g2, b2 = rows3(ln2_g), rows3(ln2_b)

    kt_all, v_all = _memory_kv(mem, mem_ln_g, mem_ln_b, w_kv)
    for layer in range(depth):
        x1, ids, route_t, seg_off, seg_len, dest, meta = _mixer(
            x, weights, kt_all, v_all, layer, alpha)
        x1 = x1.reshape(n_tok, d)
        tables = (seg_off, seg_len, dest)
        xs = _dispatch(tables, meta, ids, x1, n_rows, n_e)
        ys = _experts(meta, xs, exp_up, up_b, exp_down, down_b, layer)
        x = _combine(tables, route_t, x1, g2, b2, ys, layer, alpha, n_e).reshape(n_b, s, d)
    return x
```

```python
import functools

import jax
import jax.numpy as jnp
from jax import lax
from jax.experimental import pallas as pl
from jax.experimental.pallas import tpu as pltpu

F32 = jnp.float32
BF16 = jnp.bfloat16
I32 = jnp.int32

POOL_WINDOWS = (2, 4, 8, 16)
POOL_HALO = 16
CONV_HALO = 32
HEAD_DIM = 128
TOP_K = 4
SWIGLU_LIMIT = 7.0
SWIGLU_ALPHA = 1.702
LN_EPS = 1e-5

LANES = 128
SUBLANES = 8
SEG_ALIGN = SUBLANES
U32 = jnp.uint32
TOKEN_TILE = 256
EXPERT_BLOCK = 1024
EXPERT_PARTS = 4
FF_CHUNK = 256
CONV_ROWS = 64
PROJ_CHUNK = 256
LOOP_PROJ_CHUNKS = 5
SORT_ROWS = 128
PERM_ROWS = 256
META_LANES = 512
META_BLK_E, META_PAD_DST, META_PAD_LEN, META_N_USED, META_REAL_ROWS = 0, 1, 2, 3, 4
VMEM_LIMIT = 56 * 1024 * 1024


def _layer_norm(x, g, b):
    mu = jnp.mean(x, axis=-1, keepdims=True)
    xc = x - mu
    var = jnp.mean(xc * xc, axis=-1, keepdims=True)
    return xc * lax.rsqrt(var + LN_EPS) * g + b


def _dot(a, b):
    return jnp.dot(a, b, preferred_element_type=F32)


def _pack_pairs(v):
    n = v.shape[1] // 2
    lo = lax.bitcast_convert_type(v[:, :n], U32)
    hi = lax.bitcast_convert_type(v[:, n:], U32)
    return (hi & jnp.uint32(0xFFFF0000)) | (lo >> 16)


def _unpack_pairs(w):
    lo = lax.bitcast_convert_type(w << 16, F32)
    hi = lax.bitcast_convert_type(w & jnp.uint32(0xFFFF0000), F32)
    return jnp.concatenate([lo, hi], axis=1).astype(BF16)


def _dot_exact(a, b):
    return jnp.dot(a, b, preferred_element_type=F32, precision=lax.Precision.HIGHEST)


def _kv_kernel(mem_ref, g_ref, b_ref, wkv_ref, kt_ref, v_ref):
    a = kt_ref.shape[2]
    mem_n = _layer_norm(mem_ref[0], g_ref[...], b_ref[...]).astype(BF16)
    kv = _dot(mem_n, wkv_ref[0].astype(BF16))
    kt_ref[0, 0] = kv[:, :a].T.astype(BF16)
    v_ref[0, 0] = kv[:, a:].astype(BF16)


def _memory_kv(mem, g, b, w_kv):
    n_b, m, d = mem.shape
    n_l, _, a2 = w_kv.shape
    a = a2 // 2
    return pl.pallas_call(
        _kv_kernel,
        grid=(n_l, n_b),
        in_specs=[
            pl.BlockSpec((1, m, d), lambda l, bb: (bb, 0, 0)),
            pl.BlockSpec((1, d), lambda l, bb: (0, 0)),
            pl.BlockSpec((1, d), lambda l, bb: (0, 0)),
            pl.BlockSpec((1, d, a2), lambda l, bb: (l, 0, 0)),
        ],
        out_specs=[
            pl.BlockSpec((1, 1, a, m), lambda l, bb: (l, bb, 0, 0)),
            pl.BlockSpec((1, 1, m, a), lambda l, bb: (l, bb, 0, 0)),
        ],
        out_shape=[
            jax.ShapeDtypeStruct((n_l, n_b, a, m), BF16),
            jax.ShapeDtypeStruct((n_l, n_b, m, a), BF16),
        ],
        compiler_params=pltpu.CompilerParams(
            dimension_semantics=("arbitrary", "arbitrary"), vmem_limit_bytes=VMEM_LIMIT),
        name="memory_kv",
    )(mem, g.reshape(1, d), b.reshape(1, d), w_kv)


def _routing_tables(padded, prefix, n_e, seg_off_ref, seg_len_ref, dest_ref, meta_ref):
    row = lax.broadcasted_iota(I32, (LANES, LANES), 0)
    col = lax.broadcasted_iota(I32, (LANES, LANES), 1)
    seg_off = _dot_exact(padded, (row < col).astype(F32))
    tot = jnp.sum(padded, axis=0, keepdims=True)
    ptot = jnp.ceil(tot * (1.0 / EXPERT_BLOCK)) * EXPERT_BLOCK
    e_end = _dot_exact(jnp.broadcast_to(ptot, (SUBLANES, LANES)),
                       (row <= col).astype(F32))[0:1]
    e_start = e_end - ptot
    seg_off_ref[...] = seg_off.astype(I32)
    seg_len_ref[...] = padded.astype(I32)
    dest_ref[...] = (prefix + e_start).astype(I32)

    e_end_col = jnp.sum(jnp.where(row == col, jnp.broadcast_to(e_end, (LANES, LANES)), 0.0),
                        axis=1, keepdims=True)
    blk_row = (lax.broadcasted_iota(I32, (LANES, META_LANES), 1) * EXPERT_BLOCK).astype(F32)
    is_real = lax.broadcasted_iota(I32, (LANES, META_LANES), 0) < n_e
    blk_e = jnp.sum(jnp.where(is_real & (e_end_col <= blk_row), 1.0, 0.0), axis=0, keepdims=True)
    blk_e = jnp.minimum(blk_e, n_e - 1.0)
    lane = lax.broadcasted_iota(I32, (1, LANES), 1)
    n_used = jnp.sum(jnp.where(lane == n_e - 1, e_end, 0.0), axis=1, keepdims=True) * (
        1.0 / EXPERT_BLOCK)

    real_end_col = jnp.sum(jnp.where(row == col, jnp.broadcast_to(e_start + tot, (LANES, LANES)),
                                     0.0), axis=1, keepdims=True)
    of_blk = lax.broadcasted_iota(I32, (LANES, META_LANES), 0).astype(F32) == blk_e
    real_end = jnp.sum(jnp.where(of_blk, real_end_col, 0.0), axis=0, keepdims=True)
    real_rows = jnp.clip(real_end - blk_row[0:1], 0.0, 1.0 * EXPERT_BLOCK)

    def wide(r):
        return jnp.concatenate([r, jnp.zeros((1, META_LANES - LANES), F32)], axis=1)

    sub = lax.broadcasted_iota(I32, (SUBLANES, META_LANES), 0)
    meta = jnp.zeros((SUBLANES, META_LANES), F32)
    for r, val in ((META_BLK_E, blk_e), (META_PAD_DST, wide(e_start + tot)),
                   (META_PAD_LEN, wide(ptot - tot)),
                   (META_N_USED, jnp.broadcast_to(n_used, (1, META_LANES))),
                   (META_REAL_ROWS, real_rows)):
        meta = jnp.where(sub == r, jnp.broadcast_to(val, (SUBLANES, META_LANES)), meta)
    meta_ref[...] = meta.astype(I32)


def _load_matrices_bf16(pieces, stage, sem):
    def fetch(p):
        src, s_idx, _, _, (r, c) = pieces[p]
        return pltpu.make_async_copy(src.at[s_idx], stage.at[p % 2, 0:r, 0:c], sem.at[p % 2])

    fetch(0).start()
    for p, (_, _, dst, d_idx, (r, c)) in enumerate(pieces):
        if p + 1 < len(pieces):
            fetch(p + 1).start()
        fetch(p).wait()
        dst[d_idx] = stage[p % 2, 0:r, 0:c].astype(BF16)


def _mixer_kernel(x_ref, w_in_hbm, b_in_ref, pool_w_hbm, pool_scale_ref, pool_proj_hbm,
                  dw_ref, dwb_ref, cg_ref, cb_ref, cpw_hbm, cpwb_ref,
                  kt_ref, v_ref, ao_hbm, wout_hbm, bout_ref, g1_ref, b1_ref,
                  wr_hi_ref, wr_lo_ref, rb_ref,
                  x1_ref, ids_ref, route_t_ref, seg_off_ref, seg_len_ref, dest_ref, meta_ref,
                  w_in_ref, pool_w_ref, pool_proj_ref, cpw_ref, ao_ref, wout_ref, stage, stage_sem,
                  h_scr, x_scr, xb_scr, pool_ext, conv_ext, conv_out,
                  cnt_scr, pre_scr, run_scr,
                  *, alpha, n_s, layer):
    s = pl.program_id(0)
    n_tiles = pl.num_programs(0) - 1
    cur = (s + 1) % 2
    nxt = s % 2
    tile = jnp.maximum(s - 1, 0)
    i = tile % n_s
    ts, d = x_ref.shape[1], x_ref.shape[2]
    n_chunks = w_in_ref.shape[0]
    pw = pool_ext.shape[1]
    cw = conv_ext.shape[1]
    aw = kt_ref.shape[2]
    n_e = rb_ref.shape[1]
    c_conv = pw
    c_q = pw + 2 * cw
    c_gate = c_q + aw

    @pl.when(i == 0)
    def _():
        pool_ext[0:POOL_HALO, :] = jnp.zeros((POOL_HALO, pw), F32)
        conv_ext[0:CONV_HALO, :] = jnp.zeros((CONV_HALO, cw), F32)

    @pl.when(s == 0)
    def _():
        run_scr[...] = jnp.zeros(run_scr.shape, F32)
        h_scr[1] = jnp.zeros(h_scr.shape[1:], F32)
        x_scr[1] = jnp.zeros(x_scr.shape[1:], F32)
        pieces = []

        def add(src, dst, rows, cols, dst_chunked=False):
            for j in range(cols // PROJ_CHUNK):
                c0 = j * PROJ_CHUNK
                d_idx = (j,) if dst_chunked else (slice(None), slice(c0, c0 + PROJ_CHUNK))
                pieces.append((src, (layer, slice(None), slice(c0, c0 + PROJ_CHUNK)),
                               dst, d_idx, (rows, PROJ_CHUNK)))

        add(w_in_hbm, w_in_ref, d, n_chunks * PROJ_CHUNK, dst_chunked=True)
        add(pool_proj_hbm, pool_proj_ref, pw, d)
        add(cpw_hbm, cpw_ref, cw, d)
        add(ao_hbm, ao_ref, aw, d)
        add(wout_hbm, wout_ref, d, d)
        gd_ = pw // len(POOL_WINDOWS)
        for g in range(len(POOL_WINDOWS)):
            pieces.append((pool_w_hbm, (layer, g), pool_w_ref, (g,), (gd_, gd_)))
        _load_matrices_bf16(pieces, stage, stage_sem)

    x_in = x_ref[0]
    x_scr[nxt] = x_in
    xb_scr[...] = x_in.astype(BF16)

    x = x_scr[cur]

    def proj(lo, hi):
        return jnp.concatenate(
            [h_scr[cur, c] for c in range(lo // PROJ_CHUNK, hi // PROJ_CHUNK)], axis=1)

    glu = proj(c_conv, c_conv + cw) * jax.nn.sigmoid(proj(c_conv + cw, c_conv + 2 * cw))
    conv_ext[CONV_HALO:CONV_HALO + ts, :] = glu
    taps = dw_ref.shape[1]
    base = CONV_HALO - (taps - 1)
    span = CONV_ROWS + CONV_HALO
    n_row_chunks = ts // CONV_ROWS
    chunks_per_iter = min(LOOP_PROJ_CHUNKS, n_chunks // n_row_chunks)
    later_chunks = list(range(n_row_chunks * chunks_per_iter, n_chunks))

    def project(c):
        h_scr[nxt, c] = _dot(xb_scr[...], w_in_ref[c]) + b_in_ref[0, c]

    def project_some(n):
        for _ in range(min(n, len(later_chunks))):
            project(later_chunks.pop(0))

    def conv_rows_and_projection(j):
        r0 = j * CONV_ROWS
        for c0 in range(0, cw, LANES):
            window = conv_ext[pl.ds(r0, span), c0:c0 + LANES]
            acc = jnp.broadcast_to(dwb_ref[0, :, c0:c0 + LANES], (CONV_ROWS, LANES))
            for res in range(SUBLANES):
                offs = [o for o in range(base, base + taps) if o % SUBLANES == res]
                if not offs:
                    continue
                shifted = window if res == 0 else pltpu.roll(window, span - res, axis=0)
                for o in offs:
                    k = o - base
                    acc = acc + dw_ref[0, k:k + 1, c0:c0 + LANES] * shifted[
                        o - res:o - res + CONV_ROWS]
            conv_out[pl.ds(r0, CONV_ROWS), c0:c0 + LANES] = acc
        for m in range(chunks_per_iter):
            project(j * chunks_per_iter + m)

    for j in range(n_row_chunks):
        conv_rows_and_projection(j)
    conv_ext[0:CONV_HALO, :] = conv_ext[ts:ts + CONV_HALO, :]
    hc = _layer_norm(conv_out[...], cg_ref[0], cb_ref[0])
    hc = hc * jax.nn.sigmoid(hc)
    project_some(2)
    y_conv = _dot(hc.astype(BF16), cpw_ref[...]) + cpwb_ref[0]

    u = proj(0, pw)
    pool_ext[POOL_HALO:POOL_HALO + ts, :] = u
    t_glob = i * ts + lax.broadcasted_iota(I32, (ts, 1), 0)
    gd = pw // len(POOL_WINDOWS)
    pooled = []
    for g, w in enumerate(POOL_WINDOWS):
        lo = g * gd
        ug = u[:, lo:lo + gd]
        run = pool_ext[:, lo:lo + gd]
        step = 1
        while step < w:
            run = run + pltpu.roll(run, step, axis=0)
            step *= 2
        cnt = jnp.minimum(t_glob + 1, w).astype(F32)
        pooled.append((run[POOL_HALO:POOL_HALO + ts] / cnt - ug).astype(BF16))
    project_some(2)
    mixed = [_dot(p, pool_w_ref[g]) for g, p in enumerate(pooled)]
    mixed = jnp.concatenate(mixed, axis=1) * pool_scale_ref[0]
    y_pool = _dot(mixed.astype(BF16), pool_proj_ref[...])
    pool_ext[0:POOL_HALO, :] = pool_ext[ts:ts + POOL_HALO, :]

    q = proj(c_q, c_q + aw)
    heads = []
    for h in range(aw // HEAD_DIM):
        lo = h * HEAD_DIM
        sc = _dot(q[:, lo:lo + HEAD_DIM].astype(BF16), kt_ref[0, 0, lo:lo + HEAD_DIM, :])
        sc = sc * (HEAD_DIM ** -0.5)
        p = jnp.exp(sc - jnp.max(sc, axis=-1, keepdims=True))
        o = _dot(p.astype(BF16), v_ref[0, 0, :, lo:lo + HEAD_DIM])
        heads.append(o / jnp.sum(p, axis=-1, keepdims=True))
    y_attn = _dot(jnp.concatenate(heads, axis=1).astype(BF16), ao_ref[...])

    merged = jax.nn.sigmoid(proj(c_gate, c_gate + d)) * y_pool
    merged = merged + jax.nn.sigmoid(proj(c_gate + d, c_gate + 2 * d)) * y_conv
    merged = merged + jax.nn.sigmoid(proj(c_gate + 2 * d, c_gate + 3 * d)) * y_attn
    project_some(2)
    out = _dot(merged.astype(BF16), wout_ref[...]) + bout_ref[0]
    x1 = _layer_norm(alpha * x + out, g1_ref[0], b1_ref[0])
    x1_ref[0] = x1

    x1_hi = x1.astype(BF16)
    x1_lo = (x1 - x1_hi.astype(F32)).astype(BF16)
    logits = _dot(x1_hi, wr_hi_ref[0]) + _dot(x1_lo, wr_hi_ref[0]) + _dot(x1_hi, wr_lo_ref[0])
    logits = logits.T[0:n_e, :] + rb_ref[0]
    e_iota = lax.broadcasted_iota(I32, (n_e, ts), 0)
    rank = jnp.zeros((n_e, ts), F32)
    for other in range(n_e):
        row = logits[other:other + 1, :]
        beats = (row > logits) | ((row == logits) & (e_iota > other))
        rank = rank + jnp.where(beats, 1.0, 0.0)
    hot = jnp.where(rank < TOP_K, 1.0, 0.0)
    top_v, top_i = [], []
    for k in range(TOP_K):
        sel = rank == k
        top_i.append(jnp.sum(jnp.where(sel, e_iota, 0), axis=0, keepdims=True))
        top_v.append(jnp.sum(jnp.where(sel, logits, 0.0), axis=0, keepdims=True))
    ex = [jnp.exp(v - top_v[0]) for v in top_v]
    den = ex[0] + ex[1] + ex[2] + ex[3]
    sub = lax.broadcasted_iota(I32, (SUBLANES, ts), 0)
    route = jnp.zeros((SUBLANES, ts), F32)
    for k in range(TOP_K):
        ids_ref[k:k + 1, :] = top_i[k]
        route = jnp.where(sub == k, top_i[k].astype(F32), route)
        route = jnp.where(sub == TOP_K + k, ex[k] / den, route)
    route_t_ref[...] = jnp.concatenate(
        [route, jnp.zeros((LANES - SUBLANES, ts), F32)], axis=0).T

    hot_wide = jnp.concatenate([hot, jnp.zeros((LANES - n_e, ts), F32)], axis=0).astype(BF16)
    cnt = lax.dot_general(jnp.ones((SUBLANES, ts), BF16), hot_wide, (((1,), (1,)), ((), ())),
                          preferred_element_type=F32)[0:1]
    is_expert = lax.broadcasted_iota(I32, (1, LANES), 1) < n_e
    padded = jnp.where(is_expert, jnp.maximum(jnp.ceil(cnt * (1.0 / SEG_ALIGN)), 1.0), 0.0)
    padded = padded * SEG_ALIGN
    padded = jnp.where(s >= 1, padded, 0.0)
    cnt_scr[pl.ds(tile, 1), :] = padded
    pre_scr[pl.ds(tile, 1), :] = run_scr[0:1, :]
    run_scr[0:1, :] = run_scr[0:1, :] + padded

    project_some(len(later_chunks))

    @pl.when(s == n_tiles)
    def _():
        _routing_tables(cnt_scr[...], pre_scr[...], n_e,
                        seg_off_ref, seg_len_ref, dest_ref, meta_ref)


def _mixer(x, weights, kt, v, layer, alpha):
    n_b, s, d = x.shape
    ts = TOKEN_TILE
    n_s = s // ts
    n_t = n_b * n_s
    names = ("w_in", "b_in", "pool_w", "pool_scale", "pool_proj", "conv_dw", "conv_dw_b",
             "conv_ln_g", "conv_ln_b", "conv_pw", "conv_pw_b")
    names2 = ("attn_o", "w_out", "b_out", "ln1_g", "ln1_b", "router_hi", "router_lo", "router_b")
    in_hbm = ("w_in", "pool_w", "pool_proj", "conv_pw", "attn_o", "w_out")
    pw = weights["pool_proj"].shape[1]
    cw = weights["conv_pw"].shape[1]
    aw = weights["attn_o"].shape[1]
    n_chunks = weights["w_in"].shape[2] // PROJ_CHUNK
    gd = pw // len(POOL_WINDOWS)

    def of_layer(arr):
        tail = (0,) * (arr.ndim - 1)
        return pl.BlockSpec((1,) + arr.shape[1:], lambda st: (layer,) + tail,
                            pipeline_mode=pl.Buffered(1))

    def spec(name):
        if name in in_hbm:
            return pl.BlockSpec(memory_space=pl.ANY)
        return of_layer(weights[name])

    def resident(shape):
        return pl.BlockSpec(shape, lambda st: (0, 0))

    def tile_a(st):
        return jnp.minimum(st, n_t - 1)

    def tile_b(st):
        return jnp.maximum(st - 1, 0)

    in_specs = ([pl.BlockSpec((1, ts, d), lambda st: (tile_a(st) // n_s, tile_a(st) % n_s, 0))]
                + [spec(k) for k in names]
                + [pl.BlockSpec((1, 1) + kt.shape[2:],
                                lambda st: (layer, tile_b(st) // n_s, 0, 0)),
                   pl.BlockSpec((1, 1) + v.shape[2:],
                                lambda st: (layer, tile_b(st) // n_s, 0, 0))]
                + [spec(k) for k in names2])
    out_specs = [
        pl.BlockSpec((1, ts, d), lambda st: (tile_b(st) // n_s, tile_b(st) % n_s, 0)),
        pl.BlockSpec((TOP_K, ts), lambda st: (0, tile_b(st))),
        pl.BlockSpec((ts, LANES), lambda st: (tile_b(st), 0)),
        resident((n_t, LANES)), resident((n_t, LANES)), resident((n_t, LANES)),
        resident((SUBLANES, META_LANES)),
    ]
    out_shape = [
        jax.ShapeDtypeStruct((n_b, s, d), F32),
        jax.ShapeDtypeStruct((TOP_K, n_b * s), I32),
        jax.ShapeDtypeStruct((n_b * s, LANES), F32),
        jax.ShapeDtypeStruct((n_t, LANES), I32),
        jax.ShapeDtypeStruct((n_t, LANES), I32),
        jax.ShapeDtypeStruct((n_t, LANES), I32),
        jax.ShapeDtypeStruct((SUBLANES, META_LANES), I32),
    ]
    return pl.pallas_call(
        functools.partial(_mixer_kernel, alpha=alpha, n_s=n_s, layer=layer),
        grid=(n_t + 1,),
        in_specs=in_specs,
        out_specs=out_specs,
        out_shape=out_shape,
        scratch_shapes=[
            pltpu.VMEM((n_chunks, d, PROJ_CHUNK), BF16),
            pltpu.VMEM((len(POOL_WINDOWS), gd, gd), BF16),
            pltpu.VMEM((pw, d), BF16),
            pltpu.VMEM((cw, d), BF16),
            pltpu.VMEM((aw, d), BF16),
            pltpu.VMEM((d, d), BF16),
            pltpu.VMEM((2, max(d, pw, cw, aw), PROJ_CHUNK), F32),
            pltpu.SemaphoreType.DMA((2,)),
            pltpu.VMEM((2, n_chunks, ts, PROJ_CHUNK), F32),
            pltpu.VMEM((2, ts, d), F32),
            pltpu.VMEM((ts, d), BF16),
            pltpu.VMEM((ts + POOL_HALO, pw), F32),
            pltpu.VMEM((ts + CONV_HALO, cw), F32),
            pltpu.VMEM((ts, cw), F32),
            pltpu.VMEM((n_t, LANES), F32),
            pltpu.VMEM((n_t, LANES), F32),
            pltpu.VMEM((SUBLANES, LANES), F32),
        ],
        compiler_params=pltpu.CompilerParams(
            dimension_semantics=("arbitrary",), vmem_limit_bytes=VMEM_LIMIT),
        name="mixer",
    )(x, *[weights[k] for k in names], kt, v, *[weights[k] for k in names2])


def _start_segment_copies(seg_off_ref, seg_len_ref, dest_ref, tile, n_e, make_copy):
    for e in range(n_e):
        make_copy(pl.multiple_of(seg_off_ref[tile, e], SEG_ALIGN),
                  pl.multiple_of(dest_ref[tile, e], SEG_ALIGN),
                  pl.multiple_of(seg_len_ref[tile, e], SEG_ALIGN)).start()


def _wait_segment_copies(seg_off_ref, seg_len_ref, tile, n_e, make_copy):
    total = seg_off_ref[tile, n_e - 1] + seg_len_ref[tile, n_e - 1]
    make_copy(0, 0, pl.multiple_of(total, SEG_ALIGN)).wait()


def _dispatch_kernel(seg_off_ref, seg_len_ref, dest_ref, meta_ref,
                     ids_ref, off_ref, x1_ref, xs_ref, buf, zbuf, sem, zsem, *, n_e):
    tile = pl.program_id(0)
    n_tiles = pl.num_programs(0)
    slot = tile % 2
    ts = x1_ref.shape[0]
    rows = buf.shape[1]

    @pl.when(tile == 0)
    def _():
        zbuf[...] = jnp.zeros(zbuf.shape, U32)

        def zero_copy(e):
            n = pl.multiple_of(meta_ref[META_PAD_LEN, e], SEG_ALIGN)
            dst = pl.multiple_of(meta_ref[META_PAD_DST, e], SEG_ALIGN)
            return n, pltpu.make_async_copy(zbuf.at[pl.ds(0, n)], xs_ref.at[pl.ds(dst, n)], zsem)

        def zstart(e, c):
            n, cp = zero_copy(e)
            pl.when(n > 0)(cp.start)
            return c

        def zwait(e, c):
            n, cp = zero_copy(e)
            pl.when(n > 0)(cp.wait)
            return c

        lax.fori_loop(0, n_e, zstart, 0)
        lax.fori_loop(0, n_e, zwait, 0)

    def copies_from(s):
        def make_copy(off, dst, n):
            return pltpu.make_async_copy(buf.at[s, pl.ds(off, n)], xs_ref.at[pl.ds(dst, n)],
                                         sem.at[s])
        return make_copy

    @pl.when(tile >= 2)
    def _():
        _wait_segment_copies(seg_off_ref, seg_len_ref, tile - 2, n_e, copies_from(slot))

    ids = ids_ref[...]
    e_iota = lax.broadcasted_iota(I32, (n_e, ts), 0)
    hots = [(e_iota == ids[k:k + 1, :]).astype(F32) for k in range(TOP_K)]
    hot = hots[0] + hots[1] + hots[2] + hots[3]
    off_row = off_ref[pl.ds(tile, 1), :].astype(F32)
    eye = (lax.broadcasted_iota(I32, (n_e, LANES), 0)
           == lax.broadcasted_iota(I32, (n_e, LANES), 1))
    off_col = jnp.sum(jnp.where(eye, off_row, 0.0), axis=1, keepdims=True)
    upper = (lax.broadcasted_iota(I32, (ts, ts), 0)
             < lax.broadcasted_iota(I32, (ts, ts), 1)).astype(BF16)
    place = _dot(hot.astype(BF16), upper) + off_col
    pos = [jnp.sum(hots[k] * place, axis=0, keepdims=True).astype(I32)
           for k in range(TOP_K)]
    x1b = x1_ref[...].astype(BF16)
    for r0 in range(0, rows, SORT_ROWS):
        r_iota = r0 + lax.broadcasted_iota(I32, (SORT_ROWS, ts), 0)
        hit = (r_iota == pos[0]) | (r_iota == pos[1]) | (r_iota == pos[2]) | (r_iota == pos[3])
        perm = jnp.where(hit, 1.0, 0.0).astype(BF16)
        buf[slot, r0:r0 + SORT_ROWS] = _pack_pairs(_dot(perm, x1b))
    _start_segment_copies(seg_off_ref, seg_len_ref, dest_ref, tile, n_e, copies_from(slot))

    @pl.when(tile == n_tiles - 1)
    def _():
        @pl.when(tile >= 1)
        def _():
            _wait_segment_copies(seg_off_ref, seg_len_ref, tile - 1, n_e, copies_from(1 - slot))

        _wait_segment_copies(seg_off_ref, seg_len_ref, tile, n_e, copies_from(slot))


def _tile_rows(ts, n_e):
    return TOP_K * ts + n_e * SEG_ALIGN


def _dispatch(tables, meta, ids, x1, n_rows, n_e):
    n_tok, d = x1.shape
    ts = TOKEN_TILE
    n_t = n_tok // ts
    rows = _tile_rows(ts, n_e)
    return pl.pallas_call(
        functools.partial(_dispatch_kernel, n_e=n_e),
        grid_spec=pltpu.PrefetchScalarGridSpec(
            num_scalar_prefetch=4,
            grid=(n_t,),
            in_specs=[
                pl.BlockSpec((TOP_K, ts), lambda t, *_: (0, t)),
                pl.BlockSpec((n_t, LANES), lambda t, *_: (0, 0)),
                pl.BlockSpec((ts, d), lambda t, *_: (t, 0)),
            ],
            out_specs=pl.BlockSpec(memory_space=pl.ANY),
            scratch_shapes=[pltpu.VMEM((2, rows, d // 2), U32),
                            pltpu.VMEM((EXPERT_BLOCK, d // 2), U32),
                            pltpu.SemaphoreType.DMA((2,)), pltpu.SemaphoreType.DMA(())],
        ),
        out_shape=jax.ShapeDtypeStruct((n_rows, d // 2), U32),
        compiler_params=pltpu.CompilerParams(
            dimension_semantics=("arbitrary",), vmem_limit_bytes=VMEM_LIMIT),
        name="dispatch",
    )(*tables, meta, ids, tables[0], x1)


def _expert_kernel(meta_ref, xs_ref, wup_ref, bup_ref, wdn_ref, bdn_ref,
                   ys_ref, wup_bf, wdn_bf):
    j = pl.program_id(0)
    f = wdn_ref.shape[2]

    @pl.when(j < meta_ref[META_N_USED, 0])
    def _():
        prev = meta_ref[META_BLK_E, jnp.maximum(j - 1, 0)]

        @pl.when((j == 0) | (meta_ref[META_BLK_E, j] != prev))
        def _():
            wup_bf[...] = wup_ref[0, 0].astype(BF16)
            wdn_bf[...] = wdn_ref[0, 0].astype(BF16)

        def expert_rows(n_rows):
            x = _unpack_pairs(xs_ref[0:n_rows, :])

            def up(c0):
                glu = _dot(x, wup_bf[:, c0:c0 + FF_CHUNK]) + bup_ref[0, 0, :, c0:c0 + FF_CHUNK]
                lin = (_dot(x, wup_bf[:, f + c0:f + c0 + FF_CHUNK])
                       + bup_ref[0, 0, :, f + c0:f + c0 + FF_CHUNK])
                return glu, lin

            y = jnp.broadcast_to(bdn_ref[0, 0], (n_rows, wdn_ref.shape[3]))
            nxt = up(0)
            for c0 in range(0, f, FF_CHUNK):
                glu, lin = nxt
                if c0 + FF_CHUNK < f:
                    nxt = up(c0 + FF_CHUNK)
                glu = jnp.minimum(glu, SWIGLU_LIMIT)
                lin = jnp.clip(lin, -SWIGLU_LIMIT, SWIGLU_LIMIT)
                act = glu * jax.nn.sigmoid(SWIGLU_ALPHA * glu) * (lin + 1.0)
                y = y + _dot(act.astype(BF16), wdn_bf[c0:c0 + FF_CHUNK, :])
            ys_ref[0:n_rows, :] = _pack_pairs(y.astype(BF16).astype(F32))

        bm = xs_ref.shape[0]
        part = bm // EXPERT_PARTS
        real_rows = meta_ref[META_REAL_ROWS, j]
        for q in range(1, EXPERT_PARTS + 1):
            pl.when((real_rows > (q - 1) * part) & (real_rows <= q * part))(
                functools.partial(expert_rows, q * part))


def _experts(meta, xs, w_up, b_up, w_down, b_down, layer):
    n_rows, half_d = xs.shape
    d = 2 * half_d
    f2 = w_up.shape[-1]
    f = f2 // 2
    bm = EXPERT_BLOCK
    n_blk = n_rows // bm

    def last_used(j, meta):
        return jnp.maximum(jnp.minimum(j, meta[META_N_USED, 0] - 1), 0)

    def row_blk(j, meta):
        return (last_used(j, meta), 0)

    def w_blk(j, meta):
        return (layer, meta[META_BLK_E, last_used(j, meta)], 0, 0)

    return pl.pallas_call(
        _expert_kernel,
        grid_spec=pltpu.PrefetchScalarGridSpec(
            num_scalar_prefetch=1,
            grid=(n_blk,),
            in_specs=[
                pl.BlockSpec((bm, half_d), row_blk),
                pl.BlockSpec((1, 1, d, f2), w_blk),
                pl.BlockSpec((1, 1, 1, f2), w_blk),
                pl.BlockSpec((1, 1, f, d), w_blk),
                pl.BlockSpec((1, 1, 1, d), w_blk),
            ],
            out_specs=pl.BlockSpec((bm, half_d), row_blk),
            scratch_shapes=[pltpu.VMEM((d, f2), BF16), pltpu.VMEM((f, d), BF16)],
        ),
        out_shape=jax.ShapeDtypeStruct((n_rows, half_d), U32),
        compiler_params=pltpu.CompilerParams(
            dimension_semantics=("arbitrary",), vmem_limit_bytes=VMEM_LIMIT),
        name="experts",
    )(meta, xs, w_up, b_up, w_down, b_down)


def _combine_kernel(seg_off_ref, seg_len_ref, dest_ref,
                    route_ref, off_ref, x1_ref, g2_ref, b2_ref, ys_ref,
                    out_ref, buf, sem, *, alpha, n_e):
    tile = pl.program_id(0)
    n_tiles = pl.num_programs(0)
    slot = tile % 2
    ts = x1_ref.shape[0]
    rows = buf.shape[1]

    def copies_into(s):
        def make_copy(off, dst, n):
            return pltpu.make_async_copy(ys_ref.at[pl.ds(dst, n)], buf.at[s, pl.ds(off, n)],
                                         sem.at[s])
        return make_copy

    @pl.when(tile == 0)
    def _():
        buf[...] = jnp.zeros(buf.shape, U32)
        _start_segment_copies(seg_off_ref, seg_len_ref, dest_ref, tile, n_e, copies_into(slot))

    nxt = jnp.minimum(tile + 1, n_tiles - 1)
    _start_segment_copies(seg_off_ref, seg_len_ref, dest_ref, nxt, n_e, copies_into(1 - slot))

    route = route_ref[...]
    l_iota = lax.broadcasted_iota(I32, (ts, LANES), 1)
    hots = [(l_iota == route[:, k:k + 1].astype(I32)).astype(F32) for k in range(TOP_K)]
    hot = hots[0] + hots[1] + hots[2] + hots[3]
    lower = (lax.broadcasted_iota(I32, (ts, ts), 1)
             < lax.broadcasted_iota(I32, (ts, ts), 0)).astype(BF16)
    place = _dot(lower, hot.astype(BF16)) + off_ref[pl.ds(tile, 1), :].astype(F32)
    pos = [jnp.sum(hots[k] * place, axis=1, keepdims=True).astype(I32)
           for k in range(TOP_K)]

    _wait_segment_copies(seg_off_ref, seg_len_ref, tile, n_e, copies_into(slot))
    y = jnp.zeros(out_ref.shape, F32)
    for r0 in range(0, rows, PERM_ROWS):
        r_iota = r0 + lax.broadcasted_iota(I32, (ts, PERM_ROWS), 1)
        weight = jnp.zeros((ts, PERM_ROWS), F32)
        for k in range(TOP_K):
            weight = jnp.where(r_iota == pos[k], route[:, TOP_K + k:TOP_K + k + 1], weight)
        y = y + _dot(weight.astype(BF16), _unpack_pairs(buf[slot, r0:r0 + PERM_ROWS]))
    out_ref[...] = _layer_norm(alpha * x1_ref[...] + y, g2_ref[0], b2_ref[0])

    @pl.when(tile == n_tiles - 1)
    def _():
        _wait_segment_copies(seg_off_ref, seg_len_ref, tile, n_e, copies_into(1 - slot))


def _combine(tables, route_t, x1, g2, b2, ys, layer, alpha, n_e):
    n_tok, d = x1.shape
    ts = TOKEN_TILE
    n_t = n_tok // ts
    rows = _tile_rows(ts, n_e)
    return pl.pallas_call(
        functools.partial(_combine_kernel, alpha=alpha, n_e=n_e),
        grid_spec=pltpu.PrefetchScalarGridSpec(
            num_scalar_prefetch=3,
            grid=(n_t,),
            in_specs=[
                pl.BlockSpec((ts, LANES), lambda t, *_: (t, 0)),
                pl.BlockSpec((n_t, LANES), lambda t, *_: (0, 0)),
                pl.BlockSpec((ts, d), lambda t, *_: (t, 0)),
                pl.BlockSpec((1, 1, d), lambda t, *_: (layer, 0, 0)),
                pl.BlockSpec((1, 1, d), lambda t, *_: (layer, 0, 0)),
                pl.BlockSpec(memory_space=pl.ANY),
            ],
            out_specs=pl.BlockSpec((ts, d), lambda t, *_: (t, 0)),
            scratch_shapes=[pltpu.VMEM((2, rows, d // 2), U32), pltpu.SemaphoreType.DMA((2,))],
        ),
        out_shape=jax.ShapeDtypeStruct((n_tok, d), F32),
        compiler_params=pltpu.CompilerParams(
            dimension_semantics=("arbitrary",), vmem_limit_bytes=VMEM_LIMIT),
        name="combine",
    )(*tables, route_t, tables[0], x1, g2, b2, ys)


def kernel(x, mem, mem_ln_g, mem_ln_b, w_in, b_in, pool_w, pool_scale, pool_proj, conv_dw, conv_dw_b, conv_ln_g, conv_ln_b, conv_pw, conv_pw_b, w_kv, attn_o, w_out, b_out, ln1_g, ln1_b, router_w, router_b, exp_up, exp_up_b, exp_down, exp_down_b, ln2_g, ln2_b):
    n_b, s, d = x.shape
    depth = w_in.shape[0]
    n_e = router_w.shape[-1]
    n_tok = n_b * s
    n_t = n_tok // TOKEN_TILE
    alpha = (2.0 * depth) ** 0.25
    n_rows = TOP_K * n_tok + n_t * n_e * SEG_ALIGN + n_e * EXPERT_BLOCK
    n_rows = -(-n_rows // EXPERT_BLOCK) * EXPERT_BLOCK
    assert n_rows // EXPERT_BLOCK <= META_LANES and n_e <= LANES

    def rows3(a):
        return a.reshape(a.shape[0], 1, a.shape[1])

    in_cols = w_in.shape[-1]
    assert in_cols % PROJ_CHUNK == 0
    n_chunks = in_cols // PROJ_CHUNK
    router_pad = jnp.pad(router_w, ((0, 0), (0, 0), (0, LANES - n_e)))
    router_hi = router_pad.astype(BF16)
    weights = dict(
        w_in=w_in, b_in=b_in.reshape(depth, n_chunks, 1, PROJ_CHUNK), pool_w=pool_w,
        pool_scale=rows3(pool_scale), pool_proj=pool_proj, conv_dw=conv_dw,
        conv_dw_b=rows3(conv_dw_b), conv_ln_g=rows3(conv_ln_g), conv_ln_b=rows3(conv_ln_b),
        conv_pw=conv_pw, conv_pw_b=rows3(conv_pw_b), attn_o=attn_o,
        w_out=w_out, b_out=rows3(b_out), ln1_g=rows3(ln1_g), ln1_b=rows3(ln1_b),
        router_hi=router_hi, router_lo=(router_pad - router_hi.astype(F32)).astype(BF16),
        router_b=router_b.reshape(depth, n_e, 1))
    up_b = exp_up_b.reshape(depth, n_e, 1, exp_up_b.shape[-1])
    down_b = exp_down_b.reshape(depth, n_e, 1, d)
    g2, b2 = rows3(ln2_g), rows3(ln2_b)

    kt_all, v_all = _memory_kv(mem, mem_ln_g, mem_ln_b, w_kv)
    for layer in range(depth):
        x1, ids, route_t, seg_off, seg_len, dest, meta = _mixer(
            x, weights, kt_all, v_all, layer, alpha)
        x1 = x1.reshape(n_tok, d)
        tables = (seg_off, seg_len, dest)
        xs = _dispatch(tables, meta, ids, x1, n_rows, n_e)
        ys = _experts(meta, xs, exp_up, up_b, exp_down, down_b, layer)
        x = _combine(tables, route_t, x1, g2, b2, ys, layer, alpha, n_e).reshape(n_b, s, d)
    return x
```

```python
import functools

import jax
import jax.numpy as jnp
from jax import lax
from jax.experimental import pallas as pl
from jax.experimental.pallas import tpu as pltpu

F32 = jnp.float32
BF16 = jnp.bfloat16
I32 = jnp.int32

POOL_WINDOWS = (2, 4, 8, 16)
POOL_HALO = 16
CONV_HALO = 32
HEAD_DIM = 128
TOP_K = 4
SWIGLU_LIMIT = 7.0
SWIGLU_ALPHA = 1.702
LN_EPS = 1e-5

LANES = 128
SUBLANES = 8
SEG_ALIGN = SUBLANES
U32 = jnp.uint32
TOKEN_TILE = 256
EXPERT_BLOCK = 1024
EXPERT_PARTS = 8
FF_CHUNK = 256
CONV_ROWS = 64
PROJ_CHUNK = 256
LOOP_PROJ_CHUNKS = 5
SORT_ROWS = 128
PERM_ROWS = 256
META_LANES = 512
META_BLK_E, META_PAD_DST, META_PAD_LEN, META_N_USED, META_REAL_ROWS = 0, 1, 2, 3, 4
VMEM_LIMIT = 56 * 1024 * 1024


def _layer_norm(x, g, b):
    mu = jnp.mean(x, axis=-1, keepdims=True)
    xc = x - mu
    var = jnp.mean(xc * xc, axis=-1, keepdims=True)
    return xc * lax.rsqrt(var + LN_EPS) * g + b


def _dot(a, b):
    return jnp.dot(a, b, preferred_element_type=F32)


def _pack_pairs(v):
    n = v.shape[1] // 2
    lo = lax.bitcast_convert_type(v[:, :n], U32)
    hi = lax.bitcast_convert_type(v[:, n:], U32)
    return (hi & jnp.uint32(0xFFFF0000)) | (lo >> 16)


def _unpack_pairs(w):
    lo = lax.bitcast_convert_type(w << 16, F32)
    hi = lax.bitcast_convert_type(w & jnp.uint32(0xFFFF0000), F32)
    return jnp.concatenate([lo, hi], axis=1).astype(BF16)


def _dot_exact(a, b):
    return jnp.dot(a, b, preferred_element_type=F32, precision=lax.Precision.HIGHEST)


def _kv_kernel(mem_ref, g_ref, b_ref, wkv_ref, kt_ref, v_ref):
    a = kt_ref.shape[2]
    mem_n = _layer_norm(mem_ref[0], g_ref[...], b_ref[...]).astype(BF16)
    kv = _dot(mem_n, wkv_ref[0].astype(BF16))
    kt_ref[0, 0] = kv[:, :a].T.astype(BF16)
    v_ref[0, 0] = kv[:, a:].astype(BF16)


def _memory_kv(mem, g, b, w_kv):
    n_b, m, d = mem.shape
    n_l, _, a2 = w_kv.shape
    a = a2 // 2
    return pl.pallas_call(
        _kv_kernel,
        grid=(n_l, n_b),
        in_specs=[
            pl.BlockSpec((1, m, d), lambda l, bb: (bb, 0, 0)),
            pl.BlockSpec((1, d), lambda l, bb: (0, 0)),
            pl.BlockSpec((1, d), lambda l, bb: (0, 0)),
            pl.BlockSpec((1, d, a2), lambda l, bb: (l, 0, 0)),
        ],
        out_specs=[
            pl.BlockSpec((1, 1, a, m), lambda l, bb: (l, bb, 0, 0)),
            pl.BlockSpec((1, 1, m, a), lambda l, bb: (l, bb, 0, 0)),
        ],
        out_shape=[
            jax.ShapeDtypeStruct((n_l, n_b, a, m), BF16),
            jax.ShapeDtypeStruct((n_l, n_b, m, a), BF16),
        ],
        compiler_params=pltpu.CompilerParams(
            dimension_semantics=("arbitrary", "arbitrary"), vmem_limit_bytes=VMEM_LIMIT),
        name="memory_kv",
    )(mem, g.reshape(1, d), b.reshape(1, d), w_kv)


def _routing_tables(padded, prefix, n_e, seg_off_ref, seg_len_ref, dest_ref, meta_ref):
    row = lax.broadcasted_iota(I32, (LANES, LANES), 0)
    col = lax.broadcasted_iota(I32, (LANES, LANES), 1)
    seg_off = _dot_exact(padded, (row < col).astype(F32))
    tot = jnp.sum(padded, axis=0, keepdims=True)
    ptot = jnp.ceil(tot * (1.0 / EXPERT_BLOCK)) * EXPERT_BLOCK
    e_end = _dot_exact(jnp.broadcast_to(ptot, (SUBLANES, LANES)),
                       (row <= col).astype(F32))[0:1]
    e_start = e_end - ptot
    seg_off_ref[...] = seg_off.astype(I32)
    seg_len_ref[...] = padded.astype(I32)
    dest_ref[...] = (prefix + e_start).astype(I32)

    e_end_col = jnp.sum(jnp.where(row == col, jnp.broadcast_to(e_end, (LANES, LANES)), 0.0),
                        axis=1, keepdims=True)
    blk_row = (lax.broadcasted_iota(I32, (LANES, META_LANES), 1) * EXPERT_BLOCK).astype(F32)
    is_real = lax.broadcasted_iota(I32, (LANES, META_LANES), 0) < n_e
    blk_e = jnp.sum(jnp.where(is_real & (e_end_col <= blk_row), 1.0, 0.0), axis=0, keepdims=True)
    blk_e = jnp.minimum(blk_e, n_e - 1.0)
    lane = lax.broadcasted_iota(I32, (1, LANES), 1)
    n_used = jnp.sum(jnp.where(lane == n_e - 1, e_end, 0.0), axis=1, keepdims=True) * (
        1.0 / EXPERT_BLOCK)

    real_end_col = jnp.sum(jnp.where(row == col, jnp.broadcast_to(e_start + tot, (LANES, LANES)),
                                     0.0), axis=1, keepdims=True)
    of_blk = lax.broadcasted_iota(I32, (LANES, META_LANES), 0).astype(F32) == blk_e
    real_end = jnp.sum(jnp.where(of_blk, real_end_col, 0.0), axis=0, keepdims=True)
    real_rows = jnp.clip(real_end - blk_row[0:1], 0.0, 1.0 * EXPERT_BLOCK)

    def wide(r):
        return jnp.concatenate([r, jnp.zeros((1, META_LANES - LANES), F32)], axis=1)

    sub = lax.broadcasted_iota(I32, (SUBLANES, META_LANES), 0)
    meta = jnp.zeros((SUBLANES, META_LANES), F32)
    for r, val in ((META_BLK_E, blk_e), (META_PAD_DST, wide(e_start + tot)),
                   (META_PAD_LEN, wide(ptot - tot)),
                   (META_N_USED, jnp.broadcast_to(n_used, (1, META_LANES))),
                   (META_REAL_ROWS, real_rows)):
        meta = jnp.where(sub == r, jnp.broadcast_to(val, (SUBLANES, META_LANES)), meta)
    meta_ref[...] = meta.astype(I32)


def _load_matrices_bf16(pieces, stage, sem):
    def fetch(p):
        src, s_idx, _, _, (r, c) = pieces[p]
        return pltpu.make_async_copy(src.at[s_idx], stage.at[p % 2, 0:r, 0:c], sem.at[p % 2])

    fetch(0).start()
    for p, (_, _, dst, d_idx, (r, c)) in enumerate(pieces):
        if p + 1 < len(pieces):
            fetch(p + 1).start()
        fetch(p).wait()
        dst[d_idx] = stage[p % 2, 0:r, 0:c].astype(BF16)


def _mixer_kernel(x_ref, w_in_hbm, b_in_ref, pool_w_hbm, pool_scale_ref, pool_proj_hbm,
                  dw_ref, dwb_ref, cg_ref, cb_ref, cpw_hbm, cpwb_ref,
                  kt_ref, v_ref, ao_hbm, wout_hbm, bout_ref, g1_ref, b1_ref,
                  wr_hi_ref, wr_lo_ref, rb_ref,
                  x1_ref, ids_ref, route_t_ref, seg_off_ref, seg_len_ref, dest_ref, meta_ref,
                  w_in_ref, pool_w_ref, pool_proj_ref, cpw_ref, ao_ref, wout_ref, stage, stage_sem,
                  h_scr, x_scr, xb_scr, pool_ext, conv_ext, conv_out,
                  cnt_scr, pre_scr, run_scr,
                  *, alpha, n_s, layer):
    s = pl.program_id(0)
    n_tiles = pl.num_programs(0) - 1
    cur = (s + 1) % 2
    nxt = s % 2
    tile = jnp.maximum(s - 1, 0)
    i = tile % n_s
    ts, d = x_ref.shape[1], x_ref.shape[2]
    n_chunks = w_in_ref.shape[0]
    pw = pool_ext.shape[1]
    cw = conv_ext.shape[1]
    aw = kt_ref.shape[2]
    n_e = rb_ref.shape[1]
    c_conv = pw
    c_q = pw + 2 * cw
    c_gate = c_q + aw

    @pl.when(i == 0)
    def _():
        pool_ext[0:POOL_HALO, :] = jnp.zeros((POOL_HALO, pw), F32)
        conv_ext[0:CONV_HALO, :] = jnp.zeros((CONV_HALO, cw), F32)

    @pl.when(s == 0)
    def _():
        run_scr[...] = jnp.zeros(run_scr.shape, F32)
        h_scr[1] = jnp.zeros(h_scr.shape[1:], F32)
        x_scr[1] = jnp.zeros(x_scr.shape[1:], F32)
        pieces = []

        def add(src, dst, rows, cols, dst_chunked=False):
            for j in range(cols // PROJ_CHUNK):
                c0 = j * PROJ_CHUNK
                d_idx = (j,) if dst_chunked else (slice(None), slice(c0, c0 + PROJ_CHUNK))
                pieces.append((src, (layer, slice(None), slice(c0, c0 + PROJ_CHUNK)),
                               dst, d_idx, (rows, PROJ_CHUNK)))

        add(w_in_hbm, w_in_ref, d, n_chunks * PROJ_CHUNK, dst_chunked=True)
        add(pool_proj_hbm, pool_proj_ref, pw, d)
        add(cpw_hbm, cpw_ref, cw, d)
        add(ao_hbm, ao_ref, aw, d)
        add(wout_hbm, wout_ref, d, d)
        gd_ = pw // len(POOL_WINDOWS)
        for g in range(len(POOL_WINDOWS)):
            pieces.append((pool_w_hbm, (layer, g), pool_w_ref, (g,), (gd_, gd_)))
        _load_matrices_bf16(pieces, stage, stage_sem)

    x_in = x_ref[0]
    x_scr[nxt] = x_in
    xb_scr[...] = x_in.astype(BF16)

    x = x_scr[cur]

    def proj(lo, hi):
        return jnp.concatenate(
            [h_scr[cur, c] for c in range(lo // PROJ_CHUNK, hi // PROJ_CHUNK)], axis=1)

    glu = proj(c_conv, c_conv + cw) * jax.nn.sigmoid(proj(c_conv + cw, c_conv + 2 * cw))
    conv_ext[CONV_HALO:CONV_HALO + ts, :] = glu
    taps = dw_ref.shape[1]
    base = CONV_HALO - (taps - 1)
    span = CONV_ROWS + CONV_HALO
    n_row_chunks = ts // CONV_ROWS
    chunks_per_iter = min(LOOP_PROJ_CHUNKS, n_chunks // n_row_chunks)
    later_chunks = list(range(n_row_chunks * chunks_per_iter, n_chunks))

    def project(c):
        h_scr[nxt, c] = _dot(xb_scr[...], w_in_ref[c]) + b_in_ref[0, c]

    def project_some(n):
        for _ in range(min(n, len(later_chunks))):
            project(later_chunks.pop(0))

    def conv_rows_and_projection(j):
        r0 = j * CONV_ROWS
        for c0 in range(0, cw, LANES):
            window = conv_ext[pl.ds(r0, span), c0:c0 + LANES]
            acc = jnp.broadcast_to(dwb_ref[0, :, c0:c0 + LANES], (CONV_ROWS, LANES))
            for res in range(SUBLANES):
                offs = [o for o in range(base, base + taps) if o % SUBLANES == res]
                if not offs:
                    continue
                shifted = window if res == 0 else pltpu.roll(window, span - res, axis=0)
                for o in offs:
                    k = o - base
                    acc = acc + dw_ref[0, k:k + 1, c0:c0 + LANES] * shifted[
                        o - res:o - res + CONV_ROWS]
            conv_out[pl.ds(r0, CONV_ROWS), c0:c0 + LANES] = acc
        for m in range(chunks_per_iter):
            project(j * chunks_per_iter + m)

    for j in range(n_row_chunks):
        conv_rows_and_projection(j)
    conv_ext[0:CONV_HALO, :] = conv_ext[ts:ts + CONV_HALO, :]
    hc = _layer_norm(conv_out[...], cg_ref[0], cb_ref[0])
    hc = hc * jax.nn.sigmoid(hc)
    project_some(2)
    y_conv = _dot(hc.astype(BF16), cpw_ref[...]) + cpwb_ref[0]

    u = proj(0, pw)
    pool_ext[POOL_HALO:POOL_HALO + ts, :] = u
    t_glob = i * ts + lax.broadcasted_iota(I32, (ts, 1), 0)
    gd = pw // len(POOL_WINDOWS)
    pooled = []
    for g, w in enumerate(POOL_WINDOWS):
        lo = g * gd
        ug = u[:, lo:lo + gd]
        run = pool_ext[:, lo:lo + gd]
        step = 1
        while step < w:
            run = run + pltpu.roll(run, step, axis=0)
            step *= 2
        cnt = jnp.minimum(t_glob + 1, w).astype(F32)
        pooled.append((run[POOL_HALO:POOL_HALO + ts] / cnt - ug).astype(BF16))
    project_some(2)
    mixed = [_dot(p, pool_w_ref[g]) for g, p in enumerate(pooled)]
    mixed = jnp.concatenate(mixed, axis=1) * pool_scale_ref[0]
    y_pool = _dot(mixed.astype(BF16), pool_proj_ref[...])
    pool_ext[0:POOL_HALO, :] = pool_ext[ts:ts + POOL_HALO, :]

    q = proj(c_q, c_q + aw)
    heads = []
    for h in range(aw // HEAD_DIM):
        lo = h * HEAD_DIM
        sc = _dot(q[:, lo:lo + HEAD_DIM].astype(BF16), kt_ref[0, 0, lo:lo + HEAD_DIM, :])
        sc = sc * (HEAD_DIM ** -0.5)
        p = jnp.exp(sc - jnp.max(sc, axis=-1, keepdims=True))
        o = _dot(p.astype(BF16), v_ref[0, 0, :, lo:lo + HEAD_DIM])
        heads.append(o / jnp.sum(p, axis=-1, keepdims=True))
    y_attn = _dot(jnp.concatenate(heads, axis=1).astype(BF16), ao_ref[...])

    merged = jax.nn.sigmoid(proj(c_gate, c_gate + d)) * y_pool
    merged = merged + jax.nn.sigmoid(proj(c_gate + d, c_gate + 2 * d)) * y_conv
    merged = merged + jax.nn.sigmoid(proj(c_gate + 2 * d, c_gate + 3 * d)) * y_attn
    project_some(2)
    out = _dot(merged.astype(BF16), wout_ref[...]) + bout_ref[0]
    x1 = _layer_norm(alpha * x + out, g1_ref[0], b1_ref[0])
    x1_ref[0] = x1

    x1_hi = x1.astype(BF16)
    x1_lo = (x1 - x1_hi.astype(F32)).astype(BF16)
    logits = _dot(x1_hi, wr_hi_ref[0]) + _dot(x1_lo, wr_hi_ref[0]) + _dot(x1_hi, wr_lo_ref[0])
    logits = logits.T[0:n_e, :] + rb_ref[0]
    e_iota = lax.broadcasted_iota(I32, (n_e, ts), 0)
    rank = jnp.zeros((n_e, ts), F32)
    for other in range(n_e):
        row = logits[other:other + 1, :]
        beats = (row > logits) | ((row == logits) & (e_iota > other))
        rank = rank + jnp.where(beats, 1.0, 0.0)
    hot = jnp.where(rank < TOP_K, 1.0, 0.0)
    top_v, top_i = [], []
    for k in range(TOP_K):
        sel = rank == k
        top_i.append(jnp.sum(jnp.where(sel, e_iota, 0), axis=0, keepdims=True))
        top_v.append(jnp.sum(jnp.where(sel, logits, 0.0), axis=0, keepdims=True))
    ex = [jnp.exp(v - top_v[0]) for v in top_v]
    den = ex[0] + ex[1] + ex[2] + ex[3]
    sub = lax.broadcasted_iota(I32, (SUBLANES, ts), 0)
    route = jnp.zeros((SUBLANES, ts), F32)
    for k in range(TOP_K):
        ids_ref[k:k + 1, :] = top_i[k]
        route = jnp.where(sub == k, top_i[k].astype(F32), route)
        route = jnp.where(sub == TOP_K + k, ex[k] / den, route)
    route_t_ref[...] = jnp.concatenate(
        [route, jnp.zeros((LANES - SUBLANES, ts), F32)], axis=0).T

    hot_wide = jnp.concatenate([hot, jnp.zeros((LANES - n_e, ts), F32)], axis=0).astype(BF16)
    cnt = lax.dot_general(jnp.ones((SUBLANES, ts), BF16), hot_wide, (((1,), (1,)), ((), ())),
                          preferred_element_type=F32)[0:1]
    is_expert = lax.broadcasted_iota(I32, (1, LANES), 1) < n_e
    padded = jnp.where(is_expert, jnp.maximum(jnp.ceil(cnt * (1.0 / SEG_ALIGN)), 1.0), 0.0)
    padded = padded * SEG_ALIGN
    padded = jnp.where(s >= 1, padded, 0.0)
    cnt_scr[pl.ds(tile, 1), :] = padded
    pre_scr[pl.ds(tile, 1), :] = run_scr[0:1, :]
    run_scr[0:1, :] = run_scr[0:1, :] + padded

    project_some(len(later_chunks))

    @pl.when(s == n_tiles)
    def _():
        _routing_tables(cnt_scr[...], pre_scr[...], n_e,
                        seg_off_ref, seg_len_ref, dest_ref, meta_ref)


def _mixer(x, weights, kt, v, layer, alpha):
    n_b, s, d = x.shape
    ts = TOKEN_TILE
    n_s = s // ts
    n_t = n_b * n_s
    names = ("w_in", "b_in", "pool_w", "pool_scale", "pool_proj", "conv_dw", "conv_dw_b",
             "conv_ln_g", "conv_ln_b", "conv_pw", "conv_pw_b")
    names2 = ("attn_o", "w_out", "b_out", "ln1_g", "ln1_b", "router_hi", "router_lo", "router_b")
    in_hbm = ("w_in", "pool_w", "pool_proj", "conv_pw", "attn_o", "w_out")
    pw = weights["pool_proj"].shape[1]
    cw = weights["conv_pw"].shape[1]
    aw = weights["attn_o"].shape[1]
    n_chunks = weights["w_in"].shape[2] // PROJ_CHUNK
    gd = pw // len(POOL_WINDOWS)

    def of_layer(arr):
        tail = (0,) * (arr.ndim - 1)
        return pl.BlockSpec((1,) + arr.shape[1:], lambda st: (layer,) + tail,
                            pipeline_mode=pl.Buffered(1))

    def spec(name):
        if name in in_hbm:
            return pl.BlockSpec(memory_space=pl.ANY)
        return of_layer(weights[name])

    def resident(shape):
        return pl.BlockSpec(shape, lambda st: (0, 0))

    def tile_a(st):
        return jnp.minimum(st, n_t - 1)

    def tile_b(st):
        return jnp.maximum(st - 1, 0)

    in_specs = ([pl.BlockSpec((1, ts, d), lambda st: (tile_a(st) // n_s, tile_a(st) % n_s, 0))]
                + [spec(k) for k in names]
                + [pl.BlockSpec((1, 1) + kt.shape[2:],
                                lambda st: (layer, tile_b(st) // n_s, 0, 0)),
                   pl.BlockSpec((1, 1) + v.shape[2:],
                                lambda st: (layer, tile_b(st) // n_s, 0, 0))]
                + [spec(k) for k in names2])
    out_specs = [
        pl.BlockSpec((1, ts, d), lambda st: (tile_b(st) // n_s, tile_b(st) % n_s, 0)),
        pl.BlockSpec((TOP_K, ts), lambda st: (0, tile_b(st))),
        pl.BlockSpec((ts, LANES), lambda st: (tile_b(st), 0)),
        resident((n_t, LANES)), resident((n_t, LANES)), resident((n_t, LANES)),
        resident((SUBLANES, META_LANES)),
    ]
    out_shape = [
        jax.ShapeDtypeStruct((n_b, s, d), F32),
        jax.ShapeDtypeStruct((TOP_K, n_b * s), I32),
        jax.ShapeDtypeStruct((n_b * s, LANES), F32),
        jax.ShapeDtypeStruct((n_t, LANES), I32),
        jax.ShapeDtypeStruct((n_t, LANES), I32),
        jax.ShapeDtypeStruct((n_t, LANES), I32),
        jax.ShapeDtypeStruct((SUBLANES, META_LANES), I32),
    ]
    return pl.pallas_call(
        functools.partial(_mixer_kernel, alpha=alpha, n_s=n_s, layer=layer),
        grid=(n_t + 1,),
        in_specs=in_specs,
        out_specs=out_specs,
        out_shape=out_shape,
        scratch_shapes=[
            pltpu.VMEM((n_chunks, d, PROJ_CHUNK), BF16),
            pltpu.VMEM((len(POOL_WINDOWS), gd, gd), BF16),
            pltpu.VMEM((pw, d), BF16),
            pltpu.VMEM((cw, d), BF16),
            pltpu.VMEM((aw, d), BF16),
            pltpu.VMEM((d, d), BF16),
            pltpu.VMEM((2, max(d, pw, cw, aw), PROJ_CHUNK), F32),
            pltpu.SemaphoreType.DMA((2,)),
            pltpu.VMEM((2, n_chunks, ts, PROJ_CHUNK), F32),
            pltpu.VMEM((2, ts, d), F32),
            pltpu.VMEM((ts, d), BF16),
            pltpu.VMEM((ts + POOL_HALO, pw), F32),
            pltpu.VMEM((ts + CONV_HALO, cw), F32),
            pltpu.VMEM((ts, cw), F32),
            pltpu.VMEM((n_t, LANES), F32),
            pltpu.VMEM((n_t, LANES), F32),
            pltpu.VMEM((SUBLANES, LANES), F32),
        ],
        compiler_params=pltpu.CompilerParams(
            dimension_semantics=("arbitrary",), vmem_limit_bytes=VMEM_LIMIT),
        name="mixer",
    )(x, *[weights[k] for k in names], kt, v, *[weights[k] for k in names2])


def _start_segment_copies(seg_off_ref, seg_len_ref, dest_ref, tile, n_e, make_copy):
    for e in range(n_e):
        make_copy(pl.multiple_of(seg_off_ref[tile, e], SEG_ALIGN),
                  pl.multiple_of(dest_ref[tile, e], SEG_ALIGN),
                  pl.multiple_of(seg_len_ref[tile, e], SEG_ALIGN)).start()


def _wait_segment_copies(seg_off_ref, seg_len_ref, tile, n_e, make_copy):
    total = seg_off_ref[tile, n_e - 1] + seg_len_ref[tile, n_e - 1]
    make_copy(0, 0, pl.multiple_of(total, SEG_ALIGN)).wait()


def _dispatch_kernel(seg_off_ref, seg_len_ref, dest_ref, meta_ref,
                     ids_ref, off_ref, x1_ref, xs_ref, buf, zbuf, sem, zsem, *, n_e):
    tile = pl.program_id(0)
    n_tiles = pl.num_programs(0)
    slot = tile % 2
    ts = x1_ref.shape[0]
    rows = buf.shape[1]

    @pl.when(tile == 0)
    def _():
        zbuf[...] = jnp.zeros(zbuf.shape, U32)

        def zero_copy(e):
            n = pl.multiple_of(meta_ref[META_PAD_LEN, e], SEG_ALIGN)
            dst = pl.multiple_of(meta_ref[META_PAD_DST, e], SEG_ALIGN)
            return n, pltpu.make_async_copy(zbuf.at[pl.ds(0, n)], xs_ref.at[pl.ds(dst, n)], zsem)

        def zstart(e, c):
            n, cp = zero_copy(e)
            pl.when(n > 0)(cp.start)
            return c

        def zwait(e, c):
            n, cp = zero_copy(e)
            pl.when(n > 0)(cp.wait)
            return c

        lax.fori_loop(0, n_e, zstart, 0)
        lax.fori_loop(0, n_e, zwait, 0)

    def copies_from(s):
        def make_copy(off, dst, n):
            return pltpu.make_async_copy(buf.at[s, pl.ds(off, n)], xs_ref.at[pl.ds(dst, n)],
                                         sem.at[s])
        return make_copy

    @pl.when(tile >= 2)
    def _():
        _wait_segment_copies(seg_off_ref, seg_len_ref, tile - 2, n_e, copies_from(slot))

    ids = ids_ref[...]
    e_iota = lax.broadcasted_iota(I32, (n_e, ts), 0)
    hots = [(e_iota == ids[k:k + 1, :]).astype(F32) for k in range(TOP_K)]
    hot = hots[0] + hots[1] + hots[2] + hots[3]
    off_row = off_ref[pl.ds(tile, 1), :].astype(F32)
    eye = (lax.broadcasted_iota(I32, (n_e, LANES), 0)
           == lax.broadcasted_iota(I32, (n_e, LANES), 1))
    off_col = jnp.sum(jnp.where(eye, off_row, 0.0), axis=1, keepdims=True)
    upper = (lax.broadcasted_iota(I32, (ts, ts), 0)
             < lax.broadcasted_iota(I32, (ts, ts), 1)).astype(BF16)
    place = _dot(hot.astype(BF16), upper) + off_col
    pos = [jnp.sum(hots[k] * place, axis=0, keepdims=True).astype(I32)
           for k in range(TOP_K)]
    x1b = x1_ref[...].astype(BF16)
    for r0 in range(0, rows, SORT_ROWS):
        r_iota = r0 + lax.broadcasted_iota(I32, (SORT_ROWS, ts), 0)
        hit = (r_iota == pos[0]) | (r_iota == pos[1]) | (r_iota == pos[2]) | (r_iota == pos[3])
        perm = jnp.where(hit, 1.0, 0.0).astype(BF16)
        buf[slot, r0:r0 + SORT_ROWS] = _pack_pairs(_dot(perm, x1b))
    _start_segment_copies(seg_off_ref, seg_len_ref, dest_ref, tile, n_e, copies_from(slot))

    @pl.when(tile == n_tiles - 1)
    def _():
        @pl.when(tile >= 1)
        def _():
            _wait_segment_copies(seg_off_ref, seg_len_ref, tile - 1, n_e, copies_from(1 - slot))

        _wait_segment_copies(seg_off_ref, seg_len_ref, tile, n_e, copies_from(slot))


def _tile_rows(ts, n_e):
    return TOP_K * ts + n_e * SEG_ALIGN


def _dispatch(tables, meta, ids, x1, n_rows, n_e):
    n_tok, d = x1.shape
    ts = TOKEN_TILE
    n_t = n_tok // ts
    rows = _tile_rows(ts, n_e)
    return pl.pallas_call(
        functools.partial(_dispatch_kernel, n_e=n_e),
        grid_spec=pltpu.PrefetchScalarGridSpec(
            num_scalar_prefetch=4,
            grid=(n_t,),
            in_specs=[
                pl.BlockSpec((TOP_K, ts), lambda t, *_: (0, t)),
                pl.BlockSpec((n_t, LANES), lambda t, *_: (0, 0)),
                pl.BlockSpec((ts, d), lambda t, *_: (t, 0)),
            ],
            out_specs=pl.BlockSpec(memory_space=pl.ANY),
            scratch_shapes=[pltpu.VMEM((2, rows, d // 2), U32),
                            pltpu.VMEM((EXPERT_BLOCK, d // 2), U32),
                            pltpu.SemaphoreType.DMA((2,)), pltpu.SemaphoreType.DMA(())],
        ),
        out_shape=jax.ShapeDtypeStruct((n_rows, d // 2), U32),
        compiler_params=pltpu.CompilerParams(
            dimension_semantics=("arbitrary",), vmem_limit_bytes=VMEM_LIMIT),
        name="dispatch",
    )(*tables, meta, ids, tables[0], x1)


def _expert_kernel(meta_ref, xs_ref, wup_ref, bup_ref, wdn_ref, bdn_ref,
                   ys_ref, wup_bf, wdn_bf):
    j = pl.program_id(0)
    f = wdn_ref.shape[2]

    @pl.when(j < meta_ref[META_N_USED, 0])
    def _():
        prev = meta_ref[META_BLK_E, jnp.maximum(j - 1, 0)]

        @pl.when((j == 0) | (meta_ref[META_BLK_E, j] != prev))
        def _():
            wup_bf[...] = wup_ref[0, 0].astype(BF16)
            wdn_bf[...] = wdn_ref[0, 0].astype(BF16)

        def expert_rows(n_rows):
            x = _unpack_pairs(xs_ref[0:n_rows, :])

            def up(c0):
                glu = _dot(x, wup_bf[:, c0:c0 + FF_CHUNK]) + bup_ref[0, 0, :, c0:c0 + FF_CHUNK]
                lin = (_dot(x, wup_bf[:, f + c0:f + c0 + FF_CHUNK])
                       + bup_ref[0, 0, :, f + c0:f + c0 + FF_CHUNK])
                return glu, lin

            y = jnp.broadcast_to(bdn_ref[0, 0], (n_rows, wdn_ref.shape[3]))
            nxt = up(0)
            for c0 in range(0, f, FF_CHUNK):
                glu, lin = nxt
                if c0 + FF_CHUNK < f:
                    nxt = up(c0 + FF_CHUNK)
                glu = jnp.minimum(glu, SWIGLU_LIMIT)
                lin = jnp.clip(lin, -SWIGLU_LIMIT, SWIGLU_LIMIT)
                act = glu * jax.nn.sigmoid(SWIGLU_ALPHA * glu) * (lin + 1.0)
                y = y + _dot(act.astype(BF16), wdn_bf[c0:c0 + FF_CHUNK, :])
            ys_ref[0:n_rows, :] = _pack_pairs(y.astype(BF16).astype(F32))

        bm = xs_ref.shape[0]
        part = bm // EXPERT_PARTS
        real_rows = meta_ref[META_REAL_ROWS, j]
        for q in range(1, EXPERT_PARTS + 1):
            pl.when((real_rows > (q - 1) * part) & (real_rows <= q * part))(
                functools.partial(expert_rows, q * part))


def _experts(meta, xs, w_up, b_up, w_down, b_down, layer):
    n_rows, half_d = xs.shape
    d = 2 * half_d
    f2 = w_up.shape[-1]
    f = f2 // 2
    bm = EXPERT_BLOCK
    n_blk = n_rows // bm

    def last_used(j, meta):
        return jnp.maximum(jnp.minimum(j, meta[META_N_USED, 0] - 1), 0)

    def row_blk(j, meta):
        return (last_used(j, meta), 0)

    def w_blk(j, meta):
        return (layer, meta[META_BLK_E, last_used(j, meta)], 0, 0)

    return pl.pallas_call(
        _expert_kernel,
        grid_spec=pltpu.PrefetchScalarGridSpec(
            num_scalar_prefetch=1,
            grid=(n_blk,),
            in_specs=[
                pl.BlockSpec((bm, half_d), row_blk),
                pl.BlockSpec((1, 1, d, f2), w_blk),
                pl.BlockSpec((1, 1, 1, f2), w_blk),
                pl.BlockSpec((1, 1, f, d), w_blk),
                pl.BlockSpec((1, 1, 1, d), w_blk),
            ],
            out_specs=pl.BlockSpec((bm, half_d), row_blk),
            scratch_shapes=[pltpu.VMEM((d, f2), BF16), pltpu.VMEM((f, d), BF16)],
        ),
        out_shape=jax.ShapeDtypeStruct((n_rows, half_d), U32),
        compiler_params=pltpu.CompilerParams(
            dimension_semantics=("arbitrary",), vmem_limit_bytes=VMEM_LIMIT),
        name="experts",
    )(meta, xs, w_up, b_up, w_down, b_down)


def _combine_kernel(seg_off_ref, seg_len_ref, dest_ref,
                    route_ref, off_ref, x1_ref, g2_ref, b2_ref, ys_ref,
                    out_ref, buf, sem, *, alpha, n_e):
    tile = pl.program_id(0)
    n_tiles = pl.num_programs(0)
    slot = tile % 2
    ts = x1_ref.shape[0]
    rows = buf.shape[1]

    def copies_into(s):
        def make_copy(off, dst, n):
            return pltpu.make_async_copy(ys_ref.at[pl.ds(dst, n)], buf.at[s, pl.ds(off, n)],
                                         sem.at[s])
        return make_copy

    @pl.when(tile == 0)
    def _():
        buf[...] = jnp.zeros(buf.shape, U32)
        _start_segment_copies(seg_off_ref, seg_len_ref, dest_ref, tile, n_e, copies_into(slot))

    nxt = jnp.minimum(tile + 1, n_tiles - 1)
    _start_segment_copies(seg_off_ref, seg_len_ref, dest_ref, nxt, n_e, copies_into(1 - slot))

    route = route_ref[...]
    l_iota = lax.broadcasted_iota(I32, (ts, LANES), 1)
    hots = [(l_iota == route[:, k:k + 1].astype(I32)).astype(F32) for k in range(TOP_K)]
    hot = hots[0] + hots[1] + hots[2] + hots[3]
    lower = (lax.broadcasted_iota(I32, (ts, ts), 1)
             < lax.broadcasted_iota(I32, (ts, ts), 0)).astype(BF16)
    place = _dot(lower, hot.astype(BF16)) + off_ref[pl.ds(tile, 1), :].astype(F32)
    pos = [jnp.sum(hots[k] * place, axis=1, keepdims=True).astype(I32)
           for k in range(TOP_K)]

    _wait_segment_copies(seg_off_ref, seg_len_ref, tile, n_e, copies_into(slot))
    y = jnp.zeros(out_ref.shape, F32)
    for r0 in range(0, rows, PERM_ROWS):
        r_iota = r0 + lax.broadcasted_iota(I32, (ts, PERM_ROWS), 1)
        weight = jnp.zeros((ts, PERM_ROWS), F32)
        for k in range(TOP_K):
            weight = jnp.where(r_iota == pos[k], route[:, TOP_K + k:TOP_K + k + 1], weight)
        y = y + _dot(weight.astype(BF16), _unpack_pairs(buf[slot, r0:r0 + PERM_ROWS]))
    out_ref[...] = _layer_norm(alpha * x1_ref[...] + y, g2_ref[0], b2_ref[0])

    @pl.when(tile == n_tiles - 1)
    def _():
        _wait_segment_copies(seg_off_ref, seg_len_ref, tile, n_e, copies_into(1 - slot))


def _combine(tables, route_t, x1, g2, b2, ys, layer, alpha, n_e):
    n_tok, d = x1.shape
    ts = TOKEN_TILE
    n_t = n_tok // ts
    rows = _tile_rows(ts, n_e)
    return pl.pallas_call(
        functools.partial(_combine_kernel, alpha=alpha, n_e=n_e),
        grid_spec=pltpu.PrefetchScalarGridSpec(
            num_scalar_prefetch=3,
            grid=(n_t,),
            in_specs=[
                pl.BlockSpec((ts, LANES), lambda t, *_: (t, 0)),
                pl.BlockSpec((n_t, LANES), lambda t, *_: (0, 0)),
                pl.BlockSpec((ts, d), lambda t, *_: (t, 0)),
                pl.BlockSpec((1, 1, d), lambda t, *_: (layer, 0, 0)),
                pl.BlockSpec((1, 1, d), lambda t, *_: (layer, 0, 0)),
                pl.BlockSpec(memory_space=pl.ANY),
            ],
            out_specs=pl.BlockSpec((ts, d), lambda t, *_: (t, 0)),
            scratch_shapes=[pltpu.VMEM((2, rows, d // 2), U32), pltpu.SemaphoreType.DMA((2,))],
        ),
        out_shape=jax.ShapeDtypeStruct((n_tok, d), F32),
        compiler_params=pltpu.CompilerParams(
            dimension_semantics=("arbitrary",), vmem_limit_bytes=VMEM_LIMIT),
        name="combine",
    )(*tables, route_t, tables[0], x1, g2, b2, ys)


def kernel(x, mem, mem_ln_g, mem_ln_b, w_in, b_in, pool_w, pool_scale, pool_proj, conv_dw, conv_dw_b, conv_ln_g, conv_ln_b, conv_pw, conv_pw_b, w_kv, attn_o, w_out, b_out, ln1_g, ln1_b, router_w, router_b, exp_up, exp_up_b, exp_down, exp_down_b, ln2_g, ln2_b):
    n_b, s, d = x.shape
    depth = w_in.shape[0]
    n_e = router_w.shape[-1]
    n_tok = n_b * s
    n_t = n_tok // TOKEN_TILE
    alpha = (2.0 * depth) ** 0.25
    n_rows = TOP_K * n_tok + n_t * n_e * SEG_ALIGN + n_e * EXPERT_BLOCK
    n_rows = -(-n_rows // EXPERT_BLOCK) * EXPERT_BLOCK
    assert n_rows // EXPERT_BLOCK <= META_LANES and n_e <= LANES

    def rows3(a):
        return a.reshape(a.shape[0], 1, a.shape[1])

    in_cols = w_in.shape[-1]
    assert in_cols % PROJ_CHUNK == 0
    n_chunks = in_cols // PROJ_CHUNK
    router_pad = jnp.pad(router_w, ((0, 0), (0, 0), (0, LANES - n_e)))
    router_hi = router_pad.astype(BF16)
    weights = dict(
        w_in=w_in, b_in=b_in.reshape(depth, n_chunks, 1, PROJ_CHUNK), pool_w=pool_w,
        pool_scale=rows3(pool_scale), pool_proj=pool_proj, conv_dw=conv_dw,
        conv_dw_b=rows3(conv_dw_b), conv_ln_g=rows3(conv_ln_g), conv_ln_b=rows3(conv_ln_b),
        conv_pw=conv_pw, conv_pw_b=rows3(conv_pw_b), attn_o=attn_o,
        w_out=w_out, b_out=rows3(b_out), ln1_g=rows3(ln1_g), ln1_b=rows3(ln1_b),
        router_hi=router_hi, router_lo=(router_pad - router_hi.astype(F32)).astype(BF16),
        router_b=router_b.reshape(depth, n_e, 1))
    up_b = exp_up_b.reshape(depth, n_e, 1, exp_up_b.shape[-1])
    down_b = exp_down_b.reshape(depth, n_e, 1, d)
    g2, b2 = rows3(ln2_g), rows3(ln2_b)

    kt_all, v_all = _memory_kv(mem, mem_ln_g, mem_ln_b, w_kv)
    for layer in range(depth):
        x1, ids, route_t, seg_off, seg_len, dest, meta = _mixer(
            x, weights, kt_all, v_all, layer, alpha)
        x1 = x1.reshape(n_tok, d)
        tables = (seg_off, seg_len, dest)
        xs = _dispatch(tables, meta, ids, x1, n_rows, n_e)
        ys = _experts(meta, xs, exp_up, up_b, exp_down, down_b, layer)
        x = _combine(tables, route_t, x1, g2, b2, ys, layer, alpha, n_e).reshape(n_b, s, d)
    return x
```

```python
import functools

import jax
import jax.numpy as jnp
from jax import lax
from jax.experimental import pallas as pl
from jax.experimental.pallas import tpu as pltpu

F32 = jnp.float32
BF16 = jnp.bfloat16
I32 = jnp.int32

POOL_WINDOWS = (2, 4, 8, 16)
POOL_HALO = 16
CONV_HALO = 32
HEAD_DIM = 128
TOP_K = 4
SWIGLU_LIMIT = 7.0
SWIGLU_ALPHA = 1.702
LN_EPS = 1e-5

LANES = 128
SUBLANES = 8
SEG_ALIGN = SUBLANES
U32 = jnp.uint32
TOKEN_TILE = 256
EXPERT_BLOCK = 1024
EXPERT_PARTS = 4
FF_CHUNK = 256
CONV_ROWS = 64
PROJ_CHUNK = 256
LOOP_PROJ_CHUNKS = 5
SORT_ROWS = 128
PERM_ROWS = 256
META_LANES = 512
META_BLK_E, META_PAD_DST, META_PAD_LEN, META_N_USED, META_REAL_ROWS = 0, 1, 2, 3, 4
VMEM_LIMIT = 56 * 1024 * 1024


def _layer_norm(x, g, b):
    mu = jnp.mean(x, axis=-1, keepdims=True)
    xc = x - mu
    var = jnp.mean(xc * xc, axis=-1, keepdims=True)
    return xc * lax.rsqrt(var + LN_EPS) * g + b


def _dot(a, b):
    return jnp.dot(a, b, preferred_element_type=F32)


def _pack_pairs(v):
    n = v.shape[1] // 2
    lo = lax.bitcast_convert_type(v[:, :n], U32)
    hi = lax.bitcast_convert_type(v[:, n:], U32)
    return (hi & jnp.uint32(0xFFFF0000)) | (lo >> 16)


def _unpack_pairs(w):
    lo = lax.bitcast_convert_type(w << 16, F32)
    hi = lax.bitcast_convert_type(w & jnp.uint32(0xFFFF0000), F32)
    return jnp.concatenate([lo, hi], axis=1).astype(BF16)


def _dot_exact(a, b):
    return jnp.dot(a, b, preferred_element_type=F32, precision=lax.Precision.HIGHEST)


def _kv_kernel(mem_ref, g_ref, b_ref, wkv_ref, kt_ref, v_ref):
    a = kt_ref.shape[2]
    mem_n = _layer_norm(mem_ref[0], g_ref[...], b_ref[...]).astype(BF16)
    kv = _dot(mem_n, wkv_ref[0].astype(BF16))
    kt_ref[0, 0] = kv[:, :a].T.astype(BF16)
    v_ref[0, 0] = kv[:, a:].astype(BF16)


def _memory_kv(mem, g, b, w_kv):
    n_b, m, d = mem.shape
    n_l, _, a2 = w_kv.shape
    a = a2 // 2
    return pl.pallas_call(
        _kv_kernel,
        grid=(n_l, n_b),
        in_specs=[
            pl.BlockSpec((1, m, d), lambda l, bb: (bb, 0, 0)),
            pl.BlockSpec((1, d), lambda l, bb: (0, 0)),
            pl.BlockSpec((1, d), lambda l, bb: (0, 0)),
            pl.BlockSpec((1, d, a2), lambda l, bb: (l, 0, 0)),
        ],
        out_specs=[
            pl.BlockSpec((1, 1, a, m), lambda l, bb: (l, bb, 0, 0)),
            pl.BlockSpec((1, 1, m, a), lambda l, bb: (l, bb, 0, 0)),
        ],
        out_shape=[
            jax.ShapeDtypeStruct((n_l, n_b, a, m), BF16),
            jax.ShapeDtypeStruct((n_l, n_b, m, a), BF16),
        ],
        compiler_params=pltpu.CompilerParams(
            dimension_semantics=("arbitrary", "arbitrary"), vmem_limit_bytes=VMEM_LIMIT),
        name="memory_kv",
    )(mem, g.reshape(1, d), b.reshape(1, d), w_kv)


def _routing_tables(padded, prefix, n_e, seg_off_ref, seg_len_ref, dest_ref, meta_ref):
    row = lax.broadcasted_iota(I32, (LANES, LANES), 0)
    col = lax.broadcasted_iota(I32, (LANES, LANES), 1)
    seg_off = _dot_exact(padded, (row < col).astype(F32))
    tot = jnp.sum(padded, axis=0, keepdims=True)
    ptot = jnp.ceil(tot * (1.0 / EXPERT_BLOCK)) * EXPERT_BLOCK
    e_end = _dot_exact(jnp.broadcast_to(ptot, (SUBLANES, LANES)),
                       (row <= col).astype(F32))[0:1]
    e_start = e_end - ptot
    seg_off_ref[...] = seg_off.astype(I32)
    seg_len_ref[...] = padded.astype(I32)
    dest_ref[...] = (prefix + e_start).astype(I32)

    e_end_col = jnp.sum(jnp.where(row == col, jnp.broadcast_to(e_end, (LANES, LANES)), 0.0),
                        axis=1, keepdims=True)
    blk_row = (lax.broadcasted_iota(I32, (LANES, META_LANES), 1) * EXPERT_BLOCK).astype(F32)
    is_real = lax.broadcasted_iota(I32, (LANES, META_LANES), 0) < n_e
    blk_e = jnp.sum(jnp.where(is_real & (e_end_col <= blk_row), 1.0, 0.0), axis=0, keepdims=True)
    blk_e = jnp.minimum(blk_e, n_e - 1.0)
    lane = lax.broadcasted_iota(I32, (1, LANES), 1)
    n_used = jnp.sum(jnp.where(lane == n_e - 1, e_end, 0.0), axis=1, keepdims=True) * (
        1.0 / EXPERT_BLOCK)

    real_end_col = jnp.sum(jnp.where(row == col, jnp.broadcast_to(e_start + tot, (LANES, LANES)),
                                     0.0), axis=1, keepdims=True)
    of_blk = lax.broadcasted_iota(I32, (LANES, META_LANES), 0).astype(F32) == blk_e
    real_end = jnp.sum(jnp.where(of_blk, real_end_col, 0.0), axis=0, keepdims=True)
    real_rows = jnp.clip(real_end - blk_row[0:1], 0.0, 1.0 * EXPERT_BLOCK)

    def wide(r):
        return jnp.concatenate([r, jnp.zeros((1, META_LANES - LANES), F32)], axis=1)

    sub = lax.broadcasted_iota(I32, (SUBLANES, META_LANES), 0)
    meta = jnp.zeros((SUBLANES, META_LANES), F32)
    for r, val in ((META_BLK_E, blk_e), (META_PAD_DST, wide(e_start + tot)),
                   (META_PAD_LEN, wide(ptot - tot)),
                   (META_N_USED, jnp.broadcast_to(n_used, (1, META_LANES))),
                   (META_REAL_ROWS, real_rows)):
        meta = jnp.where(sub == r, jnp.broadcast_to(val, (SUBLANES, META_LANES)), meta)
    meta_ref[...] = meta.astype(I32)


def _load_matrices_bf16(pieces, stage, sem):
    def fetch(p):
        src, s_idx, _, _, (r, c) = pieces[p]
        return pltpu.make_async_copy(src.at[s_idx], stage.at[p % 2, 0:r, 0:c], sem.at[p % 2])

    fetch(0).start()
    for p, (_, _, dst, d_idx, (r, c)) in enumerate(pieces):
        if p + 1 < len(pieces):
            fetch(p + 1).start()
        fetch(p).wait()
        dst[d_idx] = stage[p % 2, 0:r, 0:c].astype(BF16)


def _mixer_kernel(x_ref, w_in_hbm, b_in_ref, pool_w_hbm, pool_scale_ref, pool_proj_hbm,
                  dw_ref, dwb_ref, cg_ref, cb_ref, cpw_hbm, cpwb_ref,
                  kt_ref, v_ref, ao_hbm, wout_hbm, bout_ref, g1_ref, b1_ref,
                  wr_hi_ref, wr_lo_ref, rb_ref,
                  x1_ref, ids_ref, route_t_ref, seg_off_ref, seg_len_ref, dest_ref, meta_ref,
                  w_in_ref, pool_w_ref, pool_proj_ref, cpw_ref, ao_ref, wout_ref, stage, stage_sem,
                  h_scr, x_scr, xb_scr, pool_ext, conv_ext, conv_out,
                  cnt_scr, pre_scr, run_scr,
                  *, alpha, n_s, layer):
    s = pl.program_id(0)
    n_tiles = pl.num_programs(0) - 1
    cur = (s + 1) % 2
    nxt = s % 2
    tile = jnp.maximum(s - 1, 0)
    i = tile % n_s
    ts, d = x_ref.shape[1], x_ref.shape[2]
    n_chunks = w_in_ref.shape[0]
    pw = pool_ext.shape[1]
    cw = conv_ext.shape[1]
    aw = kt_ref.shape[2]
    n_e = rb_ref.shape[1]
    c_conv = pw
    c_q = pw + 2 * cw
    c_gate = c_q + aw

    @pl.when(i == 0)
    def _():
        pool_ext[0:POOL_HALO, :] = jnp.zeros((POOL_HALO, pw), F32)
        conv_ext[0:CONV_HALO, :] = jnp.zeros((CONV_HALO, cw), F32)

    @pl.when(s == 0)
    def _():
        run_scr[...] = jnp.zeros(run_scr.shape, F32)
        h_scr[1] = jnp.zeros(h_scr.shape[1:], F32)
        x_scr[1] = jnp.zeros(x_scr.shape[1:], F32)
        pieces = []

        def add(src, dst, rows, cols, dst_chunked=False):
            for j in range(cols // PROJ_CHUNK):
                c0 = j * PROJ_CHUNK
                d_idx = (j,) if dst_chunked else (slice(None), slice(c0, c0 + PROJ_CHUNK))
                pieces.append((src, (layer, slice(None), slice(c0, c0 + PROJ_CHUNK)),
                               dst, d_idx, (rows, PROJ_CHUNK)))

        add(w_in_hbm, w_in_ref, d, n_chunks * PROJ_CHUNK, dst_chunked=True)
        add(pool_proj_hbm, pool_proj_ref, pw, d)
        add(cpw_hbm, cpw_ref, cw, d)
        add(ao_hbm, ao_ref, aw, d)
        add(wout_hbm, wout_ref, d, d)
        gd_ = pw // len(POOL_WINDOWS)
        for g in range(len(POOL_WINDOWS)):
            pieces.append((pool_w_hbm, (layer, g), pool_w_ref, (g,), (gd_, gd_)))
        _load_matrices_bf16(pieces, stage, stage_sem)

    x_in = x_ref[0]
    x_scr[nxt] = x_in
    xb_scr[...] = x_in.astype(BF16)

    x = x_scr[cur]

    def proj(lo, hi):
        return jnp.concatenate(
            [h_scr[cur, c] for c in range(lo // PROJ_CHUNK, hi // PROJ_CHUNK)], axis=1)

    glu = proj(c_conv, c_conv + cw) * jax.nn.sigmoid(proj(c_conv + cw, c_conv + 2 * cw))
    conv_ext[CONV_HALO:CONV_HALO + ts, :] = glu
    taps = dw_ref.shape[1]
    base = CONV_HALO - (taps - 1)
    span = CONV_ROWS + CONV_HALO
    n_row_chunks = ts // CONV_ROWS
    chunks_per_iter = min(LOOP_PROJ_CHUNKS, n_chunks // n_row_chunks)
    later_chunks = list(range(n_row_chunks * chunks_per_iter, n_chunks))

    def project(c):
        h_scr[nxt, c] = _dot(xb_scr[...], w_in_ref[c]) + b_in_ref[0, c]

    def project_some(n):
        for _ in range(min(n, len(later_chunks))):
            project(later_chunks.pop(0))

    def conv_rows_and_projection(j):
        r0 = j * CONV_ROWS
        for c0 in range(0, cw, LANES):
            window = conv_ext[pl.ds(r0, span), c0:c0 + LANES]
            acc = jnp.broadcast_to(dwb_ref[0, :, c0:c0 + LANES], (CONV_ROWS, LANES))
            for res in range(SUBLANES):
                offs = [o for o in range(base, base + taps) if o % SUBLANES == res]
                if not offs:
                    continue
                shifted = window if res == 0 else pltpu.roll(window, span - res, axis=0)
                for o in offs:
                    k = o - base
                    acc = acc + dw_ref[0, k:k + 1, c0:c0 + LANES] * shifted[
                        o - res:o - res + CONV_ROWS]
            conv_out[pl.ds(r0, CONV_ROWS), c0:c0 + LANES] = acc
        for m in range(chunks_per_iter):
            project(j * chunks_per_iter + m)

    for j in range(n_row_chunks):
        conv_rows_and_projection(j)
    conv_ext[0:CONV_HALO, :] = conv_ext[ts:ts + CONV_HALO, :]
    hc = _layer_norm(conv_out[...], cg_ref[0], cb_ref[0])
    hc = hc * jax.nn.sigmoid(hc)
    project_some(2)
    y_conv = _dot(hc.astype(BF16), cpw_ref[...]) + cpwb_ref[0]

    u = proj(0, pw)
    pool_ext[POOL_HALO:POOL_HALO + ts, :] = u
    t_glob = i * ts + lax.broadcasted_iota(I32, (ts, 1), 0)
    gd = pw // len(POOL_WINDOWS)
    pooled = []
    for g, w in enumerate(POOL_WINDOWS):
        lo = g * gd
        ug = u[:, lo:lo + gd]
        run = pool_ext[:, lo:lo + gd]
        step = 1
        while step < w:
            run = run + pltpu.roll(run, step, axis=0)
            step *= 2
        cnt = jnp.minimum(t_glob + 1, w).astype(F32)
        pooled.append((run[POOL_HALO:POOL_HALO + ts] / cnt - ug).astype(BF16))
    project_some(2)
    mixed = [_dot(p, pool_w_ref[g]) for g, p in enumerate(pooled)]
    mixed = jnp.concatenate(mixed, axis=1) * pool_scale_ref[0]
    y_pool = _dot(mixed.astype(BF16), pool_proj_ref[...])
    pool_ext[0:POOL_HALO, :] = pool_ext[ts:ts + POOL_HALO, :]

    q = proj(c_q, c_q + aw)
    heads = []
    for h in range(aw // HEAD_DIM):
        lo = h * HEAD_DIM
        sc = _dot(q[:, lo:lo + HEAD_DIM].astype(BF16), kt_ref[0, 0, lo:lo + HEAD_DIM, :])
        sc = sc * (HEAD_DIM ** -0.5)
        p = jnp.exp(sc - jnp.max(sc, axis=-1, keepdims=True))
        o = _dot(p.astype(BF16), v_ref[0, 0, :, lo:lo + HEAD_DIM])
        heads.append(o / jnp.sum(p, axis=-1, keepdims=True))
    y_attn = _dot(jnp.concatenate(heads, axis=1).astype(BF16), ao_ref[...])

    merged = jax.nn.sigmoid(proj(c_gate, c_gate + d)) * y_pool
    merged = merged + jax.nn.sigmoid(proj(c_gate + d, c_gate + 2 * d)) * y_conv
    merged = merged + jax.nn.sigmoid(proj(c_gate + 2 * d, c_gate + 3 * d)) * y_attn
    project_some(2)
    out = _dot(merged.astype(BF16), wout_ref[...]) + bout_ref[0]
    x1 = _layer_norm(alpha * x + out, g1_ref[0], b1_ref[0])
    x1_ref[0] = x1

    x1_hi = x1.astype(BF16)
    x1_lo = (x1 - x1_hi.astype(F32)).astype(BF16)
    logits = _dot(x1_hi, wr_hi_ref[0]) + _dot(x1_lo, wr_hi_ref[0]) + _dot(x1_hi, wr_lo_ref[0])
    logits = logits.T[0:n_e, :] + rb_ref[0]
    e_iota = lax.broadcasted_iota(I32, (n_e, ts), 0)
    rank = jnp.zeros((n_e, ts), F32)
    for other in range(n_e):
        row = logits[other:other + 1, :]
        beats = (row > logits) | ((row == logits) & (e_iota > other))
        rank = rank + jnp.where(beats, 1.0, 0.0)
    hot = jnp.where(rank < TOP_K, 1.0, 0.0)
    top_v, top_i = [], []
    for k in range(TOP_K):
        sel = rank == k
        top_i.append(jnp.sum(jnp.where(sel, e_iota, 0), axis=0, keepdims=True))
        top_v.append(jnp.sum(jnp.where(sel, logits, 0.0), axis=0, keepdims=True))
    ex = [jnp.exp(v - top_v[0]) for v in top_v]
    den = ex[0] + ex[1] + ex[2] + ex[3]
    sub = lax.broadcasted_iota(I32, (SUBLANES, ts), 0)
    route = jnp.zeros((SUBLANES, ts), F32)
    for k in range(TOP_K):
        ids_ref[k:k + 1, :] = top_i[k]
        route = jnp.where(sub == k, top_i[k].astype(F32), route)
        route = jnp.where(sub == TOP_K + k, ex[k] / den, route)
    route_t_ref[...] = jnp.concatenate(
        [route, jnp.zeros((LANES - SUBLANES, ts), F32)], axis=0).T

    hot_wide = jnp.concatenate([hot, jnp.zeros((LANES - n_e, ts), F32)], axis=0).astype(BF16)
    cnt = lax.dot_general(jnp.ones((SUBLANES, ts), BF16), hot_wide, (((1,), (1,)), ((), ())),
                          preferred_element_type=F32)[0:1]
    is_expert = lax.broadcasted_iota(I32, (1, LANES), 1) < n_e
    padded = jnp.where(is_expert, jnp.maximum(jnp.ceil(cnt * (1.0 / SEG_ALIGN)), 1.0), 0.0)
    padded = padded * SEG_ALIGN
    padded = jnp.where(s >= 1, padded, 0.0)
    cnt_scr[pl.ds(tile, 1), :] = padded
    pre_scr[pl.ds(tile, 1), :] = run_scr[0:1, :]
    run_scr[0:1, :] = run_scr[0:1, :] + padded

    project_some(len(later_chunks))

    @pl.when(s == n_tiles)
    def _():
        _routing_tables(cnt_scr[...], pre_scr[...], n_e,
                        seg_off_ref, seg_len_ref, dest_ref, meta_ref)


def _mixer(x, weights, kt, v, layer, alpha):
    n_b, s, d = x.shape
    ts = TOKEN_TILE
    n_s = s // ts
    n_t = n_b * n_s
    names = ("w_in", "b_in", "pool_w", "pool_scale", "pool_proj", "conv_dw", "conv_dw_b",
             "conv_ln_g", "conv_ln_b", "conv_pw", "conv_pw_b")
    names2 = ("attn_o", "w_out", "b_out", "ln1_g", "ln1_b", "router_hi", "router_lo", "router_b")
    in_hbm = ("w_in", "pool_w", "pool_proj", "conv_pw", "attn_o", "w_out")
    pw = weights["pool_proj"].shape[1]
    cw = weights["conv_pw"].shape[1]
    aw = weights["attn_o"].shape[1]
    n_chunks = weights["w_in"].shape[2] // PROJ_CHUNK
    gd = pw // len(POOL_WINDOWS)

    def of_layer(arr):
        tail = (0,) * (arr.ndim - 1)
        return pl.BlockSpec((1,) + arr.shape[1:], lambda st: (layer,) + tail,
                            pipeline_mode=pl.Buffered(1))

    def spec(name):
        if name in in_hbm:
            return pl.BlockSpec(memory_space=pl.ANY)
        return of_layer(weights[name])

    def resident(shape):
        return pl.BlockSpec(shape, lambda st: (0, 0))

    def tile_a(st):
        return jnp.minimum(st, n_t - 1)

    def tile_b(st):
        return jnp.maximum(st - 1, 0)

    in_specs = ([pl.BlockSpec((1, ts, d), lambda st: (tile_a(st) // n_s, tile_a(st) % n_s, 0))]
                + [spec(k) for k in names]
                + [pl.BlockSpec((1, 1) + kt.shape[2:],
                                lambda st: (layer, tile_b(st) // n_s, 0, 0)),
                   pl.BlockSpec((1, 1) + v.shape[2:],
                                lambda st: (layer, tile_b(st) // n_s, 0, 0))]
                + [spec(k) for k in names2])
    out_specs = [
        pl.BlockSpec((1, ts, d), lambda st: (tile_b(st) // n_s, tile_b(st) % n_s, 0)),
        pl.BlockSpec((TOP_K, ts), lambda st: (0, tile_b(st))),
        pl.BlockSpec((ts, LANES), lambda st: (tile_b(st), 0)),
        resident((n_t, LANES)), resident((n_t, LANES)), resident((n_t, LANES)),
        resident((SUBLANES, META_LANES)),
    ]
    out_shape = [
        jax.ShapeDtypeStruct((n_b, s, d), F32),
        jax.ShapeDtypeStruct((TOP_K, n_b * s), I32),
        jax.ShapeDtypeStruct((n_b * s, LANES), F32),
        jax.ShapeDtypeStruct((n_t, LANES), I32),
        jax.ShapeDtypeStruct((n_t, LANES), I32),
        jax.ShapeDtypeStruct((n_t, LANES), I32),
        jax.ShapeDtypeStruct((SUBLANES, META_LANES), I32),
    ]
    return pl.pallas_call(
        functools.partial(_mixer_kernel, alpha=alpha, n_s=n_s, layer=layer),
        grid=(n_t + 1,),
        in_specs=in_specs,
        out_specs=out_specs,
        out_shape=out_shape,
        scratch_shapes=[
            pltpu.VMEM((n_chunks, d, PROJ_CHUNK), BF16),
            pltpu.VMEM((len(POOL_WINDOWS), gd, gd), BF16),
            pltpu.VMEM((pw, d), BF16),
            pltpu.VMEM((cw, d), BF16),
            pltpu.VMEM((aw, d), BF16),
            pltpu.VMEM((d, d), BF16),
            pltpu.VMEM((2, max(d, pw, cw, aw), PROJ_CHUNK), F32),
            pltpu.SemaphoreType.DMA((2,)),
            pltpu.VMEM((2, n_chunks, ts, PROJ_CHUNK), F32),
            pltpu.VMEM((2, ts, d), F32),
            pltpu.VMEM((ts, d), BF16),
            pltpu.VMEM((ts + POOL_HALO, pw), F32),
            pltpu.VMEM((ts + CONV_HALO, cw), F32),
            pltpu.VMEM((ts, cw), F32),
            pltpu.VMEM((n_t, LANES), F32),
            pltpu.VMEM((n_t, LANES), F32),
            pltpu.VMEM((SUBLANES, LANES), F32),
        ],
        compiler_params=pltpu.CompilerParams(
            dimension_semantics=("arbitrary",), vmem_limit_bytes=VMEM_LIMIT),
        name="mixer",
    )(x, *[weights[k] for k in names], kt, v, *[weights[k] for k in names2])


def _start_segment_copies(seg_off_ref, seg_len_ref, dest_ref, tile, n_e, make_copy):
    for e in range(n_e):
        make_copy(pl.multiple_of(seg_off_ref[tile, e], SEG_ALIGN),
                  pl.multiple_of(dest_ref[tile, e], SEG_ALIGN),
                  pl.multiple_of(seg_len_ref[tile, e], SEG_ALIGN)).start(priority=e % 2)


def _wait_segment_copies(seg_off_ref, seg_len_ref, tile, n_e, make_copy):
    total = seg_off_ref[tile, n_e - 1] + seg_len_ref[tile, n_e - 1]
    make_copy(0, 0, pl.multiple_of(total, SEG_ALIGN)).wait()


def _dispatch_kernel(seg_off_ref, seg_len_ref, dest_ref, meta_ref,
                     ids_ref, off_ref, x1_ref, xs_ref, buf, zbuf, sem, zsem, *, n_e):
    tile = pl.program_id(0)
    n_tiles = pl.num_programs(0)
    slot = tile % 2
    ts = x1_ref.shape[0]
    rows = buf.shape[1]

    @pl.when(tile == 0)
    def _():
        zbuf[...] = jnp.zeros(zbuf.shape, U32)

        def zero_copy(e):
            n = pl.multiple_of(meta_ref[META_PAD_LEN, e], SEG_ALIGN)
            dst = pl.multiple_of(meta_ref[META_PAD_DST, e], SEG_ALIGN)
            return n, pltpu.make_async_copy(zbuf.at[pl.ds(0, n)], xs_ref.at[pl.ds(dst, n)], zsem)

        def zstart(e, c):
            n, cp = zero_copy(e)
            pl.when(n > 0)(cp.start)
            return c

        def zwait(e, c):
            n, cp = zero_copy(e)
            pl.when(n > 0)(cp.wait)
            return c

        lax.fori_loop(0, n_e, zstart, 0)
        lax.fori_loop(0, n_e, zwait, 0)

    def copies_from(s):
        def make_copy(off, dst, n):
            return pltpu.make_async_copy(buf.at[s, pl.ds(off, n)], xs_ref.at[pl.ds(dst, n)],
                                         sem.at[s])
        return make_copy

    @pl.when(tile >= 2)
    def _():
        _wait_segment_copies(seg_off_ref, seg_len_ref, tile - 2, n_e, copies_from(slot))

    ids = ids_ref[...]
    e_iota = lax.broadcasted_iota(I32, (n_e, ts), 0)
    hots = [(e_iota == ids[k:k + 1, :]).astype(F32) for k in range(TOP_K)]
    hot = hots[0] + hots[1] + hots[2] + hots[3]
    off_row = off_ref[pl.ds(tile, 1), :].astype(F32)
    eye = (lax.broadcasted_iota(I32, (n_e, LANES), 0)
           == lax.broadcasted_iota(I32, (n_e, LANES), 1))
    off_col = jnp.sum(jnp.where(eye, off_row, 0.0), axis=1, keepdims=True)
    upper = (lax.broadcasted_iota(I32, (ts, ts), 0)
             < lax.broadcasted_iota(I32, (ts, ts), 1)).astype(BF16)
    place = _dot(hot.astype(BF16), upper) + off_col
    pos = [jnp.sum(hots[k] * place, axis=0, keepdims=True).astype(I32)
           for k in range(TOP_K)]
    x1b = x1_ref[...].astype(BF16)
    for r0 in range(0, rows, SORT_ROWS):
        r_iota = r0 + lax.broadcasted_iota(I32, (SORT_ROWS, ts), 0)
        hit = (r_iota == pos[0]) | (r_iota == pos[1]) | (r_iota == pos[2]) | (r_iota == pos[3])
        perm = jnp.where(hit, 1.0, 0.0).astype(BF16)
        buf[slot, r0:r0 + SORT_ROWS] = _pack_pairs(_dot(perm, x1b))
    _start_segment_copies(seg_off_ref, seg_len_ref, dest_ref, tile, n_e, copies_from(slot))

    @pl.when(tile == n_tiles - 1)
    def _():
        @pl.when(tile >= 1)
        def _():
            _wait_segment_copies(seg_off_ref, seg_len_ref, tile - 1, n_e, copies_from(1 - slot))

        _wait_segment_copies(seg_off_ref, seg_len_ref, tile, n_e, copies_from(slot))


def _tile_rows(ts, n_e):
    return TOP_K * ts + n_e * SEG_ALIGN


def _dispatch(tables, meta, ids, x1, n_rows, n_e):
    n_tok, d = x1.shape
    ts = TOKEN_TILE
    n_t = n_tok // ts
    rows = _tile_rows(ts, n_e)
    return pl.pallas_call(
        functools.partial(_dispatch_kernel, n_e=n_e),
        grid_spec=pltpu.PrefetchScalarGridSpec(
            num_scalar_prefetch=4,
            grid=(n_t,),
            in_specs=[
                pl.BlockSpec((TOP_K, ts), lambda t, *_: (0, t)),
                pl.BlockSpec((n_t, LANES), lambda t, *_: (0, 0)),
                pl.BlockSpec((ts, d), lambda t, *_: (t, 0)),
            ],
            out_specs=pl.BlockSpec(memory_space=pl.ANY),
            scratch_shapes=[pltpu.VMEM((2, rows, d // 2), U32),
                            pltpu.VMEM((EXPERT_BLOCK, d // 2), U32),
                            pltpu.SemaphoreType.DMA((2,)), pltpu.SemaphoreType.DMA(())],
        ),
        out_shape=jax.ShapeDtypeStruct((n_rows, d // 2), U32),
        compiler_params=pltpu.CompilerParams(
            dimension_semantics=("arbitrary",), vmem_limit_bytes=VMEM_LIMIT),
        name="dispatch",
    )(*tables, meta, ids, tables[0], x1)


def _expert_kernel(meta_ref, xs_ref, wup_ref, bup_ref, wdn_ref, bdn_ref,
                   ys_ref, wup_bf, wdn_bf):
    j = pl.program_id(0)
    f = wdn_ref.shape[2]

    @pl.when(j < meta_ref[META_N_USED, 0])
    def _():
        prev = meta_ref[META_BLK_E, jnp.maximum(j - 1, 0)]

        @pl.when((j == 0) | (meta_ref[META_BLK_E, j] != prev))
        def _():
            wup_bf[...] = wup_ref[0, 0].astype(BF16)
            wdn_bf[...] = wdn_ref[0, 0].astype(BF16)

        def expert_rows(n_rows):
            x = _unpack_pairs(xs_ref[0:n_rows, :])

            def up(c0):
                glu = _dot(x, wup_bf[:, c0:c0 + FF_CHUNK]) + bup_ref[0, 0, :, c0:c0 + FF_CHUNK]
                lin = (_dot(x, wup_bf[:, f + c0:f + c0 + FF_CHUNK])
                       + bup_ref[0, 0, :, f + c0:f + c0 + FF_CHUNK])
                return glu, lin

            y = jnp.broadcast_to(bdn_ref[0, 0], (n_rows, wdn_ref.shape[3]))
            nxt = up(0)
            for c0 in range(0, f, FF_CHUNK):
                glu, lin = nxt
                if c0 + FF_CHUNK < f:
                    nxt = up(c0 + FF_CHUNK)
                glu = jnp.minimum(glu, SWIGLU_LIMIT)
                lin = jnp.clip(lin, -SWIGLU_LIMIT, SWIGLU_LIMIT)
                act = glu * jax.nn.sigmoid(SWIGLU_ALPHA * glu) * (lin + 1.0)
                y = y + _dot(act.astype(BF16), wdn_bf[c0:c0 + FF_CHUNK, :])
            ys_ref[0:n_rows, :] = _pack_pairs(y.astype(BF16).astype(F32))

        bm = xs_ref.shape[0]
        part = bm // EXPERT_PARTS
        real_rows = meta_ref[META_REAL_ROWS, j]
        for q in range(1, EXPERT_PARTS + 1):
            pl.when((real_rows > (q - 1) * part) & (real_rows <= q * part))(
                functools.partial(expert_rows, q * part))


def _experts(meta, xs, w_up, b_up, w_down, b_down, layer):
    n_rows, half_d = xs.shape
    d = 2 * half_d
    f2 = w_up.shape[-1]
    f = f2 // 2
    bm = EXPERT_BLOCK
    n_blk = n_rows // bm

    def last_used(j, meta):
        return jnp.maximum(jnp.minimum(j, meta[META_N_USED, 0] - 1), 0)

    def row_blk(j, meta):
        return (last_used(j, meta), 0)

    def w_blk(j, meta):
        return (layer, meta[META_BLK_E, last_used(j, meta)], 0, 0)

    return pl.pallas_call(
        _expert_kernel,
        grid_spec=pltpu.PrefetchScalarGridSpec(
            num_scalar_prefetch=1,
            grid=(n_blk,),
            in_specs=[
                pl.BlockSpec((bm, half_d), row_blk),
                pl.BlockSpec((1, 1, d, f2), w_blk),
                pl.BlockSpec((1, 1, 1, f2), w_blk),
                pl.BlockSpec((1, 1, f, d), w_blk),
                pl.BlockSpec((1, 1, 1, d), w_blk),
            ],
            out_specs=pl.BlockSpec((bm, half_d), row_blk),
            scratch_shapes=[pltpu.VMEM((d, f2), BF16), pltpu.VMEM((f, d), BF16)],
        ),
        out_shape=jax.ShapeDtypeStruct((n_rows, half_d), U32),
        compiler_params=pltpu.CompilerParams(
            dimension_semantics=("arbitrary",), vmem_limit_bytes=VMEM_LIMIT),
        name="experts",
    )(meta, xs, w_up, b_up, w_down, b_down)


def _combine_kernel(seg_off_ref, seg_len_ref, dest_ref,
                    route_ref, off_ref, x1_ref, g2_ref, b2_ref, ys_ref,
                    out_ref, buf, sem, *, alpha, n_e):
    tile = pl.program_id(0)
    n_tiles = pl.num_programs(0)
    slot = tile % 2
    ts = x1_ref.shape[0]
    rows = buf.shape[1]

    def copies_into(s):
        def make_copy(off, dst, n):
            return pltpu.make_async_copy(ys_ref.at[pl.ds(dst, n)], buf.at[s, pl.ds(off, n)],
                                         sem.at[s])
        return make_copy

    @pl.when(tile == 0)
    def _():
        buf[...] = jnp.zeros(buf.shape, U32)
        _start_segment_copies(seg_off_ref, seg_len_ref, dest_ref, tile, n_e, copies_into(slot))

    nxt = jnp.minimum(tile + 1, n_tiles - 1)
    _start_segment_copies(seg_off_ref, seg_len_ref, dest_ref, nxt, n_e, copies_into(1 - slot))

    route = route_ref[...]
    l_iota = lax.broadcasted_iota(I32, (ts, LANES), 1)
    hots = [(l_iota == route[:, k:k + 1].astype(I32)).astype(F32) for k in range(TOP_K)]
    hot = hots[0] + hots[1] + hots[2] + hots[3]
    lower = (lax.broadcasted_iota(I32, (ts, ts), 1)
             < lax.broadcasted_iota(I32, (ts, ts), 0)).astype(BF16)
    place = _dot(lower, hot.astype(BF16)) + off_ref[pl.ds(tile, 1), :].astype(F32)
    pos = [jnp.sum(hots[k] * place, axis=1, keepdims=True).astype(I32)
           for k in range(TOP_K)]

    _wait_segment_copies(seg_off_ref, seg_len_ref, tile, n_e, copies_into(slot))
    y = jnp.zeros(out_ref.shape, F32)
    for r0 in range(0, rows, PERM_ROWS):
        r_iota = r0 + lax.broadcasted_iota(I32, (ts, PERM_ROWS), 1)
        weight = jnp.zeros((ts, PERM_ROWS), F32)
        for k in range(TOP_K):
            weight = jnp.where(r_iota == pos[k], route[:, TOP_K + k:TOP_K + k + 1], weight)
        y = y + _dot(weight.astype(BF16), _unpack_pairs(buf[slot, r0:r0 + PERM_ROWS]))
    out_ref[...] = _layer_norm(alpha * x1_ref[...] + y, g2_ref[0], b2_ref[0])

    @pl.when(tile == n_tiles - 1)
    def _():
        _wait_segment_copies(seg_off_ref, seg_len_ref, tile, n_e, copies_into(1 - slot))


def _combine(tables, route_t, x1, g2, b2, ys, layer, alpha, n_e):
    n_tok, d = x1.shape
    ts = TOKEN_TILE
    n_t = n_tok // ts
    rows = _tile_rows(ts, n_e)
    return pl.pallas_call(
        functools.partial(_combine_kernel, alpha=alpha, n_e=n_e),
        grid_spec=pltpu.PrefetchScalarGridSpec(
            num_scalar_prefetch=3,
            grid=(n_t,),
            in_specs=[
                pl.BlockSpec((ts, LANES), lambda t, *_: (t, 0)),
                pl.BlockSpec((n_t, LANES), lambda t, *_: (0, 0)),
                pl.BlockSpec((ts, d), lambda t, *_: (t, 0)),
                pl.BlockSpec((1, 1, d), lambda t, *_: (layer, 0, 0)),
                pl.BlockSpec((1, 1, d), lambda t, *_: (layer, 0, 0)),
                pl.BlockSpec(memory_space=pl.ANY),
            ],
            out_specs=pl.BlockSpec((ts, d), lambda t, *_: (t, 0)),
            scratch_shapes=[pltpu.VMEM((2, rows, d // 2), U32), pltpu.SemaphoreType.DMA((2,))],
        ),
        out_shape=jax.ShapeDtypeStruct((n_tok, d), F32),
        compiler_params=pltpu.CompilerParams(
            dimension_semantics=("arbitrary",), vmem_limit_bytes=VMEM_LIMIT),
        name="combine",
    )(*tables, route_t, tables[0], x1, g2, b2, ys)


def kernel(x, mem, mem_ln_g, mem_ln_b, w_in, b_in, pool_w, pool_scale, pool_proj, conv_dw, conv_dw_b, conv_ln_g, conv_ln_b, conv_pw, conv_pw_b, w_kv, attn_o, w_out, b_out, ln1_g, ln1_b, router_w, router_b, exp_up, exp_up_b, exp_down, exp_down_b, ln2_g, ln2_b):
    n_b, s, d = x.shape
    depth = w_in.shape[0]
    n_e = router_w.shape[-1]
    n_tok = n_b * s
    n_t = n_tok // TOKEN_TILE
    alpha = (2.0 * depth) ** 0.25
    n_rows = TOP_K * n_tok + n_t * n_e * SEG_ALIGN + n_e * EXPERT_BLOCK
    n_rows = -(-n_rows // EXPERT_BLOCK) * EXPERT_BLOCK
    assert n_rows // EXPERT_BLOCK <= META_LANES and n_e <= LANES

    def rows3(a):
        return a.reshape(a.shape[0], 1, a.shape[1])

    in_cols = w_in.shape[-1]
    assert in_cols % PROJ_CHUNK == 0
    n_chunks = in_cols // PROJ_CHUNK
    router_pad = jnp.pad(router_w, ((0, 0), (0, 0), (0, LANES - n_e)))
    router_hi = router_pad.astype(BF16)
    weights = dict(
        w_in=w_in, b_in=b_in.reshape(depth, n_chunks, 1, PROJ_CHUNK), pool_w=pool_w,
        pool_scale=rows3(pool_scale), pool_proj=pool_proj, conv_dw=conv_dw,
        conv_dw_b=rows3(conv_dw_b), conv_ln_g=rows3(conv_ln_g), conv_ln_b=rows3(conv_ln_b),
        conv_pw=conv_pw, conv_pw_b=rows3(conv_pw_b), attn_o=attn_o,
        w_out=w_out, b_out=rows3(b_out), ln1_g=rows3(ln1_g), ln1_b=rows3(ln1_b),
        router_hi=router_hi, router_lo=(router_pad - router_hi.astype(F32)).astype(BF16),
        router_b=router_b.reshape(depth, n_e, 1))
    up_b = exp_up_b.reshape(depth, n_e, 1, exp_up_b.shape[-1])
    down_b = exp_down_b.reshape(depth, n_e, 1, d)
    g2, b2 = rows3(ln2_g), rows3(ln2_b)

    kt_all, v_all = _memory_kv(mem, mem_ln_g, mem_ln_b, w_kv)
    for layer in range(depth):
        x1, ids, route_t, seg_off, seg_len, dest, meta = _mixer(
            x, weights, kt_all, v_all, layer, alpha)
        x1 = x1.reshape(n_tok, d)
        tables = (seg_off, seg_len, dest)
        xs = _dispatch(tables, meta, ids, x1, n_rows, n_e)
        ys = _experts(meta, xs, exp_up, up_b, exp_down, down_b, layer)
        x = _combine(tables, route_t, x1, g2, b2, ys, layer, alpha, n_e).reshape(n_b, s, d)
    return x
```
